```python
import math
import functools
import jax
import jax.numpy as jnp
from jax import lax
import numpy as np

D_MODEL = 2048
BATCH = 4
SEQ = 2048
DEPTH = 2
DEC_BATCH = 128
DEC_SEQ = 1
PAST_LEN = 2048
PAGE_SIZE = 128

A_HEADS = 8
A_KV = 2
A_HPG = A_HEADS // A_KV
A_DH = 128
A_WIDTH = A_HEADS * A_DH
A_KVW = A_KV * A_DH
CMP_BLOCK = 32
SEL_BLOCK = 64
CMP_PER_SEL = SEL_BLOCK // CMP_BLOCK
N_SEL = 16
WINDOW = 512
Q_BLOCK = 128
SEL_Q_BLOCK = 64
ROT_DIM = A_DH // 4
ROPE_THETA = 500000.0
ATT_SCALE = A_DH ** -0.5
FORCE_BONUS = 1.0e4
S_GROUPS = 32
S_CH = 16
S_WIDTH = S_GROUPS * S_CH
S_STATE = 64
R_HEADS = 8
R_DH = 64
R_WIDTH = R_HEADS * R_DH
R_DECAY_LORA = 64
R_A_LORA = 64
R_GATE_LORA = 128
R_IN = 3 * R_WIDTH + R_DECAY_LORA + R_A_LORA + R_GATE_LORA
GN_EPS = 64e-5
D_FF = 5632
OFF_KV = A_WIDTH
OFF_GATE = OFF_KV + 6 * A_KVW
OFF_SSM = OFF_GATE + 3 * A_HEADS
OFF_RWKV = OFF_SSM + S_WIDTH
D_IN = OFF_RWKV + R_IN
D_MIX = A_WIDTH + S_WIDTH + R_WIDTH
F32 = jnp.float32

kernel_name = 'hybrid_nsa_s5_rwkv7_decode_step'


def rms_norm(x, g, eps=1e-6):
    xf = x.astype(F32)
    y = xf * lax.rsqrt(jnp.mean(xf * xf, axis=-1, keepdims=True) + eps)
    return y * g.astype(F32)


def masked_softmax(s, mask, axis):
    s = jnp.where(mask, s.astype(F32), -jnp.inf)
    m = jnp.max(s, axis=axis, keepdims=True)
    m = jnp.where(jnp.isfinite(m), m, 0.0)
    e = jnp.where(mask, jnp.exp(s - m), 0.0)
    return e / jnp.maximum(jnp.sum(e, axis=axis, keepdims=True), 1e-30)


def rope_partial(x, pos):
    half = ROT_DIM // 2
    inv = ROPE_THETA ** (-2.0 * jnp.arange(half, dtype=F32) / ROT_DIM)
    ang = pos.astype(F32)[:, None] * inv[None, :]
    cos = jnp.cos(ang)[:, None, :]
    sin = jnp.sin(ang)[:, None, :]
    x1, x2, rest = x[..., :half], x[..., half:ROT_DIM], x[..., ROT_DIM:]
    return jnp.concatenate([x1 * cos - x2 * sin, x2 * cos + x1 * sin, rest.astype(F32)], axis=-1)


def pad_rows(x, mult):
    extra = (-x.shape[1]) % mult
    if extra == 0:
        return x
    return jnp.pad(x, [(0, 0), (0, extra)] + [(0, 0)] * (x.ndim - 2))


def swiglu(h, w1, w3, w2):
    return (jax.nn.silu(h @ w1) * (h @ w3)) @ w2


def compress_kv(rows, pe, w1, w2):
    rows = pad_rows(rows, CMP_BLOCK)
    b, l = rows.shape[:2]
    nc = l // CMP_BLOCK
    blk = rows.reshape(b, nc, CMP_BLOCK, 2, A_KV, A_DH) + jnp.transpose(pe, (1, 0, 2))[None, None, :, :, None, :]
    flat = jnp.transpose(blk, (0, 1, 3, 4, 2, 5)).reshape(b, nc, 2, A_KV, CMP_BLOCK * A_DH)
    hid = jax.nn.gelu(jnp.einsum('bcsgf,sfe->bcsge', flat, w1))
    out = jnp.einsum('bcsge,sed->bcsgd', hid, w2)
    return out[:, :, 0], out[:, :, 1]


def nsa_heads(za, pos, q_gain, k_gain):
    b, t, _ = za.shape
    q = rms_norm(za[..., :A_WIDTH].reshape(b, t, A_HEADS, A_DH), q_gain)
    kv = za[..., OFF_KV:OFF_GATE].reshape(b, t, 6, A_KV, A_DH)
    gates = jax.nn.sigmoid(za[..., OFF_GATE:].astype(F32)).reshape(b, t, 3, A_KV, A_HPG, 1)
    q_rot = rope_partial(q, pos).reshape(b, t, A_KV, A_HPG, A_DH)
    k_slc = rope_partial(rms_norm(kv[:, :, 2], k_gain[1]), pos)
    k_win = rope_partial(rms_norm(kv[:, :, 4], k_gain[2]), pos)
    cmp_rows = kv[:, :, 0:2]
    slc_rows = jnp.stack([k_slc, kv[:, :, 3].astype(F32)], axis=2)
    win_rows = jnp.stack([k_win, kv[:, :, 5].astype(F32)], axis=2)
    return q.reshape(b, t, A_KV, A_HPG, A_DH), q_rot, cmp_rows, slc_rows, win_rows, gates


def nsa_compressed(q, ck, cv, pos, k_gain_c):
    ck = rms_norm(ck, k_gain_c)
    nc = ck.shape[1]
    s = jnp.einsum('btghd,bcgd->btghc', q, ck) * ATT_SCALE
    blk_end = (jnp.arange(nc) + 1) * CMP_BLOCK - 1
    valid = (blk_end[None, :] <= pos[:, None])[None, :, None, None, :]
    p = masked_softmax(s, valid, -1)
    return jnp.einsum('btghc,bcgd->btghd', p, cv), p


def nsa_select(p, pos):
    imp = jnp.sum(p, axis=3)
    nc = imp.shape[-1]
    ns = -(-nc // CMP_PER_SEL)
    imp = jnp.pad(imp, ((0, 0), (0, 0), (0, 0), (0, ns * CMP_PER_SEL - nc)))
    imp = imp.reshape(imp.shape[:-1] + (ns, CMP_PER_SEL)).sum(-1)
    blk = jnp.arange(ns)[None, :]
    cur = (pos // SEL_BLOCK)[:, None]
    forced = (blk == 0) | (blk == cur) | (blk == cur - 1)
    valid = blk * SEL_BLOCK <= pos[:, None]
    score = jnp.where(valid[None, :, None, :], imp + jnp.where(forced, FORCE_BONUS, 0.0)[None, :, None, :], -1e9)
    _, idx = lax.top_k(score, min(N_SEL, ns))
    return idx


def nsa_selected_attend(q_rot, ks, vs, key_pos, pos):
    s = jnp.einsum('btghd,btgkrd->btghkr', q_rot, ks) * ATT_SCALE
    mask = (key_pos <= pos[None, :, None, None, None])[:, :, :, None]
    p = masked_softmax(s, mask, (-2, -1))
    return jnp.einsum('btghkr,btgkrd->btghd', p, vs)


def nsa_selected_prompt(q_rot, slc_rows, idx, pos):
    b, t = q_rot.shape[:2]
    n = idx.shape[-1]
    ns = t // SEL_BLOCK
    kb = jnp.transpose(slc_rows.reshape(b, ns, SEL_BLOCK, 2, A_KV, A_DH), (0, 4, 1, 2, 3, 5))
    nqb = t // SEL_Q_BLOCK
    qb = jnp.swapaxes(q_rot.reshape(b, nqb, SEL_Q_BLOCK, A_KV, A_HPG, A_DH), 0, 1)
    ib = jnp.swapaxes(idx.reshape(b, nqb, SEL_Q_BLOCK, A_KV, n), 0, 1)
    pb = pos.reshape(nqb, SEL_Q_BLOCK)
    bi = jnp.arange(b)[:, None, None, None]
    gi = jnp.arange(A_KV)[None, None, :, None]

    def one_block(args):
        qc, ic, pc = args
        rows = kb[bi, gi, ic]
        key_pos = ic[..., None] * SEL_BLOCK + jnp.arange(SEL_BLOCK)
        return nsa_selected_attend(qc, rows[..., 0, :], rows[..., 1, :], key_pos, pc)

    out = lax.map(one_block, (qb, ib, pb))
    return jnp.swapaxes(out, 0, 1).reshape(b, t, A_KV, A_HPG, A_DH)


def nsa_selected_sample(q_rot, slc_rows, idx, pos, cache_slc, layer, page_table):
    b = q_rot.shape[0]
    n_past_blk = page_table.shape[1] * PAGE_SIZE // SEL_BLOCK
    blk_per_page = PAGE_SIZE // SEL_BLOCK
    bi = jnp.arange(b)[:, None, None, None]
    gi = jnp.arange(A_KV)[None, None, :, None]
    r = jnp.arange(SEL_BLOCK)
    pidx = jnp.minimum(idx, n_past_blk - 1)
    phys = page_table[bi, pidx // blk_per_page]
    row = (pidx % blk_per_page)[..., None] * SEL_BLOCK + r
    past = cache_slc[layer, phys[..., None], row, :, gi[..., None], :]
    new = pad_rows(slc_rows, SEL_BLOCK)
    nbn = new.shape[1] // SEL_BLOCK
    new = jnp.transpose(new.reshape(b, nbn, SEL_BLOCK, 2, A_KV, A_DH), (0, 4, 1, 2, 3, 5))
    fresh = new[bi, gi, jnp.clip(idx - n_past_blk, 0, nbn - 1)]
    rows = jnp.where((idx < n_past_blk)[..., None, None, None], past, fresh)
    key_pos = idx[..., None] * SEL_BLOCK + r
    return nsa_selected_attend(q_rot, rows[..., 0, :], rows[..., 1, :], key_pos, pos)


def nsa_window_prompt(q_rot, win_rows, pos):
    b, t = q_rot.shape[:2]
    nqb = t // Q_BLOCK
    nwb = -(-WINDOW // Q_BLOCK)
    padded = jnp.pad(win_rows, ((0, 0), (nwb * Q_BLOCK, 0), (0, 0), (0, 0), (0, 0)))
    blocks = padded.reshape(b, nqb + nwb, Q_BLOCK, 2, A_KV, A_DH)
    band_idx = jnp.arange(nqb)[:, None] + jnp.arange(nwb + 1)[None, :]
    band = blocks[:, band_idx].reshape(b, nqb, (nwb + 1) * Q_BLOCK, 2, A_KV, A_DH)
    kpos = (band_idx[:, :, None] * Q_BLOCK + jnp.arange(Q_BLOCK) - nwb * Q_BLOCK).reshape(nqb, -1)
    qb = q_rot.reshape(b, nqb, Q_BLOCK, A_KV, A_HPG, A_DH)
    qpos = pos.reshape(nqb, Q_BLOCK)
    s = jnp.einsum('bqtghd,bqkgd->bqtghk', qb, band[:, :, :, 0]) * ATT_SCALE
    kp = kpos[:, None, :]
    qp = qpos[:, :, None]
    mask = ((kp <= qp) & (kp > qp - WINDOW) & (kp >= 0))[None, :, :, None, None, :]
    p = masked_softmax(s, mask, -1)
    o = jnp.einsum('bqtghk,bqkgd->bqtghd', p, band[:, :, :, 1])
    return o.reshape(b, t, A_KV, A_HPG, A_DH)


def nsa_window_sample(q_rot, win_buf, win_rows, pos):
    wb = win_buf.shape[1]
    keys = jnp.concatenate([win_buf.astype(F32), win_rows], axis=1)
    kpos = PAST_LEN - wb + jnp.arange(keys.shape[1])
    s = jnp.einsum('btghd,bkgd->btghk', q_rot, keys[:, :, 0]) * ATT_SCALE
    kp = kpos[None, :]
    qp = pos[:, None]
    mask = ((kp <= qp) & (kp > qp - WINDOW))[None, :, None, None, :]
    p = masked_softmax(s, mask, -1)
    o = jnp.einsum('btghk,bkgd->btghd', p, keys[:, :, 1])
    return o, keys[:, keys.shape[1] - wb:]


def nsa_prompt(q, q_rot, cmp_rows, slc_rows, win_rows, pos, lp):
    ck, cv = compress_kv(cmp_rows, lp['cmp_pe'], lp['cmp_w1'], lp['cmp_w2'])
    o_cmp, p_cmp = nsa_compressed(q, ck, cv, pos, lp['k_gain'][0])
    idx = nsa_select(p_cmp, pos)
    o_slc = nsa_selected_prompt(q_rot, slc_rows, idx, pos)
    o_win = nsa_window_prompt(q_rot, win_rows, pos)
    keep = min(WINDOW, win_rows.shape[1])
    return o_cmp, o_slc, o_win, win_rows[:, win_rows.shape[1] - keep:]


def nsa_sample(q, q_rot, cmp_rows, slc_rows, win_rows, pos, lp, cache_cmp, cache_slc, win_buf, page_table, layer):
    b = q.shape[0]
    past = cache_cmp[layer, page_table].reshape(b, -1, 2, A_KV, A_DH)
    ckp, cvp = compress_kv(past, lp['cmp_pe'], lp['cmp_w1'], lp['cmp_w2'])
    ckn, cvn = compress_kv(cmp_rows, lp['cmp_pe'], lp['cmp_w1'], lp['cmp_w2'])
    ck = jnp.concatenate([ckp, ckn], axis=1)
    cv = jnp.concatenate([cvp, cvn], axis=1)
    o_cmp, p_cmp = nsa_compressed(q, ck, cv, pos, lp['k_gain'][0])
    idx = nsa_select(p_cmp, pos)
    o_slc = nsa_selected_sample(q_rot, slc_rows, idx, pos, cache_slc, layer, page_table)
    o_win, new_buf = nsa_window_sample(q_rot, win_buf, win_rows, pos)
    return o_cmp, o_slc, o_win, new_buf


def complex_affine_combine(e1, e2):
    a1r, a1i, b1r, b1i = e1
    a2r, a2i, b2r, b2i = e2
    return (a2r * a1r - a2i * a1i, a2r * a1i + a2i * a1r,
            a2r * b1r - a2i * b1i + b2r, a2r * b1i + a2i * b1r + b2i)


def ssm_mixer(u, h0, lp):
    b, t, _ = u.shape
    uf = u.astype(F32).reshape(b, t, S_GROUPS, S_CH)
    a_re = lp['ssm_a_re'].astype(F32)
    a_im = lp['ssm_a_im'].astype(F32)
    dt = jnp.exp(lp['ssm_log_dt'].astype(F32))[:, None]
    mag = jnp.exp(a_re * dt)
    lam_re = mag * jnp.cos(a_im * dt)
    lam_im = mag * jnp.sin(a_im * dt)
    den = a_re * a_re + a_im * a_im
    co_re = ((lam_re - 1.0) * a_re + lam_im * a_im) / den
    co_im = (lam_im * a_re - (lam_re - 1.0) * a_im) / den
    b_re = lp['ssm_b_re'].astype(F32)
    b_im = lp['ssm_b_im'].astype(F32)
    bb_re = co_re[..., None] * b_re - co_im[..., None] * b_im
    bb_im = co_re[..., None] * b_im + co_im[..., None] * b_re
    x_re = jnp.einsum('btgc,gpc->btgp', uf, bb_re)
    x_im = jnp.einsum('btgc,gpc->btgp', uf, bb_im)
    h0 = h0.astype(F32)
    x_re = x_re.at[:, 0].add(lam_re * h0[..., 0] - lam_im * h0[..., 1])
    x_im = x_im.at[:, 0].add(lam_re * h0[..., 1] + lam_im * h0[..., 0])
    lr = jnp.broadcast_to(lam_re, x_re.shape)
    li = jnp.broadcast_to(lam_im, x_re.shape)
    _, _, h_re, h_im = lax.associative_scan(complex_affine_combine, (lr, li, x_re, x_im), axis=1)
    y = (jnp.einsum('gcp,btgp->btgc', lp['ssm_c_re'].astype(F32), h_re)
         - jnp.einsum('gcp,btgp->btgc', lp['ssm_c_im'].astype(F32), h_im)
         + lp['ssm_d'] * uf)
    y = jax.nn.gelu(y.reshape(b, t, S_WIDTH))
    out = y * jax.nn.sigmoid(y @ lp['ssm_w_glu'] + lp['ssm_b_glu'])
    return out, jnp.stack([h_re[:, -1], h_im[:, -1]], axis=-1)


def rwkv_mixer(zr, shift0, wkv0, lp):
    b, t, _ = zr.shape
    zf = zr.astype(F32)
    prev = jnp.concatenate([shift0[:, None].astype(F32), zf[:, :-1]], axis=1)
    zm = zf + (prev - zf) * lp['rwkv_mu']
    cuts = [R_WIDTH, 2 * R_WIDTH, 3 * R_WIDTH, 3 * R_WIDTH + R_DECAY_LORA, 3 * R_WIDTH + R_DECAY_LORA + R_A_LORA]
    r, k, v, wl, al, gl = jnp.split(zm, cuts, axis=-1)
    w = -jax.nn.softplus(-(lp['rwkv_w0'] + jnp.tanh(wl) @ lp['rwkv_w2'])) - 0.5
    decay = jnp.exp(-jnp.exp(w))
    a = jax.nn.sigmoid(lp['rwkv_a0'] + al @ lp['rwkv_a2'])
    g = jax.nn.sigmoid(gl) @ lp['rwkv_g2']

    def heads(y):
        return y.reshape(b, t, R_HEADS, R_DH)

    kk = heads(k * lp['rwkv_k_k'])
    kk = kk / jnp.maximum(jnp.sqrt(jnp.sum(kk * kk, axis=-1, keepdims=True)), 1e-12)
    k = k * (1.0 + (a - 1.0) * lp['rwkv_k_a'])
    r_h, w_h, k_h, v_h, a_h = heads(r), heads(decay), heads(k), heads(v), heads(a)

    def step(state, inp):
        r_t, w_t, k_t, v_t, kk_t, a_t = inp
        removal = jnp.einsum('bhij,bhj->bhi', state, kk_t)
        state = (state * w_t[:, :, None, :] - removal[..., None] * (kk_t * a_t)[:, :, None, :]
                 + v_t[..., None] * k_t[:, :, None, :])
        return state, jnp.einsum('bhij,bhj->bhi', state, r_t)

    seq = tuple(jnp.swapaxes(y, 0, 1) for y in (r_h, w_h, k_h, v_h, kk, a_h))
    wkv_last, o = lax.scan(step, wkv0.astype(F32), seq)
    o = jnp.swapaxes(o, 0, 1)
    mean = jnp.mean(o, axis=-1, keepdims=True)
    var = jnp.mean(jnp.square(o - mean), axis=-1, keepdims=True)
    o = ((o - mean) * lax.rsqrt(var + GN_EPS)).reshape(b, t, R_WIDTH) * lp['rwkv_ln_g'] + lp['rwkv_ln_b']
    bonus = jnp.sum(r_h * k_h * lp['rwkv_r_k'], axis=-1, keepdims=True) * v_h
    o = (o + bonus.reshape(b, t, R_WIDTH)) * g
    return o, zr[:, -1], wkv_last


def layer_forward(x, c, pos, lp, nsa_attend, ssm_h0, shift0, wkv0):
    b, t, d = x.shape
    mod = (jax.nn.silu(c.astype(F32)) @ lp['w_ada'] + lp['b_ada']).reshape(b, 1, 9, d)

    def normed(h, i):
        return rms_norm(h, lp['norm_g'][i]) * (1.0 + mod[:, :, 3 * i + 1]) + mod[:, :, 3 * i]

    h = x + 0.5 * mod[:, :, 2] * swiglu(normed(x, 0), lp['ffn_w1'][0], lp['ffn_w3'][0], lp['ffn_w2'][0])
    z = normed(h, 1) @ lp['w_in']
    q, q_rot, cmp_rows, slc_rows, win_rows, gates = nsa_heads(z[..., :OFF_SSM], pos, lp['q_gain'], lp['k_gain'])
    o_cmp, o_slc, o_win, win_state = nsa_attend(q, q_rot, cmp_rows, slc_rows, win_rows, pos, lp)
    o_nsa = (gates[:, :, 0] * o_cmp + gates[:, :, 1] * o_slc + gates[:, :, 2] * o_win).reshape(b, t, A_WIDTH)
    o_ssm, ssm_state = ssm_mixer(z[..., OFF_SSM:OFF_RWKV], ssm_h0, lp)
    o_rwkv, shift_state, wkv_state = rwkv_mixer(z[..., OFF_RWKV:], shift0, wkv0, lp)
    h = h + mod[:, :, 5] * (jnp.concatenate([o_nsa, o_ssm, o_rwkv], axis=-1) @ lp['w_out'])
    y = h + 0.5 * mod[:, :, 8] * swiglu(normed(h, 2), lp['ffn_w1'][1], lp['ffn_w3'][1], lp['ffn_w2'][1])
    return y, (cmp_rows, slc_rows, win_state, ssm_state, shift_state, wkv_state)


def setup_inputs(seed: int = 0) -> dict:
    key = jax.random.key(seed)
    keys = iter(jax.random.split(key, 96))

    def nrm(shape, scale=1.0):
        return jax.random.normal(next(keys), shape, F32) * scale

    def near_one(shape):
        return 1.0 + nrm(shape, 0.05)

    n_pages = PAST_LEN // PAGE_SIZE
    n_pool = (5 * DEC_BATCH * n_pages + 3) // 4
    win_buf = min(WINDOW, PAST_LEN)
    perm = jax.random.permutation(next(keys), n_pool)
    return {
        'x_prompt': nrm((BATCH, SEQ, D_MODEL)),
        'x_sample': nrm((DEC_BATCH, DEC_SEQ, D_MODEL)),
        'cache_nsa_cmp': nrm((DEPTH, n_pool, PAGE_SIZE, 2, A_KV, A_DH)),
        'cache_nsa_slc': nrm((DEPTH, n_pool, PAGE_SIZE, 2, A_KV, A_DH)),
        'cache_nsa_win': nrm((DEPTH, DEC_BATCH, win_buf, 2, A_KV, A_DH)),
        'state_ssm': nrm((DEPTH, DEC_BATCH, S_GROUPS, S_STATE, 2), 0.1),
        'state_rwkv_shift': nrm((DEPTH, DEC_BATCH, R_IN)),
        'state_rwkv_wkv': nrm((DEPTH, DEC_BATCH, R_HEADS, R_DH, R_DH), 0.5),
        'page_table': perm[: DEC_BATCH * n_pages].reshape(DEC_BATCH, n_pages).astype(jnp.int32),
        'c_prompt': nrm((BATCH, D_MODEL)),
        'c_sample': nrm((DEC_BATCH, D_MODEL)),
        'norm_g': near_one((DEPTH, 3, D_MODEL)),
        'w_ada': nrm((DEPTH, D_MODEL, 9 * D_MODEL), 0.5 * D_MODEL ** -0.5),
        'b_ada': nrm((DEPTH, 9 * D_MODEL), 0.01),
        'ffn_w1': nrm((DEPTH, 2, D_MODEL, D_FF), D_MODEL ** -0.5),
        'ffn_w3': nrm((DEPTH, 2, D_MODEL, D_FF), D_MODEL ** -0.5),
        'ffn_w2': nrm((DEPTH, 2, D_FF, D_MODEL), D_FF ** -0.5),
        'w_in': nrm((DEPTH, D_MODEL, D_IN), D_MODEL ** -0.5),
        'w_out': nrm((DEPTH, D_MIX, D_MODEL), D_MIX ** -0.5),
        'nsa_q_gain': near_one((DEPTH, A_DH)),
        'nsa_k_gain': near_one((DEPTH, 3, A_DH)),
        'nsa_cmp_pe': nrm((DEPTH, 2, CMP_BLOCK, A_DH), 0.5),
        'nsa_cmp_w1': nrm((DEPTH, 2, CMP_BLOCK * A_DH, A_DH), (CMP_BLOCK * A_DH) ** -0.5),
        'nsa_cmp_w2': nrm((DEPTH, 2, A_DH, A_DH), A_DH ** -0.5),
        'ssm_a_re': -0.5 * jnp.exp(nrm((DEPTH, S_GROUPS, S_STATE), 0.05)),
        'ssm_a_im': jnp.pi * jnp.arange(S_STATE, dtype=F32) + nrm((DEPTH, S_GROUPS, S_STATE), 0.01),
        'ssm_log_dt': jax.random.uniform(next(keys), (DEPTH, S_GROUPS), F32, math.log(1e-3), math.log(1e-1)),
        'ssm_b_re': nrm((DEPTH, S_GROUPS, S_STATE, S_CH), (2 * S_CH) ** -0.5),
        'ssm_b_im': nrm((DEPTH, S_GROUPS, S_STATE, S_CH), (2 * S_CH) ** -0.5),
        'ssm_c_re': nrm((DEPTH, S_GROUPS, S_CH, S_STATE), 0.25),
        'ssm_c_im': nrm((DEPTH, S_GROUPS, S_CH, S_STATE), 0.25),
        'ssm_d': nrm((DEPTH, S_GROUPS, S_CH)),
        'ssm_w_glu': nrm((DEPTH, S_WIDTH, S_WIDTH), S_WIDTH ** -0.5),
        'ssm_b_glu': nrm((DEPTH, S_WIDTH), 0.01),
        'rwkv_mu': jax.random.uniform(next(keys), (DEPTH, R_IN), F32),
        'rwkv_w0': jax.random.uniform(next(keys), (DEPTH, R_WIDTH), F32, -2.5, 0.0),
        'rwkv_w2': nrm((DEPTH, R_DECAY_LORA, R_WIDTH), 0.1),
        'rwkv_a0': nrm((DEPTH, R_WIDTH), 0.1),
        'rwkv_a2': nrm((DEPTH, R_A_LORA, R_WIDTH), R_A_LORA ** -0.5),
        'rwkv_g2': nrm((DEPTH, R_GATE_LORA, R_WIDTH), R_GATE_LORA ** -0.5),
        'rwkv_k_k': 0.85 + nrm((DEPTH, R_WIDTH), 0.05),
        'rwkv_k_a': near_one((DEPTH, R_WIDTH)),
        'rwkv_r_k': nrm((DEPTH, R_HEADS, R_DH), 0.1),
        'rwkv_ln_g': near_one((DEPTH, R_WIDTH)),
        'rwkv_ln_b': nrm((DEPTH, R_WIDTH), 0.01),
    }


def reference(x_prompt, x_sample, cache_nsa_cmp, cache_nsa_slc, cache_nsa_win, state_ssm, state_rwkv_shift,
              state_rwkv_wkv, page_table, c_prompt, c_sample, norm_g, w_ada, b_ada, ffn_w1, ffn_w3, ffn_w2,
              w_in, w_out, nsa_q_gain, nsa_k_gain, nsa_cmp_pe, nsa_cmp_w1, nsa_cmp_w2, ssm_a_re, ssm_a_im,
              ssm_log_dt, ssm_b_re, ssm_b_im, ssm_c_re, ssm_c_im, ssm_d, ssm_w_glu, ssm_b_glu, rwkv_mu, rwkv_w0,
              rwkv_w2, rwkv_a0, rwkv_a2, rwkv_g2, rwkv_k_k, rwkv_k_a, rwkv_r_k, rwkv_ln_g, rwkv_ln_b):
    bp, tp = x_prompt.shape[:2]
    pos_p = jnp.arange(tp)
    pos_s = PAST_LEN + jnp.arange(x_sample.shape[1])
    hp, hs = x_prompt, x_sample
    cmp_p, cmp_s, slc_p, slc_s, win_p, win_s = [], [], [], [], [], []
    ssm_p, ssm_s, sh_p, sh_s, wkv_p, wkv_s = [], [], [], [], [], []
    for l in range(DEPTH):
        lp = {
            'norm_g': norm_g[l], 'w_ada': w_ada[l], 'b_ada': b_ada[l],
            'ffn_w1': ffn_w1[l], 'ffn_w3': ffn_w3[l], 'ffn_w2': ffn_w2[l],
            'w_in': w_in[l], 'w_out': w_out[l],
            'q_gain': nsa_q_gain[l], 'k_gain': nsa_k_gain[l],
            'cmp_pe': nsa_cmp_pe[l], 'cmp_w1': nsa_cmp_w1[l], 'cmp_w2': nsa_cmp_w2[l],
            'ssm_a_re': ssm_a_re[l], 'ssm_a_im': ssm_a_im[l], 'ssm_log_dt': ssm_log_dt[l],
            'ssm_b_re': ssm_b_re[l], 'ssm_b_im': ssm_b_im[l], 'ssm_c_re': ssm_c_re[l], 'ssm_c_im': ssm_c_im[l],
            'ssm_d': ssm_d[l], 'ssm_w_glu': ssm_w_glu[l], 'ssm_b_glu': ssm_b_glu[l],
            'rwkv_mu': rwkv_mu[l], 'rwkv_w0': rwkv_w0[l], 'rwkv_w2': rwkv_w2[l], 'rwkv_a0': rwkv_a0[l],
            'rwkv_a2': rwkv_a2[l], 'rwkv_g2': rwkv_g2[l], 'rwkv_k_k': rwkv_k_k[l], 'rwkv_k_a': rwkv_k_a[l],
            'rwkv_r_k': rwkv_r_k[l], 'rwkv_ln_g': rwkv_ln_g[l], 'rwkv_ln_b': rwkv_ln_b[l],
        }
        hp, st_p = layer_forward(
            hp, c_prompt, pos_p, lp, nsa_prompt,
            jnp.zeros((bp, S_GROUPS, S_STATE, 2), F32), jnp.zeros((bp, R_IN), F32),
            jnp.zeros((bp, R_HEADS, R_DH, R_DH), F32))
        nsa_s = functools.partial(nsa_sample, cache_cmp=cache_nsa_cmp, cache_slc=cache_nsa_slc,
                                  win_buf=cache_nsa_win[l], page_table=page_table, layer=l)
        hs, st_s = layer_forward(hs, c_sample, pos_s, lp, nsa_s, state_ssm[l], state_rwkv_shift[l],
                                 state_rwkv_wkv[l])
        cmp_p.append(st_p[0]); slc_p.append(st_p[1]); win_p.append(st_p[2])
        ssm_p.append(st_p[3]); sh_p.append(st_p[4]); wkv_p.append(st_p[5])
        cmp_s.append(st_s[0]); slc_s.append(st_s[1]); win_s.append(st_s[2])
        ssm_s.append(st_s[3]); sh_s.append(st_s[4]); wkv_s.append(st_s[5])
    return (hp, hs, jnp.stack(cmp_p), jnp.stack(cmp_s), jnp.stack(slc_p), jnp.stack(slc_s),
            jnp.stack(win_p), jnp.stack(win_s), jnp.stack(ssm_p), jnp.stack(ssm_s),
            jnp.stack(sh_p), jnp.stack(sh_s), jnp.stack(wkv_p), jnp.stack(wkv_s))
```

```python
import functools
import math
from typing import NamedTuple

import jax
import jax.numpy as jnp
from jax import lax
from jax.experimental import pallas as pl
from jax.experimental.pallas import tpu as pltpu

F32 = jnp.float32
BF = jnp.bfloat16
I32 = jnp.int32
HI = lax.Precision.HIGHEST

D_MODEL = 2048
PAST_LEN = 2048
PAGE_SIZE = 128
A_HEADS, A_KV, A_HPG, A_DH = 8, 2, 4, 128
A_WIDTH = A_HEADS * A_DH
A_KVW = A_KV * A_DH
CMP_BLOCK, SEL_BLOCK, N_SEL, WINDOW = 32, 64, 16, 512
ROT_DIM = A_DH // 4
ROPE_THETA = 500000.0
ATT_SCALE = A_DH ** -0.5
FORCE_BONUS = 1.0e4
S_GROUPS, S_CH, S_STATE = 32, 16, 64
S_WIDTH = S_GROUPS * S_CH
S_FLAT = S_GROUPS * S_STATE
R_HEADS, R_DH = 8, 64
R_WIDTH = R_HEADS * R_DH
R_DECAY_LORA, R_A_LORA, R_GATE_LORA = 64, 64, 128
R_IN = 3 * R_WIDTH + R_DECAY_LORA + R_A_LORA + R_GATE_LORA
GN_EPS = 64e-5
OFF_KV = A_WIDTH
OFF_GATE = OFF_KV + 6 * A_KVW
OFF_SSM = OFF_GATE + 3 * A_HEADS
OFF_RWKV = OFF_SSM + S_WIDTH

ZQ, ZCMP, ZSLC, ZWIN, ZSSM, ZRW = 0, 1024, 1536, 2048, 2560, 3072
ZGATE = ZRW + R_IN
ZW = ZGATE + 128
LANE = 128
RWKV_CHUNK = 16


def _dot(a, b, precision=None):
    return jnp.dot(a, b, preferred_element_type=F32, precision=precision)


def _dot_nt(a, b):
    return lax.dot_general(a, b, (((1,), (1,)), ((), ())), preferred_element_type=F32)


def _dot_tn(a, b, precision=None):
    return lax.dot_general(a, b, (((0,), (0,)), ((), ())), preferred_element_type=F32, precision=precision)


def _call(body, grid, in_specs, out_specs, out_shape, scratch=(), nsp=0, name=None):
    gs = pltpu.PrefetchScalarGridSpec(num_scalar_prefetch=nsp, grid=grid, in_specs=in_specs, out_specs=out_specs,
                                      scratch_shapes=list(scratch))
    return pl.pallas_call(body, grid_spec=gs, out_shape=out_shape, name=name,
                          compiler_params=pltpu.CompilerParams(dimension_semantics=("arbitrary",) * len(grid)))


def _rms(x, g):
    return x * lax.rsqrt(jnp.mean(x * x, axis=-1, keepdims=True) + 1e-6) * g


def _finite_or_zero(m):
    return jnp.where(jnp.abs(m) < jnp.inf, m, 0.0)


def _masked_softmax(s, mask, axis):
    s = jnp.where(mask, s, -jnp.inf)
    m = _finite_or_zero(jnp.max(s, axis=axis, keepdims=True))
    e = jnp.where(mask, jnp.exp(s - m), 0.0)
    return e / jnp.maximum(jnp.sum(e, axis=axis, keepdims=True), 1e-30)


class _Rows(NamedTuple):
    n: int
    seq: int
    mod: jax.Array
    per_row: bool
    tm: int


def _mod_spec(rows, tm, col, width=D_MODEL, jdep=False):
    per = D_MODEL // width
    if rows.per_row:
        return pl.BlockSpec((tm, width), lambda i, j: (i, col * per + (j if jdep else 0)))
    tpb = rows.seq // tm
    return pl.BlockSpec((None, 1, width), lambda i, j: (i // tpb, 0, col * per + (j if jdep else 0)))


def _zspec(tm, width, off):
    assert off % width == 0
    return pl.BlockSpec((tm, width), lambda i, j: (i, off // width))


def _full(shape):
    nd = len(shape)
    return pl.BlockSpec(shape, lambda *a: (0,) * nd)


def _ada_body(cp_ref, cs_ref, w_ref, b_ref, op_ref, os_ref):
    w = w_ref[...].astype(BF)
    b = b_ref[...]
    for c_ref, o_ref in ((cp_ref, op_ref), (cs_ref, os_ref)):
        c = c_ref[...]
        o_ref[...] = _dot((c * jax.nn.sigmoid(c)).astype(BF), w) + b


def _ada(cp8, cs, w_ada, b_ada):
    nl, d, n = w_ada.shape
    tn = 1024
    ns = cs.shape[0]
    return _call(
        _ada_body, (nl, n // tn),
        [_full((8, d)), _full((ns, d)),
         pl.BlockSpec((None, d, tn), lambda l, j: (l, 0, j)),
         pl.BlockSpec((None, 1, tn), lambda l, j: (l, 0, j))],
        [pl.BlockSpec((None, 8, tn), lambda l, j: (l, 0, j)),
         pl.BlockSpec((None, ns, tn), lambda l, j: (l, 0, j))],
        [jax.ShapeDtypeStruct((nl, 8, n), F32), jax.ShapeDtypeStruct((nl, ns, n), F32)],
        name="ada")(cp8, cs, w_ada, b_ada.reshape(nl, 1, n))


def _norm_mod(x, g, scale, shift):
    return _rms(x, g) * (1.0 + scale) + shift


def _ffn_body(x_ref, g_ref, sh_ref, sc_ref, gt_ref, w1_ref, w3_ref, w2_ref, o_ref, xn_ref):
    f = pl.program_id(1)

    @pl.when(f == 0)
    def _():
        xn_ref[...] = _norm_mod(x_ref[...], g_ref[...], sc_ref[...], sh_ref[...]).astype(BF)
        o_ref[...] = jnp.zeros_like(o_ref)

    xn = xn_ref[...]
    h1 = _dot(xn, w1_ref[...])
    h3 = _dot(xn, w3_ref[...])
    o_ref[...] += _dot((h1 * jax.nn.sigmoid(h1) * h3).astype(BF), w2_ref[...])

    @pl.when(f == pl.num_programs(1) - 1)
    def _():
        o_ref[...] = x_ref[...] + 0.5 * gt_ref[...] * o_ref[...]


def _ffn(rows, x, g, col0, w1, w3, w2):
    d, ff = w1.shape
    tm, tf = rows.tm, 512
    return _call(
        _ffn_body, (rows.n // tm, ff // tf),
        [pl.BlockSpec((tm, d), lambda i, f: (i, 0)), _full((1, d)),
         _mod_spec(rows, tm, col0), _mod_spec(rows, tm, col0 + 1), _mod_spec(rows, tm, col0 + 2),
         pl.BlockSpec((d, tf), lambda i, f: (0, f)), pl.BlockSpec((d, tf), lambda i, f: (0, f)),
         pl.BlockSpec((tf, d), lambda i, f: (f, 0))],
        pl.BlockSpec((tm, d), lambda i, f: (i, 0)),
        jax.ShapeDtypeStruct((rows.n, d), F32),
        scratch=[pltpu.VMEM((tm, d), BF)], name="ffn")(x, g, rows.mod, rows.mod, rows.mod, w1, w3, w2)


def _projin_body(x_ref, g_ref, sh_ref, sc_ref, w_ref, o_ref, xn_ref):
    @pl.when(pl.program_id(1) == 0)
    def _():
        xn_ref[...] = _norm_mod(x_ref[...], g_ref[...], sc_ref[...], sh_ref[...]).astype(BF)

    o_ref[...] = _dot(xn_ref[...], w_ref[...])


def _projin(rows, x, g, col0, w):
    d, n = w.shape
    tm, tn = rows.tm, 384
    return _call(
        _projin_body, (rows.n // tm, n // tn),
        [pl.BlockSpec((tm, d), lambda i, j: (i, 0)), _full((1, d)),
         _mod_spec(rows, tm, col0), _mod_spec(rows, tm, col0 + 1),
         pl.BlockSpec((d, tn), lambda i, j: (0, j))],
        pl.BlockSpec((tm, tn), lambda i, j: (i, j)),
        jax.ShapeDtypeStruct((rows.n, n), F32),
        scratch=[pltpu.VMEM((tm, d), BF)], name="proj_in")(x, g, rows.mod, rows.mod, w)


def _projout_body(h_ref, gt_ref, a_ref, b_ref, c_ref, wa_ref, wb_ref, wc_ref, o_ref):
    mix = _dot(a_ref[...], wa_ref[...]) + _dot(b_ref[...], wb_ref[...]) + _dot(c_ref[...], wc_ref[...])
    o_ref[...] = h_ref[...] + gt_ref[...] * mix


def _projout(rows, h, o_nsa, o_ssm, o_rwkv, w):
    d = h.shape[1]
    tm, tn = rows.tm, 512
    nblk = A_WIDTH // S_WIDTH
    return _call(
        _projout_body, (rows.n // tm, d // tn),
        [pl.BlockSpec((tm, tn), lambda i, j: (i, j)), _mod_spec(rows, tm, 5, tn, True),
         pl.BlockSpec((tm, A_WIDTH), lambda i, j: (i, 0)), pl.BlockSpec((tm, S_WIDTH), lambda i, j: (i, 0)),
         pl.BlockSpec((tm, R_WIDTH), lambda i, j: (i, 0)),
         pl.BlockSpec((A_WIDTH, tn), lambda i, j: (0, j)), pl.BlockSpec((S_WIDTH, tn), lambda i, j: (nblk, j)),
         pl.BlockSpec((R_WIDTH, tn), lambda i, j: (nblk + 1, j))],
        pl.BlockSpec((tm, tn), lambda i, j: (i, j)),
        jax.ShapeDtypeStruct((rows.n, d), F32), name="proj_out")(h, rows.mod, o_nsa, o_ssm, o_rwkv, w, w, w)


def _rope_tables(pos):
    half = ROT_DIM // 2
    inv = ROPE_THETA ** (-2.0 * jnp.arange(half, dtype=F32) / ROT_DIM)
    ang = pos.astype(F32)[:, None] * inv[None, :]
    cos, sin = jnp.cos(ang), jnp.sin(ang)
    n = pos.shape[0]
    ct = jnp.concatenate([cos, cos, jnp.ones((n, A_DH - ROT_DIM), F32)], axis=1)
    sa = jnp.concatenate([-sin, jnp.zeros((n, A_DH - half), F32)], axis=1)
    sb = jnp.concatenate([jnp.zeros((n, half), F32), sin, jnp.zeros((n, A_DH - ROT_DIM), F32)], axis=1)
    return ct, sa, sb


def _prep_body(q_ref, ks_ref, kw_ref, gt_ref, ct_ref, sa_ref, sb_ref, qg_ref, kg1_ref, kg2_ref,
               qn_ref, qr_ref, slc_ref, win_ref, go_ref):
    ct, sa, sb = ct_ref[...], sa_ref[...], sb_ref[...]
    half = ROT_DIM // 2

    def rope(x):
        return x * ct + pltpu.roll(x, A_DH - half, 1) * sa + pltpu.roll(x, half, 1) * sb

    for h in range(A_HEADS):
        sl = slice(h * A_DH, (h + 1) * A_DH)
        x = _rms(q_ref[:, sl], qg_ref[...])
        qn_ref[:, sl] = (x * ATT_SCALE).astype(BF)
        qr_ref[:, sl] = (rope(x) * ATT_SCALE).astype(BF)
    for src, dst, kg in ((ks_ref, slc_ref, kg1_ref), (kw_ref, win_ref, kg2_ref)):
        for g in range(A_KV):
            sl = slice(g * A_DH, (g + 1) * A_DH)
            dst[:, sl] = rope(_rms(src[:, sl], kg[...]))
        dst[:, A_KVW:] = src[:, A_KVW:]
    go_ref[...] = jax.nn.sigmoid(gt_ref[...])


def _nsa_prep(n, tm, z, tabs, tab_blocks, q_gain, k_gain):
    tspec = pl.BlockSpec((tm, LANE), lambda i, j: (i % tab_blocks, 0))
    vspec = _full((1, A_DH))
    return _call(
        _prep_body, (n // tm, 1),
        [_zspec(tm, A_WIDTH, ZQ), _zspec(tm, 2 * A_KVW, ZSLC), _zspec(tm, 2 * A_KVW, ZWIN), _zspec(tm, LANE, ZGATE),
         tspec, tspec, tspec, vspec, vspec, vspec],
        [pl.BlockSpec((tm, A_WIDTH), lambda i, j: (i, 0)), pl.BlockSpec((tm, A_WIDTH), lambda i, j: (i, 0)),
         pl.BlockSpec((tm, 2 * A_KVW), lambda i, j: (i, 0)), pl.BlockSpec((tm, 2 * A_KVW), lambda i, j: (i, 0)),
         pl.BlockSpec((tm, LANE), lambda i, j: (i, 0))],
        [jax.ShapeDtypeStruct((n, A_WIDTH), BF), jax.ShapeDtypeStruct((n, A_WIDTH), BF),
         jax.ShapeDtypeStruct((n, 2 * A_KVW), F32), jax.ShapeDtypeStruct((n, 2 * A_KVW), F32),
         jax.ShapeDtypeStruct((n, LANE), F32)],
        name="nsa_prep")(z, z, z, z, *tabs, q_gain.reshape(1, A_DH), k_gain[1:2], k_gain[2:3])


def _cmp_tail(acc, w2, kg, is_k):
    out = _dot(jax.nn.gelu(acc).astype(BF), w2)
    return jnp.where(is_k, _rms(out, kg), out)


def _compress_body(x_ref, pe_ref, w1_ref, w2_ref, kg_ref, o_ref):
    x = (x_ref[...] + pe_ref[...]).astype(BF)
    o_ref[...] = _cmp_tail(_dot(x, w1_ref[...]), w2_ref[...], kg_ref[...], pl.program_id(0) == 0)


def _compress(x, pe, w1, w2, kg):
    _, m, f = x.shape
    tm = min(m, 256)
    return _call(
        _compress_body, (2, m // tm),
        [pl.BlockSpec((None, tm, f), lambda s, i: (s, i, 0)), pl.BlockSpec((None, 1, f), lambda s, i: (s, 0, 0)),
         pl.BlockSpec((None, f, A_DH), lambda s, i: (s, 0, 0)), pl.BlockSpec((None, A_DH, A_DH), lambda s, i: (s, 0, 0)),
         _full((1, A_DH))],
        pl.BlockSpec((None, tm, A_DH), lambda s, i: (s, i, 0)),
        jax.ShapeDtypeStruct((2, m, A_DH), F32), name="compress")(x, pe.reshape(2, 1, f), w1, w2, kg)


PAST_NB = 2


def _cmp_past_body(pt_ref, *refs):
    npg = PAST_LEN // PAGE_SIZE
    pages = refs[:PAST_NB * npg]
    pe_ref, w1_ref, w2_ref, kg_ref, o_ref, buf_ref = refs[PAST_NB * npg:]
    ncb = PAST_LEN // CMP_BLOCK
    for i, pg in enumerate(pages):
        for s in range(2):
            for g in range(A_KV):
                buf_ref[s * A_KV + g, i * PAGE_SIZE:(i + 1) * PAGE_SIZE, :] = pg[:, s, g, :]
    for s in range(2):
        acc = jnp.zeros((A_KV * PAST_NB * ncb, A_DH), F32)
        for r in range(0, CMP_BLOCK, 2):
            parts = []
            for rr in (r, r + 1):
                pe = pe_ref[s, rr:rr + 1, :]
                lhs = [buf_ref[s * A_KV + g, pl.ds(rr, PAST_NB * ncb, stride=CMP_BLOCK), :] for g in range(A_KV)]
                parts.append((jnp.concatenate(lhs, axis=0) + pe).astype(BF))
            acc = acc + _dot(jnp.concatenate(parts, axis=1), w1_ref[s, r * A_DH:(r + 2) * A_DH, :])
        out = _cmp_tail(acc, w2_ref[s], kg_ref[...], s == 0)
        for g in range(A_KV):
            for i in range(PAST_NB):
                o_ref[i, s, g] = out[(g * PAST_NB + i) * ncb:(g * PAST_NB + i + 1) * ncb]


def _compress_past(layer, cache, page_table, pe, w1, w2, kg):
    nb, npg = page_table.shape
    ncb = PAST_LEN // CMP_BLOCK

    def page_spec(i, p):
        return pl.BlockSpec((None, None, PAGE_SIZE, 2, A_KV, A_DH),
                            lambda b, pt: (layer, pt[(b * PAST_NB + i) * npg + p], 0, 0, 0, 0))

    specs = [page_spec(i, p) for i in range(PAST_NB) for p in range(npg)]
    return _call(
        _cmp_past_body, (nb // PAST_NB,),
        specs + [_full((2, CMP_BLOCK, A_DH)), _full((2, CMP_BLOCK * A_DH, A_DH)), _full((2, A_DH, A_DH)), _full((1, A_DH))],
        pl.BlockSpec((PAST_NB, 2, A_KV, ncb, A_DH), lambda b, pt: (b, 0, 0, 0, 0)),
        jax.ShapeDtypeStruct((nb, 2, A_KV, ncb, A_DH), F32),
        scratch=[pltpu.VMEM((2 * A_KV, PAST_NB * PAST_LEN, A_DH), F32)], nsp=1, name="compress_past")(
            page_table.reshape(-1), *([cache] * (PAST_NB * npg)), pe, w1, w2, kg)


def _select(score, nblk):
    j = lax.broadcasted_iota(I32, score.shape, 0)
    rank = jnp.zeros(score.shape, F32)
    for k in range(nblk):
        rk = score[k:k + 1, :]
        rank = rank + ((rk > score) | ((rk == score) & (k < j))).astype(F32)
    return rank


def _cmpattn_body(q_ref, ck_ref, cv_ref, o_ref, sel_ref):
    qi = pl.program_id(1)
    tq = q_ref.shape[0]
    nc = ck_ref.shape[1]
    ns = nc // 2
    base = qi * tq
    sel_rows = []
    for g in range(A_KV):
        q4 = jnp.concatenate([q_ref[:, (g * A_HPG + h) * A_DH:(g * A_HPG + h + 1) * A_DH] for h in range(A_HPG)], axis=0)
        ckg = ck_ref[g].astype(BF)
        cvg = cv_ref[g].astype(BF)
        pos = base + jnp.bitwise_and(lax.broadcasted_iota(I32, (A_HPG * tq, nc), 0), tq - 1)
        blk_end = (lax.broadcasted_iota(I32, (A_HPG * tq, nc), 1) + 1) * CMP_BLOCK - 1
        p = _masked_softmax(_dot_nt(q4, ckg), blk_end <= pos, -1)
        o = _dot(p.astype(BF), cvg)
        for h in range(A_HPG):
            o_ref[:, (g * A_HPG + h) * A_DH:(g * A_HPG + h + 1) * A_DH] = o[h * tq:(h + 1) * tq]
        ckp = jnp.concatenate([ck_ref[g, pl.ds(0, ns, stride=2), :], ck_ref[g, pl.ds(1, ns, stride=2), :]], axis=0).astype(BF)
        row = lax.broadcasted_iota(I32, (nc, A_HPG * tq), 0)
        blk = jnp.where(row < ns, 2 * row, 2 * (row - ns) + 1)
        post = base + jnp.bitwise_and(lax.broadcasted_iota(I32, (nc, A_HPG * tq), 1), tq - 1)
        pt = _masked_softmax(_dot_nt(ckp, q4), (blk + 1) * CMP_BLOCK - 1 <= post, 0)
        imp = pt[:, 0:tq]
        for h in range(1, A_HPG):
            imp = imp + pt[:, h * tq:(h + 1) * tq]
        imp = imp[:ns] + imp[ns:]
        j = lax.broadcasted_iota(I32, (ns, tq), 0)
        pos2 = base + lax.broadcasted_iota(I32, (ns, tq), 1)
        cur = jnp.right_shift(pos2, int(math.log2(SEL_BLOCK)))
        forced = (j == 0) | (j == cur) | (j == cur - 1)
        score = jnp.where(j * SEL_BLOCK <= pos2, imp + jnp.where(forced, FORCE_BONUS, 0.0), -1e9)
        sel_rows.append((_select(score, ns) < min(N_SEL, ns)).astype(F32))
    pad = jnp.zeros((LANE - A_KV * ns, tq), F32)
    sel_ref[...] = jnp.concatenate(sel_rows + [pad], axis=0).T


def _cmp_attn(nb, t, qn, ck, cv):
    tq = 256
    nc = ck.shape[2]
    cspec = pl.BlockSpec((None, A_KV, nc, A_DH), lambda b, i: (b, 0, 0, 0))
    return _call(
        _cmpattn_body, (nb, t // tq),
        [pl.BlockSpec((tq, A_WIDTH), lambda b, i: (b * (t // tq) + i, 0)), cspec, cspec],
        [pl.BlockSpec((tq, A_WIDTH), lambda b, i: (b * (t // tq) + i, 0)),
         pl.BlockSpec((tq, LANE), lambda b, i: (b * (t // tq) + i, 0))],
        [jax.ShapeDtypeStruct((nb * t, A_WIDTH), F32), jax.ShapeDtypeStruct((nb * t, LANE), F32)],
        name="cmp_attn")(qn, ck, cv)


def _slcwin_body(q_ref, sk_ref, wk_ref, sel_ref, os_ref, ow_ref):
    qi = pl.program_id(1)
    tq = q_ref.shape[0]
    t = sk_ref.shape[0]
    ns = t // SEL_BLOCK
    base = qi * tq
    wkeys = WINDOW + tq
    wstart = pl.multiple_of(jnp.maximum(base - WINDOW, 0), tq)
    selb = sel_ref[...].astype(BF)
    lrow = lax.broadcasted_iota(I32, (LANE, t), 0)
    kblk = jnp.right_shift(lax.broadcasted_iota(I32, (LANE, t), 1), int(math.log2(SEL_BLOCK)))
    qpos = base + jnp.bitwise_and(lax.broadcasted_iota(I32, (A_HPG * tq, t), 0), tq - 1)
    causal = lax.broadcasted_iota(I32, (A_HPG * tq, t), 1) <= qpos
    qpw = base + jnp.bitwise_and(lax.broadcasted_iota(I32, (A_HPG * tq, wkeys), 0), tq - 1)
    kpw = wstart + lax.broadcasted_iota(I32, (A_HPG * tq, wkeys), 1)
    wmask = (kpw <= qpw) & (kpw > qpw - WINDOW)
    for g in range(A_KV):
        ksl = slice(g * A_DH, (g + 1) * A_DH)
        vsl = slice(A_KVW + g * A_DH, A_KVW + (g + 1) * A_DH)
        q4 = jnp.concatenate([q_ref[:, (g * A_HPG + h) * A_DH:(g * A_HPG + h + 1) * A_DH] for h in range(A_HPG)], axis=0)
        expand = (lrow == g * ns + kblk).astype(BF)
        picked = _dot(selb, expand)
        smask = (jnp.concatenate([picked] * A_HPG, axis=0) > 0.5) & causal
        p = _masked_softmax(_dot_nt(q4, sk_ref[:, ksl].astype(BF)), smask, -1)
        o = _dot(p.astype(BF), sk_ref[:, vsl].astype(BF))
        pw = _masked_softmax(_dot_nt(q4, wk_ref[pl.ds(wstart, wkeys), ksl].astype(BF)), wmask, -1)
        ow = _dot(pw.astype(BF), wk_ref[pl.ds(wstart, wkeys), vsl].astype(BF))
        for h in range(A_HPG):
            hs = slice((g * A_HPG + h) * A_DH, (g * A_HPG + h + 1) * A_DH)
            os_ref[:, hs] = o[h * tq:(h + 1) * tq]
            ow_ref[:, hs] = ow[h * tq:(h + 1) * tq]


def _slc_win_attn(nb, t, qr, slc_rows, win_rows, sel):
    tq = 128
    nq = t // tq
    rowspec = pl.BlockSpec((tq, A_WIDTH), lambda b, i: (b * nq + i, 0))
    kvspec = pl.BlockSpec((t, 2 * A_KVW), lambda b, i: (b, 0))
    return _call(
        _slcwin_body, (nb, nq),
        [rowspec, kvspec, kvspec, pl.BlockSpec((tq, LANE), lambda b, i: (b * nq + i, 0))],
        [rowspec, rowspec],
        [jax.ShapeDtypeStruct((nb * t, A_WIDTH), F32)] * 2, name="slc_win_attn")(qr, slc_rows, win_rows, sel)


def _combine_body(g_ref, oc_ref, os_ref, ow_ref, o_ref):
    gt = g_ref[...]
    for hd in range(A_HEADS):
        sl = slice(hd * A_DH, (hd + 1) * A_DH)
        acc = (gt[:, hd:hd + 1] * oc_ref[:, sl] + gt[:, A_HEADS + hd:A_HEADS + hd + 1] * os_ref[:, sl]
               + gt[:, 2 * A_HEADS + hd:2 * A_HEADS + hd + 1] * ow_ref[:, sl])
        o_ref[:, sl] = acc.astype(BF)


def _nsa_combine(n, tm, gates, o_cmp, o_slc, o_win):
    spec = pl.BlockSpec((tm, A_WIDTH), lambda i: (i, 0))
    return _call(_combine_body, (n // tm,), [pl.BlockSpec((tm, LANE), lambda i: (i, 0)), spec, spec, spec], spec,
                 jax.ShapeDtypeStruct((n, A_WIDTH), BF), name="nsa_combine")(gates, o_cmp, o_slc, o_win)


def _cmpattn_s_body(q_ref, cp_ref, cn_ref, o_ref, idx_ref):
    pos = PAST_LEN
    q = q_ref[...]
    ncp = cp_ref.shape[2]
    ns = ncp // 2
    nrow = ns + 8
    hrow = lax.broadcasted_iota(I32, (A_HEADS, A_DH), 0)
    o_all = jnp.zeros((A_HEADS, A_DH), F32)
    idx_ref[...] = jnp.zeros(idx_ref.shape, I32)
    new_ok = (ncp + 1) * CMP_BLOCK - 1 <= pos
    for g in range(A_KV):
        ck = cp_ref[0, g].astype(BF)
        cv = cp_ref[1, g].astype(BF)
        ckn = cn_ref[0, g].astype(BF)
        cvn = cn_ref[1, g]
        vp = (lax.broadcasted_iota(I32, (A_HEADS, ncp), 1) + 1) * CMP_BLOCK - 1 <= pos
        vn = (lax.broadcasted_iota(I32, (A_HEADS, 8), 1) == 0) & new_ok
        sp = jnp.where(vp, _dot_nt(q, ck), -jnp.inf)
        sn = jnp.where(vn, _dot_nt(q, ckn), -jnp.inf)
        m = jnp.maximum(jnp.max(sp, axis=-1, keepdims=True), jnp.max(sn, axis=-1, keepdims=True))
        m = _finite_or_zero(m)
        ep = jnp.where(vp, jnp.exp(sp - m), 0.0)
        en = jnp.where(vn, jnp.exp(sn - m), 0.0)
        den = jnp.maximum(jnp.sum(ep, axis=-1, keepdims=True) + jnp.sum(en, axis=-1, keepdims=True), 1e-30)
        o = _dot((ep / den).astype(BF), cv) + (en / den)[:, 0:1] * cvn[0:1, :]
        o_all = jnp.where((hrow >= g * A_HPG) & (hrow < (g + 1) * A_HPG), o, o_all)
        cke = cp_ref[0, g, pl.ds(0, ns, stride=2), :].astype(BF)
        cko = cp_ref[0, g, pl.ds(1, ns, stride=2), :].astype(BF)
        rowe = lax.broadcasted_iota(I32, (ns, A_HEADS), 0)
        ve = (2 * rowe + 1) * CMP_BLOCK - 1 <= pos
        vo = (2 * rowe + 2) * CMP_BLOCK - 1 <= pos
        vnt = (lax.broadcasted_iota(I32, (8, A_HEADS), 0) == 0) & new_ok
        ste = jnp.where(ve, _dot_nt(cke, q), -jnp.inf)
        sto = jnp.where(vo, _dot_nt(cko, q), -jnp.inf)
        stn = jnp.where(vnt, _dot_nt(ckn, q), -jnp.inf)
        mt = jnp.maximum(jnp.maximum(jnp.max(ste, axis=0, keepdims=True), jnp.max(sto, axis=0, keepdims=True)),
                         jnp.max(stn, axis=0, keepdims=True))
        mt = _finite_or_zero(mt)
        ee = jnp.where(ve, jnp.exp(ste - mt), 0.0)
        eo = jnp.where(vo, jnp.exp(sto - mt), 0.0)
        et = jnp.where(vnt, jnp.exp(stn - mt), 0.0)
        dent = jnp.maximum(jnp.sum(ee, axis=0, keepdims=True) + jnp.sum(eo, axis=0, keepdims=True)
                           + jnp.sum(et, axis=0, keepdims=True), 1e-30)
        hlane = lax.broadcasted_iota(I32, (1, A_HEADS), 1)
        ing = (hlane >= g * A_HPG) & (hlane < (g + 1) * A_HPG)

        def imp_of(e):
            return jnp.sum(jnp.where(ing, e / dent, 0.0), axis=1, keepdims=True)

        imp = jnp.concatenate([imp_of(ee) + imp_of(eo), imp_of(et)], axis=0)
        imp = jnp.broadcast_to(imp, (nrow, LANE))
        j = lax.broadcasted_iota(I32, (nrow, LANE), 0)
        cur = pos // SEL_BLOCK
        nsel = ns + 1
        forced = (j == 0) | (j == cur) | (j == cur - 1)
        score = jnp.where(j * SEL_BLOCK <= pos, imp + jnp.where(forced, FORCE_BONUS, 0.0), -1e9)
        score = jnp.where(j < nsel, score, -3e9)
        rank = _select(score, nsel)
        slot = lax.broadcasted_iota(I32, (nrow, LANE), 1)
        hit = (rank == slot.astype(F32)) & (slot < min(N_SEL, nsel)) & (j < nsel)
        idx_ref[g:g + 1, :] = jnp.sum(jnp.where(hit, j, 0), axis=0, keepdims=True)
    o_ref[...] = o_all


def _cmp_attn_sample(q3, ckv_past, ckv_new):
    nb = q3.shape[0]
    ncp = ckv_past.shape[3]
    return _call(
        _cmpattn_s_body, (nb,),
        [pl.BlockSpec((None, A_HEADS, A_DH), lambda b: (b, 0, 0)),
         pl.BlockSpec((None, 2, A_KV, ncp, A_DH), lambda b: (b, 0, 0, 0, 0)),
         pl.BlockSpec((None, 2, A_KV, 8, A_DH), lambda b: (b, 0, 0, 0, 0))],
        [pl.BlockSpec((None, A_HEADS, A_DH), lambda b: (b, 0, 0)), pl.BlockSpec((None, 8, LANE), lambda b: (b, 0, 0))],
        [jax.ShapeDtypeStruct((nb, A_HEADS, A_DH), F32), jax.ShapeDtypeStruct((nb, 8, LANE), I32)],
        name="cmp_attn_sample")(q3, ckv_past, ckv_new)


def _slc_s_body(idx_ref, pt_ref, q_ref, *refs):
    blocks = refs[:N_SEL]
    new_ref, o_ref, kb_ref, vb_ref = refs[N_SEL:]
    b = pl.program_id(0)
    g = pl.program_id(1)
    npast = PAST_LEN // SEL_BLOCK
    nk = N_SEL * SEL_BLOCK
    for n in range(N_SEL):
        kb_ref[n * SEL_BLOCK:(n + 1) * SEL_BLOCK, :] = blocks[n][:, 0, g, :].astype(BF)
        vb_ref[n * SEL_BLOCK:(n + 1) * SEL_BLOCK, :] = blocks[n][:, 1, g, :].astype(BF)
    q = q_ref[...]
    slot = jnp.right_shift(lax.broadcasted_iota(I32, (A_HEADS, nk), 1), int(math.log2(SEL_BLOCK)))
    past = jnp.zeros((A_HEADS, nk), I32)
    nfresh = jnp.int32(0)
    for n in range(N_SEL):
        is_past = (idx_ref[(b * A_KV + g) * N_SEL + n] < npast).astype(I32)
        past = jnp.where(slot == n, is_past, past)
        nfresh = nfresh + (1 - is_past)
    mask = past > 0
    fresh = jnp.full((A_HEADS, 1), nfresh, I32) > 0
    new = new_ref[...]
    kn = jnp.where(g == 0, new[:, 0:A_DH], new[:, A_DH:A_KVW])
    vn = jnp.where(g == 0, new[:, A_KVW:A_KVW + A_DH], new[:, A_KVW + A_DH:])
    s = jnp.where(mask, _dot_nt(q, kb_ref[...]), -jnp.inf)
    sn = jnp.where(fresh, jnp.sum(q.astype(F32) * kn, axis=-1, keepdims=True), -jnp.inf)
    m = jnp.maximum(jnp.max(s, axis=-1, keepdims=True), sn)
    m = _finite_or_zero(m)
    e = jnp.where(mask, jnp.exp(s - m), 0.0)
    en = jnp.where(fresh, jnp.exp(sn - m), 0.0)
    den = jnp.maximum(jnp.sum(e, axis=-1, keepdims=True) + en, 1e-30)
    o = _dot((e / den).astype(BF), vb_ref[...]) + (en / den) * vn
    hrow = lax.broadcasted_iota(I32, (A_HEADS, A_DH), 0)

    @pl.when(g == 0)
    def _():
        o_ref[...] = jnp.where(hrow < A_HPG, o, 0.0)

    @pl.when(g != 0)
    def _():
        o_ref[...] = jnp.where((hrow >= g * A_HPG) & (hrow < (g + 1) * A_HPG), o, o_ref[...])


def _slc_attn_sample(layer, q3, cache, idx, page_table, slc_new):
    nb = q3.shape[0]
    npg = page_table.shape[1]
    npast = PAST_LEN // SEL_BLOCK
    bpp = PAGE_SIZE // SEL_BLOCK

    def blk_spec(n):
        def im(b, g, idx_ref, pt_ref):
            i = jnp.minimum(idx_ref[(b * A_KV + g) * N_SEL + n], npast - 1)
            return (layer, pt_ref[b * npg + i // bpp], i % bpp, 0, 0, 0)
        return pl.BlockSpec((None, None, SEL_BLOCK, 2, A_KV, A_DH), im)

    return _call(
        _slc_s_body, (nb, A_KV),
        [pl.BlockSpec((None, A_HEADS, A_DH), lambda b, g, i, p: (b, 0, 0))] + [blk_spec(n) for n in range(N_SEL)]
        + [pl.BlockSpec((None, 1, 2 * A_KVW), lambda b, g, i, p: (b, 0, 0))],
        pl.BlockSpec((None, A_HEADS, A_DH), lambda b, g, i, p: (b, 0, 0)),
        jax.ShapeDtypeStruct((nb, A_HEADS, A_DH), F32),
        scratch=[pltpu.VMEM((N_SEL * SEL_BLOCK, A_DH), BF)] * 2, nsp=2, name="slc_attn_sample")(
            idx, page_table.reshape(-1), q3, *([cache] * N_SEL), slc_new)


def _win_s_body(q_ref, buf_ref, new_ref, o_ref, nb_ref):
    pos = PAST_LEN
    wb = buf_ref.shape[0]
    q = q_ref[...]
    new = new_ref[...]
    hrow = lax.broadcasted_iota(I32, (A_HEADS, A_DH), 0)
    kpos = PAST_LEN - wb + lax.broadcasted_iota(I32, (A_HEADS, wb), 1)
    mask = (kpos <= pos) & (kpos > pos - WINDOW)
    o_all = jnp.zeros((A_HEADS, A_DH), F32)
    for g in range(A_KV):
        kn = new[:, g * A_DH:(g + 1) * A_DH]
        vn = new[:, A_KVW + g * A_DH:A_KVW + (g + 1) * A_DH]
        s = jnp.where(mask, _dot_nt(q, buf_ref[:, 0, g, :].astype(BF)), -jnp.inf)
        sn = jnp.sum(q.astype(F32) * kn, axis=-1, keepdims=True)
        m = jnp.maximum(jnp.max(s, axis=-1, keepdims=True), sn)
        e = jnp.where(mask, jnp.exp(s - m), 0.0)
        en = jnp.exp(sn - m)
        den = jnp.maximum(jnp.sum(e, axis=-1, keepdims=True) + en, 1e-30)
        o = _dot((e / den).astype(BF), buf_ref[:, 1, g, :].astype(BF)) + (en / den) * vn
        o_all = jnp.where((hrow >= g * A_HPG) & (hrow < (g + 1) * A_HPG), o, o_all)
        nb_ref[wb - 1, 0, g:g + 1, :] = kn
        nb_ref[wb - 1, 1, g:g + 1, :] = vn
    o_ref[...] = o_all
    nb_ref[pl.ds(0, wb - 1)] = buf_ref[pl.ds(1, wb - 1)]


def _win_attn_sample(layer, q3, cache, win_new):
    nb = q3.shape[0]
    wb = cache.shape[2]
    return _call(
        _win_s_body, (nb,),
        [pl.BlockSpec((None, A_HEADS, A_DH), lambda b: (b, 0, 0)),
         pl.BlockSpec((None, None, wb, 2, A_KV, A_DH), lambda b: (layer, b, 0, 0, 0, 0)),
         pl.BlockSpec((None, 1, 2 * A_KVW), lambda b: (b, 0, 0))],
        [pl.BlockSpec((None, A_HEADS, A_DH), lambda b: (b, 0, 0)),
         pl.BlockSpec((None, wb, 2, A_KV, A_DH), lambda b: (b, 0, 0, 0, 0))],
        [jax.ShapeDtypeStruct((nb, A_HEADS, A_DH), F32), jax.ShapeDtypeStruct((nb, wb, 2, A_KV, A_DH), F32)],
        name="win_attn_sample")(q3, cache, win_new)


def _ssm_disc_body(are_ref, aim_ref, ldt_ref, bre_ref, bim_ref, lre_ref, lim_ref, bbre_ref, bbim_ref):
    a_re, a_im = are_ref[...], aim_ref[...]
    dt = jnp.exp(ldt_ref[...])
    mag = jnp.exp(a_re * dt)
    lam_re = mag * jnp.cos(a_im * dt)
    lam_im = mag * jnp.sin(a_im * dt)
    den = a_re * a_re + a_im * a_im
    co_re = ((lam_re - 1.0) * a_re + lam_im * a_im) / den
    co_im = (lam_im * a_re - (lam_re - 1.0) * a_im) / den
    lre_ref[...] = lam_re
    lim_ref[...] = lam_im
    bbre_ref[...] = co_re * bre_ref[...] - co_im * bim_ref[...]
    bbim_ref[...] = co_re * bim_ref[...] + co_im * bre_ref[...]


def _ssm_disc(a_re, a_im, log_dt, b_re, b_im):
    flat = lambda a: a.reshape(1, S_FLAT)
    bt = lambda b: jnp.transpose(b, (2, 0, 1)).reshape(S_CH, S_FLAT)
    ldt = jnp.broadcast_to(log_dt[:, None], (S_GROUPS, S_STATE))
    v = jax.ShapeDtypeStruct((1, S_FLAT), F32)
    m = jax.ShapeDtypeStruct((S_CH, S_FLAT), F32)
    return pl.pallas_call(_ssm_disc_body, out_shape=[v, v, m, m], name="ssm_disc")(
        flat(a_re), flat(a_im), flat(ldt), bt(b_re), bt(b_im))


def _ssm_tail(hr, hi, u, cre, cim, d, wg, bg):
    y = _dot(hr.astype(BF), cre) - _dot(hi.astype(BF), cim) + d * u
    y = jax.nn.gelu(y)
    return (y * jax.nn.sigmoid(_dot(y.astype(BF), wg) + bg)).astype(BF)


def _ssm_body(u_ref, lre_ref, lim_ref, bbre_ref, bbim_ref, cre_ref, cim_ref, d_ref, wg_ref, bg_ref,
              o_ref, st_ref, xre, xim, hre, him):
    @pl.when(pl.program_id(1) == 0)
    def _():
        st_ref[...] = jnp.zeros_like(st_ref)

    tt = u_ref.shape[0]
    u = u_ref[...]
    ub = u.astype(BF)
    xre[...] = _dot(ub, bbre_ref[...])
    xim[...] = _dot(ub, bbim_ref[...])
    lre, lim = lre_ref[...], lim_ref[...]

    def step(t, c):
        hr, hi = c
        nr = lre * hr - lim * hi + xre[pl.ds(t, 1), :]
        ni = lre * hi + lim * hr + xim[pl.ds(t, 1), :]
        hre[pl.ds(t, 1), :] = nr
        him[pl.ds(t, 1), :] = ni
        return nr, ni

    hr, hi = lax.fori_loop(0, tt, step, (st_ref[0:1, :], st_ref[1:2, :]), unroll=4)
    st_ref[0:1, :] = hr
    st_ref[1:2, :] = hi
    o_ref[...] = _ssm_tail(hre[...], him[...], u, cre_ref[...], cim_ref[...], d_ref[...], wg_ref[...], bg_ref[...])


def _ssm_prompt(nb, t, z, sp):
    tt = 256
    nt = t // tt
    cs = [_full(a.shape) for a in sp]
    return _call(
        _ssm_body, (nb, nt),
        [pl.BlockSpec((tt, S_WIDTH), lambda b, i: (b * nt + i, ZSSM // S_WIDTH))] + cs,
        [pl.BlockSpec((tt, S_WIDTH), lambda b, i: (b * nt + i, 0)), pl.BlockSpec((None, 2, S_FLAT), lambda b, i: (b, 0, 0))],
        [jax.ShapeDtypeStruct((nb * t, S_WIDTH), BF), jax.ShapeDtypeStruct((nb, 2, S_FLAT), F32)],
        scratch=[pltpu.VMEM((tt, S_FLAT), F32)] * 4, name="ssm_prompt")(z, *sp)


def _ssm_s_body(u_ref, h0r_ref, h0i_ref, lre_ref, lim_ref, bbre_ref, bbim_ref, cre_ref, cim_ref, d_ref, wg_ref, bg_ref,
                o_ref, hr_ref, hi_ref):
    u = u_ref[...]
    ub = u.astype(BF)
    lre, lim = lre_ref[...], lim_ref[...]
    h0r, h0i = h0r_ref[...], h0i_ref[...]
    hr = _dot(ub, bbre_ref[...]) + (lre * h0r - lim * h0i)
    hi = _dot(ub, bbim_ref[...]) + (lre * h0i + lim * h0r)
    hr_ref[...] = hr
    hi_ref[...] = hi
    o_ref[...] = _ssm_tail(hr, hi, u, cre_ref[...], cim_ref[...], d_ref[...], wg_ref[...], bg_ref[...])


def _ssm_sample(z, h0r, h0i, sp):
    n = z.shape[0]
    st = pl.BlockSpec((n, S_FLAT), lambda i: (0, 0))
    return _call(
        _ssm_s_body, (1,),
        [pl.BlockSpec((n, S_WIDTH), lambda i: (0, ZSSM // S_WIDTH)), st, st] + [_full(a.shape) for a in sp],
        [pl.BlockSpec((n, S_WIDTH), lambda i: (0, 0)), st, st],
        [jax.ShapeDtypeStruct((n, S_WIDTH), BF), jax.ShapeDtypeStruct((n, S_FLAT), F32), jax.ShapeDtypeStruct((n, S_FLAT), F32)],
        name="ssm_sample")(z, h0r, h0i, *sp)


def _rwkv_prep_body(zr_ref, zk_ref, zv_ref, zwa_ref, zgl_ref, pr_ref, pk_ref, pv_ref, pwa_ref, pgl_ref,
                    mr_ref, mk_ref, mv_ref, mwa_ref, mgl_ref, w0_ref, w2_ref, a0_ref, a2_ref, g2_ref,
                    kkw_ref, kaw_ref, rk_ref, hs_ref,
                    r_ref, w_ref, k_ref, kk_ref, ka_ref, v_ref, g_ref, bonus_ref):
    def mix(z_ref, p_ref, m_ref):
        z = z_ref[...]
        return z + (p_ref[...] - z) * m_ref[...]

    r = mix(zr_ref, pr_ref, mr_ref)
    k = mix(zk_ref, pk_ref, mk_ref)
    v = mix(zv_ref, pv_ref, mv_ref)
    wa = mix(zwa_ref, pwa_ref, mwa_ref)
    gl = mix(zgl_ref, pgl_ref, mgl_ref)
    w = -jax.nn.softplus(-(w0_ref[...] + _dot(jnp.tanh(wa).astype(BF), w2_ref[...]))) - 0.5
    a = jax.nn.sigmoid(a0_ref[...] + _dot(wa.astype(BF), a2_ref[...]))
    hs = hs_ref[...]
    kk = k * kkw_ref[...]
    kk = kk / jnp.maximum(jnp.sqrt(_dot(kk * kk, hs, HI)), 1e-12)
    k = k * (1.0 + (a - 1.0) * kaw_ref[...])
    r_ref[...] = r
    w_ref[...] = jnp.exp(-jnp.exp(w))
    k_ref[...] = k
    kk_ref[...] = kk
    ka_ref[...] = kk * a
    v_ref[...] = v
    g_ref[...] = _dot(jax.nn.sigmoid(gl).astype(BF), g2_ref[...])
    bonus_ref[...] = _dot(r * k * rk_ref[...], hs, HI) * v


def _rwkv_prep(n, tm, z, prev, mu, rp):
    def spec(rows, width, off):
        assert off % width == 0
        return pl.BlockSpec((rows, width), (lambda i: (i, off // width)) if rows == tm else (lambda i: (0, off // width)))

    w = R_WIDTH
    offs = [(w, 0), (w, w), (w, 2 * w), (LANE, 3 * w), (LANE, 3 * w + LANE)]
    out = pl.BlockSpec((tm, w), lambda i: (i, 0))
    return _call(
        _rwkv_prep_body, (n // tm,),
        [spec(tm, wd, ZRW + o) for wd, o in offs] + [spec(tm, wd, o) for wd, o in offs] + [spec(1, wd, o) for wd, o in offs]
        + [_full(a.shape) for a in rp],
        [out] * 8, [jax.ShapeDtypeStruct((n, w), F32)] * 8, name="rwkv_prep")(
            *([z] * 5), *([prev] * 5), *([mu] * 5), *rp)


def _rwkv_scan_body(r_ref, w_ref, k_ref, kk_ref, ka_ref, v_ref, o_ref, st_ref, rs, ws, ks, kks, kas, vs, os_):
    @pl.when(pl.program_id(1) == 0)
    def _():
        st_ref[...] = jnp.zeros_like(st_ref)

    tt = r_ref.shape[0]
    n = R_DH
    lc = RWKV_CHUNK
    for h in range(R_HEADS):
        sl = slice(h * n, (h + 1) * n)
        for src, dst in ((r_ref, rs), (w_ref, ws), (k_ref, ks), (kk_ref, kks), (ka_ref, kas), (v_ref, vs)):
            dst[h] = src[:, sl]
    eye = (lax.broadcasted_iota(I32, (n, n), 0) == lax.broadcasted_iota(I32, (n, n), 1)).astype(F32)
    m0 = jnp.concatenate([eye, jnp.zeros((lc, n), F32)], axis=0)
    rowid = lax.broadcasted_iota(I32, (n + lc, n), 0)
    lane = lax.broadcasted_iota(I32, (n + lc, lc), 1)

    def chunk(ci, carry):
        t0 = pl.multiple_of(ci * lc, lc)
        for h in range(R_HEADS):
            m = m0
            cd = jnp.zeros((n + lc, lc), F32)
            for s in range(lc):
                t = t0 + s
                c = jnp.sum(m * kks[h, pl.ds(t, 1), :], axis=1, keepdims=True)
                m = m * ws[h, pl.ds(t, 1), :] - c * kas[h, pl.ds(t, 1), :]
                m = jnp.where(rowid == n + s, ks[h, pl.ds(t, 1), :], m)
                d = jnp.sum(m * rs[h, pl.ds(t, 1), :], axis=1, keepdims=True)
                cd = jnp.where(lane == s, d, cd)
            stack = jnp.concatenate([st_ref[h], vs[h, pl.ds(t0, lc), :]], axis=0)
            os_[h, pl.ds(t0, lc), :] = _dot_tn(cd, stack, HI)
            st_ref[h] = _dot_tn(m, stack, HI)
        return carry

    lax.fori_loop(0, tt // lc, chunk, 0)
    for h in range(R_HEADS):
        o_ref[:, h * n:(h + 1) * n] = os_[h]


def _rwkv_scan(nb, t, r, w, k, kk, ka, v):
    tt = 256
    nt = t // tt
    row = pl.BlockSpec((tt, R_WIDTH), lambda b, i: (b * nt + i, 0))
    return _call(
        _rwkv_scan_body, (nb, nt), [row] * 6,
        [row, pl.BlockSpec((None, R_HEADS, R_DH, R_DH), lambda b, i: (b, 0, 0, 0))],
        [jax.ShapeDtypeStruct((nb * t, R_WIDTH), F32), jax.ShapeDtypeStruct((nb, R_HEADS, R_DH, R_DH), F32)],
        scratch=[pltpu.VMEM((R_HEADS, tt, R_DH), F32)] * 7, name="rwkv_scan")(r, w, k, kk, ka, v)


RWKV_SB = 8


def _rwkv_step_body(r_ref, w_ref, k_ref, kk_ref, ka_ref, vt_ref, s_ref, ot_ref, so_ref):
    n = R_DH
    lane = lax.broadcasted_iota(I32, (n, RWKV_SB), 1)
    for h in range(R_HEADS):
        sl = slice(h * n, (h + 1) * n)
        ot = jnp.zeros((n, RWKV_SB), F32)
        for s in range(RWKV_SB):
            st = s_ref[s, h]
            rem = jnp.sum(st * kk_ref[s:s + 1, sl], axis=1, keepdims=True)
            st = st * w_ref[s:s + 1, sl] - rem * ka_ref[s:s + 1, sl] + vt_ref[sl, s:s + 1] * k_ref[s:s + 1, sl]
            so_ref[s, h] = st
            ot = jnp.where(lane == s, jnp.sum(st * r_ref[s:s + 1, sl], axis=1, keepdims=True), ot)
        ot_ref[sl, :] = ot


def _rwkv_step(r, w, k, kk, ka, v, state):
    nb = r.shape[0]
    nblk = nb // RWKV_SB
    vt = jnp.transpose(v.reshape(nblk, RWKV_SB, R_WIDTH), (0, 2, 1))
    row = pl.BlockSpec((RWKV_SB, R_WIDTH), lambda i: (i, 0))
    col = pl.BlockSpec((None, R_WIDTH, RWKV_SB), lambda i: (i, 0, 0))
    sts = pl.BlockSpec((RWKV_SB, R_HEADS, R_DH, R_DH), lambda i: (i, 0, 0, 0))
    ot, so = _call(
        _rwkv_step_body, (nblk,), [row] * 5 + [col, sts], [col, sts],
        [jax.ShapeDtypeStruct((nblk, R_WIDTH, RWKV_SB), F32), jax.ShapeDtypeStruct(state.shape, F32)],
        name="rwkv_step")(r, w, k, kk, ka, vt, state)
    return jnp.transpose(ot, (0, 2, 1)).reshape(nb, R_WIDTH), so


def _rwkv_post_body(o_ref, g_ref, bonus_ref, lng_ref, lnb_ref, ha_ref, out_ref):
    o = o_ref[...]
    ha = ha_ref[...]
    cen = o - _dot(o, ha, HI)
    var = _dot(cen * cen, ha, HI)
    y = cen * lax.rsqrt(var + GN_EPS) * lng_ref[...] + lnb_ref[...]
    out_ref[...] = ((y + bonus_ref[...]) * g_ref[...]).astype(BF)


def _rwkv_post(n, tm, o, g, bonus, ln_g, ln_b, havg):
    row = pl.BlockSpec((tm, R_WIDTH), lambda i: (i, 0))
    vec = _full((1, R_WIDTH))
    return _call(_rwkv_post_body, (n // tm,), [row, row, row, vec, vec, _full(havg.shape)], row,
                 jax.ShapeDtypeStruct((n, R_WIDTH), BF), name="rwkv_post")(o, g, bonus, ln_g, ln_b, havg)


def _block_diag(blocks):
    g, a, b = blocks.shape
    return jnp.einsum('gab,gh->gahb', blocks, jnp.eye(g, dtype=blocks.dtype)).reshape(g * a, g * b)


class _LayerParams(NamedTuple):
    norm_g: jax.Array
    ffn: tuple
    w_in: jax.Array
    w_out: jax.Array
    q_gain: jax.Array
    k_gain: jax.Array
    cmp: tuple
    ssm: tuple
    rwkv_mu: jax.Array
    rwkv: tuple
    rwkv_ln: tuple


def _layer_params(l, p):
    d = D_MODEL
    w_in = p['w_in'][l]
    w_in_r = jnp.concatenate([w_in[:, :OFF_GATE], w_in[:, OFF_SSM:], w_in[:, OFF_GATE:OFF_SSM],
                              jnp.zeros((d, ZW - ZGATE - 3 * A_HEADS), F32)], axis=1).astype(BF)
    ffn = tuple((p['ffn_w1'][l, i].astype(BF), p['ffn_w3'][l, i].astype(BF), p['ffn_w2'][l, i].astype(BF)) for i in range(2))
    cmp = (p['nsa_cmp_pe'][l], p['nsa_cmp_w1'][l].astype(BF), p['nsa_cmp_w2'][l].astype(BF), p['nsa_k_gain'][l, 0:1])
    lre, lim, bbre, bbim = _ssm_disc(p['ssm_a_re'][l], p['ssm_a_im'][l], p['ssm_log_dt'][l], p['ssm_b_re'][l], p['ssm_b_im'][l])
    to_gcp = lambda m: jnp.transpose(m.reshape(S_CH, S_GROUPS, S_STATE), (1, 0, 2))
    ssm = (lre, lim, _block_diag(to_gcp(bbre)).astype(BF), _block_diag(to_gcp(bbim)).astype(BF),
           _block_diag(jnp.transpose(p['ssm_c_re'][l], (0, 2, 1))).astype(BF),
           _block_diag(jnp.transpose(p['ssm_c_im'][l], (0, 2, 1))).astype(BF),
           p['ssm_d'][l].reshape(1, S_WIDTH), p['ssm_w_glu'][l].astype(BF), p['ssm_b_glu'][l].reshape(1, S_WIDTH))
    zl = jnp.zeros((R_DECAY_LORA, R_WIDTH), F32)
    vec = lambda a: a.reshape(1, R_WIDTH)
    hsum = _block_diag(jnp.ones((R_HEADS, R_DH, R_DH), F32))
    rwkv = (vec(p['rwkv_w0'][l]), jnp.concatenate([p['rwkv_w2'][l], zl], axis=0).astype(BF),
            vec(p['rwkv_a0'][l]), jnp.concatenate([zl, p['rwkv_a2'][l]], axis=0).astype(BF), p['rwkv_g2'][l].astype(BF),
            vec(p['rwkv_k_k'][l]), vec(p['rwkv_k_a'][l]), vec(p['rwkv_r_k'][l]), hsum)
    return _LayerParams(p['norm_g'][l], ffn, w_in_r, p['w_out'][l].astype(BF), p['nsa_q_gain'][l], p['nsa_k_gain'][l], cmp,
                        ssm, p['rwkv_mu'][l].reshape(1, R_IN), rwkv,
                        (vec(p['rwkv_ln_g'][l]), vec(p['rwkv_ln_b'][l]), hsum / R_DH))


def _layer_prompt(rows, nb, t, x, lp, tabs):
    n = rows.n
    h = _ffn(rows, x, lp.norm_g[0:1], 0, *lp.ffn[0])
    z = _projin(rows, h, lp.norm_g[1:2], 3, lp.w_in)
    qn, qr, slc_rows, win_rows, gates = _nsa_prep(n, 256, z, tabs, t // 256, lp.q_gain, lp.k_gain)
    ncb = t // CMP_BLOCK
    cmp_rows = z[:, ZCMP:ZSLC]
    xcmp = jnp.transpose(cmp_rows.reshape(nb, ncb, CMP_BLOCK, 2, A_KV, A_DH), (3, 0, 4, 1, 2, 5))
    ckv = _compress(xcmp.reshape(2, nb * A_KV * ncb, CMP_BLOCK * A_DH), *lp.cmp).reshape(2, nb, A_KV, ncb, A_DH)
    o_cmp, sel = _cmp_attn(nb, t, qn, ckv[0], ckv[1])
    o_slc, o_win = _slc_win_attn(nb, t, qr, slc_rows, win_rows, sel)
    o_nsa = _nsa_combine(n, 256, gates, o_cmp, o_slc, o_win)
    o_ssm, ssm_st = _ssm_prompt(nb, t, z, lp.ssm)
    zr = z[:, ZRW:ZGATE].reshape(nb, t, R_IN)
    prev = jnp.concatenate([jnp.zeros((nb, 1, R_IN), F32), zr[:, :-1]], axis=1).reshape(n, R_IN)
    r, w, k, kk, ka, v, g, bonus = _rwkv_prep(n, 256, z, prev, lp.rwkv_mu, lp.rwkv)
    o_scan, wkv_t = _rwkv_scan(nb, t, r, w, k, kk, ka, v)
    o_rwkv = _rwkv_post(n, 256, o_scan, g, bonus, *lp.rwkv_ln)
    h = _projout(rows, h, o_nsa, o_ssm, o_rwkv, lp.w_out)
    y = _ffn(rows, h, lp.norm_g[2:3], 6, *lp.ffn[1])
    keep = min(WINDOW, t)
    shape6 = lambda a: a.reshape(nb, -1, 2, A_KV, A_DH)
    state = (shape6(cmp_rows), shape6(slc_rows), shape6(win_rows)[:, t - keep:],
             jnp.stack([ssm_st[:, 0], ssm_st[:, 1]], axis=-1).reshape(nb, S_GROUPS, S_STATE, 2),
             zr[:, -1], jnp.swapaxes(wkv_t, -1, -2))
    return y, state


def _layer_sample(rows, layer, x, lp, tabs, cache_cmp, cache_slc, cache_win, page_table, ssm0, shift0, wkv0):
    n = rows.n
    h = _ffn(rows, x, lp.norm_g[0:1], 0, *lp.ffn[0])
    z = _projin(rows, h, lp.norm_g[1:2], 3, lp.w_in)
    qn, qr, slc_new, win_new, gates = _nsa_prep(n, n, z, tabs, 1, lp.q_gain, lp.k_gain)
    cmp_new = z[:, ZCMP:ZSLC]
    ckv_past = _compress_past(layer, cache_cmp, page_table, *lp.cmp)
    xnew = jnp.transpose(cmp_new.reshape(n, 2, A_KV, A_DH), (1, 0, 2, 3)).reshape(2, n * A_KV, A_DH)
    xnew = jnp.pad(xnew, ((0, 0), (0, 0), (0, (CMP_BLOCK - 1) * A_DH)))
    ckv_new = _compress(xnew, *lp.cmp).reshape(2, n, A_KV, 1, A_DH)
    ckv_new = jnp.pad(jnp.transpose(ckv_new, (1, 0, 2, 3, 4)), ((0, 0), (0, 0), (0, 0), (0, 7), (0, 0)))
    q3n = qn.reshape(n, A_HEADS, A_DH)
    q3r = qr.reshape(n, A_HEADS, A_DH)
    o_cmp, idx = _cmp_attn_sample(q3n, ckv_past, ckv_new)
    idx_flat = idx[:, :A_KV, :N_SEL].reshape(-1)
    o_slc = _slc_attn_sample(layer, q3r, cache_slc, idx_flat, page_table, slc_new.reshape(n, 1, 2 * A_KVW))
    o_win, new_buf = _win_attn_sample(layer, q3r, cache_win, win_new.reshape(n, 1, 2 * A_KVW))
    flat = lambda a: a.reshape(n, A_WIDTH)
    o_nsa = _nsa_combine(n, n, gates, flat(o_cmp), flat(o_slc), flat(o_win))
    o_ssm, hr, hi = _ssm_sample(z, ssm0[..., 0].reshape(n, S_FLAT), ssm0[..., 1].reshape(n, S_FLAT), lp.ssm)
    r, w, k, kk, ka, v, g, bonus = _rwkv_prep(n, n, z, shift0, lp.rwkv_mu, lp.rwkv)
    o_step, wkv = _rwkv_step(r, w, k, kk, ka, v, wkv0)
    o_rwkv = _rwkv_post(n, n, o_step, g, bonus, *lp.rwkv_ln)
    h = _projout(rows, h, o_nsa, o_ssm, o_rwkv, lp.w_out)
    y = _ffn(rows, h, lp.norm_g[2:3], 6, *lp.ffn[1])
    shape6 = lambda a: a.reshape(n, 1, 2, A_KV, A_DH)
    state = (shape6(cmp_new), shape6(slc_new), new_buf,
             jnp.stack([hr, hi], axis=-1).reshape(n, S_GROUPS, S_STATE, 2), z[:, ZRW:ZGATE], wkv)
    return y, state


def kernel(x_prompt, x_sample, cache_nsa_cmp, cache_nsa_slc, cache_nsa_win, state_ssm, state_rwkv_shift, state_rwkv_wkv, page_table, c_prompt, c_sample, norm_g, w_ada, b_ada, ffn_w1, ffn_w3, ffn_w2, w_in, w_out, nsa_q_gain, nsa_k_gain, nsa_cmp_pe, nsa_cmp_w1, nsa_cmp_w2, ssm_a_re, ssm_a_im, ssm_log_dt, ssm_b_re, ssm_b_im, ssm_c_re, ssm_c_im, ssm_d, ssm_w_glu, ssm_b_glu, rwkv_mu, rwkv_w0, rwkv_w2, rwkv_a0, rwkv_a2, rwkv_g2, rwkv_k_k, rwkv_k_a, rwkv_r_k, rwkv_ln_g, rwkv_ln_b):
    p = dict(norm_g=norm_g, ffn_w1=ffn_w1, ffn_w3=ffn_w3, ffn_w2=ffn_w2, w_in=w_in, w_out=w_out, nsa_q_gain=nsa_q_gain,
             nsa_k_gain=nsa_k_gain, nsa_cmp_pe=nsa_cmp_pe, nsa_cmp_w1=nsa_cmp_w1, nsa_cmp_w2=nsa_cmp_w2, ssm_a_re=ssm_a_re,
             ssm_a_im=ssm_a_im, ssm_log_dt=ssm_log_dt, ssm_b_re=ssm_b_re, ssm_b_im=ssm_b_im, ssm_c_re=ssm_c_re,
             ssm_c_im=ssm_c_im, ssm_d=ssm_d, ssm_w_glu=ssm_w_glu, ssm_b_glu=ssm_b_glu, rwkv_mu=rwkv_mu, rwkv_w0=rwkv_w0,
             rwkv_w2=rwkv_w2, rwkv_a0=rwkv_a0, rwkv_a2=rwkv_a2, rwkv_g2=rwkv_g2, rwkv_k_k=rwkv_k_k, rwkv_k_a=rwkv_k_a,
             rwkv_r_k=rwkv_r_k, rwkv_ln_g=rwkv_ln_g, rwkv_ln_b=rwkv_ln_b)
    depth = w_in.shape[0]
    nbp, t, d = x_prompt.shape
    nbs, ts, _ = x_sample.shape
    assert ts == 1 and nbp <= 8 and d == D_MODEL
    cp8 = jnp.pad(c_prompt, ((0, 8 - nbp), (0, 0)))
    mod_p, mod_s = _ada(cp8, c_sample, w_ada, b_ada)
    tabs_p = _rope_tables(jnp.arange(t))
    tabs_s = _rope_tables(jnp.full((nbs,), PAST_LEN))
    hp = x_prompt.reshape(nbp * t, d)
    hs = x_sample.reshape(nbs, d)
    st_p, st_s = [], []
    for l in range(depth):
        lp = _layer_params(l, p)
        rows_p = _Rows(nbp * t, t, mod_p[l].reshape(8, 1, 9 * d), False, 512)
        rows_s = _Rows(nbs, 1, mod_s[l], True, nbs)
        hp, sp = _layer_prompt(rows_p, nbp, t, hp, lp, tabs_p)
        hs, ss = _layer_sample(rows_s, l, hs, lp, tabs_s, cache_nsa_cmp, cache_nsa_slc, cache_nsa_win, page_table,
                               state_ssm[l], state_rwkv_shift[l], state_rwkv_wkv[l])
        st_p.append(sp)
        st_s.append(ss)
    outs = [hp.reshape(nbp, t, d), hs.reshape(nbs, 1, d)]
    for i in range(6):
        outs.append(jnp.stack([s[i] for s in st_p]))
        outs.append(jnp.stack([s[i] for s in st_s]))
    return tuple(outs)
```

```python
import functools
import math
from typing import NamedTuple

import jax
import jax.numpy as jnp
from jax import lax
from jax.experimental import pallas as pl
from jax.experimental.pallas import tpu as pltpu

F32 = jnp.float32
BF = jnp.bfloat16
I32 = jnp.int32
HI = lax.Precision.HIGHEST

D_MODEL = 2048
PAST_LEN = 2048
PAGE_SIZE = 128
A_HEADS, A_KV, A_HPG, A_DH = 8, 2, 4, 128
A_WIDTH = A_HEADS * A_DH
A_KVW = A_KV * A_DH
CMP_BLOCK, SEL_BLOCK, N_SEL, WINDOW = 32, 64, 16, 512
ROT_DIM = A_DH // 4
ROPE_THETA = 500000.0
ATT_SCALE = A_DH ** -0.5
FORCE_BONUS = 1.0e4
S_GROUPS, S_CH, S_STATE = 32, 16, 64
S_WIDTH = S_GROUPS * S_CH
S_FLAT = S_GROUPS * S_STATE
R_HEADS, R_DH = 8, 64
R_WIDTH = R_HEADS * R_DH
R_DECAY_LORA, R_A_LORA, R_GATE_LORA = 64, 64, 128
R_IN = 3 * R_WIDTH + R_DECAY_LORA + R_A_LORA + R_GATE_LORA
GN_EPS = 64e-5
OFF_KV = A_WIDTH
OFF_GATE = OFF_KV + 6 * A_KVW
OFF_SSM = OFF_GATE + 3 * A_HEADS
OFF_RWKV = OFF_SSM + S_WIDTH

ZQ, ZCMP, ZSLC, ZWIN, ZSSM, ZRW = 0, 1024, 1536, 2048, 2560, 3072
ZGATE = ZRW + R_IN
ZW = ZGATE + 128
LANE = 128
RWKV_CHUNK = 16


def _dot(a, b, precision=None):
    return jnp.dot(a, b, preferred_element_type=F32, precision=precision)


def _dot_nt(a, b, precision=None):
    return lax.dot_general(a, b, (((1,), (1,)), ((), ())), preferred_element_type=F32, precision=precision)


def _dot_tn(a, b, precision=None):
    return lax.dot_general(a, b, (((0,), (0,)), ((), ())), preferred_element_type=F32, precision=precision)


def _call(body, grid, in_specs, out_specs, out_shape, scratch=(), nsp=0, name=None, aliases=None):
    gs = pltpu.PrefetchScalarGridSpec(num_scalar_prefetch=nsp, grid=grid, in_specs=in_specs, out_specs=out_specs,
                                      scratch_shapes=list(scratch))
    return pl.pallas_call(body, grid_spec=gs, out_shape=out_shape, name=name, input_output_aliases=aliases or {},
                          compiler_params=pltpu.CompilerParams(dimension_semantics=("arbitrary",) * len(grid)))


def _rms(x, g):
    return x * lax.rsqrt(jnp.mean(x * x, axis=-1, keepdims=True) + 1e-6) * g


def _finite_or_zero(m):
    return jnp.where(jnp.abs(m) < jnp.inf, m, 0.0)


def _masked_softmax(s, mask, axis):
    s = jnp.where(mask, s, -jnp.inf)
    m = _finite_or_zero(jnp.max(s, axis=axis, keepdims=True))
    e = jnp.where(mask, jnp.exp(s - m), 0.0)
    return e / jnp.maximum(jnp.sum(e, axis=axis, keepdims=True), 1e-30)


class _Rows(NamedTuple):
    n: int
    seq: int
    mod: jax.Array
    per_row: bool
    tm: int


def _mod_spec(rows, tm, col, width=D_MODEL, jdep=False):
    per = D_MODEL // width
    if rows.per_row:
        return pl.BlockSpec((tm, width), lambda i, j: (i, col * per + (j if jdep else 0)))
    tpb = rows.seq // tm
    return pl.BlockSpec((None, 1, width), lambda i, j: (i // tpb, 0, col * per + (j if jdep else 0)))


def _zspec(tm, width, off):
    assert off % width == 0
    return pl.BlockSpec((tm, width), lambda i, j: (i, off // width))


def _full(shape):
    nd = len(shape)
    return pl.BlockSpec(shape, lambda *a: (0,) * nd)


def _ada_body(cp_ref, cs_ref, w_ref, b_ref, op_ref, os_ref):
    w = w_ref[...].astype(BF)
    b = b_ref[...]
    for c_ref, o_ref in ((cp_ref, op_ref), (cs_ref, os_ref)):
        c = c_ref[...]
        o_ref[...] = _dot((c * jax.nn.sigmoid(c)).astype(BF), w) + b


def _ada(cp8, cs, w_ada, b_ada):
    nl, d, n = w_ada.shape
    tn = 1024
    ns = cs.shape[0]
    return _call(
        _ada_body, (nl, n // tn),
        [_full((8, d)), _full((ns, d)),
         pl.BlockSpec((None, d, tn), lambda l, j: (l, 0, j)),
         pl.BlockSpec((None, 1, tn), lambda l, j: (l, 0, j))],
        [pl.BlockSpec((None, 8, tn), lambda l, j: (l, 0, j)),
         pl.BlockSpec((None, ns, tn), lambda l, j: (l, 0, j))],
        [jax.ShapeDtypeStruct((nl, 8, n), F32), jax.ShapeDtypeStruct((nl, ns, n), F32)],
        name="ada")(cp8, cs, w_ada, b_ada.reshape(nl, 1, n))


def _norm_mod(x, g, scale, shift):
    return _rms(x, g) * (1.0 + scale) + shift


def _ffn_body(x_ref, g_ref, sh_ref, sc_ref, gt_ref, w1_ref, w3_ref, w2_ref, o_ref, xn_ref):
    f = pl.program_id(1)

    @pl.when(f == 0)
    def _():
        xn_ref[...] = _norm_mod(x_ref[...], g_ref[...], sc_ref[...], sh_ref[...]).astype(BF)
        o_ref[...] = jnp.zeros_like(o_ref)

    xn = xn_ref[...]
    h1 = _dot(xn, w1_ref[...])
    h3 = _dot(xn, w3_ref[...])
    o_ref[...] += _dot((h1 * jax.nn.sigmoid(h1) * h3).astype(BF), w2_ref[...])

    @pl.when(f == pl.num_programs(1) - 1)
    def _():
        o_ref[...] = x_ref[...] + 0.5 * gt_ref[...] * o_ref[...]


def _ffn(rows, x, g, col0, w1, w3, w2):
    d, ff = w1.shape
    tm, tf = rows.tm, 512
    return _call(
        _ffn_body, (rows.n // tm, ff // tf),
        [pl.BlockSpec((tm, d), lambda i, f: (i, 0)), _full((1, d)),
         _mod_spec(rows, tm, col0), _mod_spec(rows, tm, col0 + 1), _mod_spec(rows, tm, col0 + 2),
         pl.BlockSpec((d, tf), lambda i, f: (0, f)), pl.BlockSpec((d, tf), lambda i, f: (0, f)),
         pl.BlockSpec((tf, d), lambda i, f: (f, 0))],
        pl.BlockSpec((tm, d), lambda i, f: (i, 0)),
        jax.ShapeDtypeStruct((rows.n, d), F32),
        scratch=[pltpu.VMEM((tm, d), BF)], name="ffn")(x, g, rows.mod, rows.mod, rows.mod, w1, w3, w2)


def _projin_body(x_ref, g_ref, sh_ref, sc_ref, w_ref, o_ref, xn_ref):
    @pl.when(pl.program_id(1) == 0)
    def _():
        xn_ref[...] = _norm_mod(x_ref[...], g_ref[...], sc_ref[...], sh_ref[...]).astype(BF)

    o_ref[...] = _dot(xn_ref[...], w_ref[...])


def _projin(rows, x, g, col0, w):
    d, n = w.shape
    tm, tn = rows.tm, 1664
    return _call(
        _projin_body, (rows.n // tm, n // tn),
        [pl.BlockSpec((tm, d), lambda i, j: (i, 0)), _full((1, d)),
         _mod_spec(rows, tm, col0), _mod_spec(rows, tm, col0 + 1),
         pl.BlockSpec((d, tn), lambda i, j: (0, j))],
        pl.BlockSpec((tm, tn), lambda i, j: (i, j)),
        jax.ShapeDtypeStruct((rows.n, n), F32),
        scratch=[pltpu.VMEM((tm, d), BF)], name="proj_in")(x, g, rows.mod, rows.mod, w)


def _projout_body(h_ref, gt_ref, a_ref, b_ref, c_ref, wa_ref, wb_ref, wc_ref, o_ref):
    mix = _dot(a_ref[...], wa_ref[...]) + _dot(b_ref[...], wb_ref[...]) + _dot(c_ref[...], wc_ref[...])
    o_ref[...] = h_ref[...] + gt_ref[...] * mix


def _projout(rows, h, o_nsa, o_ssm, o_rwkv, w):
    d = h.shape[1]
    tm, tn = rows.tm, 512
    nblk = A_WIDTH // S_WIDTH
    return _call(
        _projout_body, (rows.n // tm, d // tn),
        [pl.BlockSpec((tm, tn), lambda i, j: (i, j)), _mod_spec(rows, tm, 5, tn, True),
         pl.BlockSpec((tm, A_WIDTH), lambda i, j: (i, 0)), pl.BlockSpec((tm, S_WIDTH), lambda i, j: (i, 0)),
         pl.BlockSpec((tm, R_WIDTH), lambda i, j: (i, 0)),
         pl.BlockSpec((A_WIDTH, tn), lambda i, j: (0, j)), pl.BlockSpec((S_WIDTH, tn), lambda i, j: (nblk, j)),
         pl.BlockSpec((R_WIDTH, tn), lambda i, j: (nblk + 1, j))],
        pl.BlockSpec((tm, tn), lambda i, j: (i, j)),
        jax.ShapeDtypeStruct((rows.n, d), F32), name="proj_out")(h, rows.mod, o_nsa, o_ssm, o_rwkv, w, w, w)


def _rope_tables(pos):
    half = ROT_DIM // 2
    inv = ROPE_THETA ** (-2.0 * jnp.arange(half, dtype=F32) / ROT_DIM)
    ang = pos.astype(F32)[:, None] * inv[None, :]
    cos, sin = jnp.cos(ang), jnp.sin(ang)
    n = pos.shape[0]
    ct = jnp.concatenate([cos, cos, jnp.ones((n, A_DH - ROT_DIM), F32)], axis=1)
    sa = jnp.concatenate([-sin, jnp.zeros((n, A_DH - half), F32)], axis=1)
    sb = jnp.concatenate([jnp.zeros((n, half), F32), sin, jnp.zeros((n, A_DH - ROT_DIM), F32)], axis=1)
    return ct, sa, sb


def _prep_body(q_ref, ks_ref, kw_ref, gt_ref, ct_ref, sa_ref, sb_ref, qg_ref, kg1_ref, kg2_ref,
               qn_ref, qr_ref, slc_ref, win_ref, go_ref):
    ct, sa, sb = ct_ref[...], sa_ref[...], sb_ref[...]
    half = ROT_DIM // 2

    def rope(x):
        return x * ct + pltpu.roll(x, A_DH - half, 1) * sa + pltpu.roll(x, half, 1) * sb

    for h in range(A_HEADS):
        sl = slice(h * A_DH, (h + 1) * A_DH)
        x = _rms(q_ref[:, sl], qg_ref[...])
        qn_ref[:, sl] = (x * ATT_SCALE).astype(BF)
        qr_ref[:, sl] = (rope(x) * ATT_SCALE).astype(BF)
    for src, dst, kg in ((ks_ref, slc_ref, kg1_ref), (kw_ref, win_ref, kg2_ref)):
        for g in range(A_KV):
            sl = slice(g * A_DH, (g + 1) * A_DH)
            dst[:, sl] = rope(_rms(src[:, sl], kg[...]))
        dst[:, A_KVW:] = src[:, A_KVW:]
    go_ref[...] = jax.nn.sigmoid(gt_ref[...])


def _nsa_prep(n, tm, z, tabs, tab_blocks, q_gain, k_gain):
    tspec = pl.BlockSpec((tm, LANE), lambda i, j: (i % tab_blocks, 0))
    vspec = _full((1, A_DH))
    return _call(
        _prep_body, (n // tm, 1),
        [_zspec(tm, A_WIDTH, ZQ), _zspec(tm, 2 * A_KVW, ZSLC), _zspec(tm, 2 * A_KVW, ZWIN), _zspec(tm, LANE, ZGATE),
         tspec, tspec, tspec, vspec, vspec, vspec],
        [pl.BlockSpec((tm, A_WIDTH), lambda i, j: (i, 0)), pl.BlockSpec((tm, A_WIDTH), lambda i, j: (i, 0)),
         pl.BlockSpec((tm, 2 * A_KVW), lambda i, j: (i, 0)), pl.BlockSpec((tm, 2 * A_KVW), lambda i, j: (i, 0)),
         pl.BlockSpec((tm, LANE), lambda i, j: (i, 0))],
        [jax.ShapeDtypeStruct((n, A_WIDTH), BF), jax.ShapeDtypeStruct((n, A_WIDTH), BF),
         jax.ShapeDtypeStruct((n, 2 * A_KVW), F32), jax.ShapeDtypeStruct((n, 2 * A_KVW), F32),
         jax.ShapeDtypeStruct((n, LANE), F32)],
        name="nsa_prep")(z, z, z, z, *tabs, q_gain.reshape(1, A_DH), k_gain[1:2], k_gain[2:3])


def _cmp_tail(acc, w2, kg, is_k):
    out = _dot(jax.nn.gelu(acc).astype(BF), w2)
    return jnp.where(is_k, _rms(out, kg), out)


def _compress_body(x_ref, pe_ref, w1_ref, w2_ref, kg_ref, o_ref):
    x = (x_ref[...] + pe_ref[...]).astype(BF)
    o_ref[...] = _cmp_tail(_dot(x, w1_ref[...]), w2_ref[...], kg_ref[...], pl.program_id(0) == 0)


def _compress(x, pe, w1, w2, kg):
    _, m, f = x.shape
    tm = min(m, 256)
    return _call(
        _compress_body, (2, m // tm),
        [pl.BlockSpec((None, tm, f), lambda s, i: (s, i, 0)), pl.BlockSpec((None, 1, f), lambda s, i: (s, 0, 0)),
         pl.BlockSpec((None, f, A_DH), lambda s, i: (s, 0, 0)), pl.BlockSpec((None, A_DH, A_DH), lambda s, i: (s, 0, 0)),
         _full((1, A_DH))],
        pl.BlockSpec((None, tm, A_DH), lambda s, i: (s, i, 0)),
        jax.ShapeDtypeStruct((2, m, A_DH), F32), name="compress")(x, pe.reshape(2, 1, f), w1, w2, kg)


PAST_NB = 2


def _cmp_past_body(pt_ref, *refs):
    npg = PAST_LEN // PAGE_SIZE
    pages = refs[:PAST_NB * npg]
    pe_ref, w1_ref, w2_ref, kg_ref, o_ref, buf_ref = refs[PAST_NB * npg:]
    ncb = PAST_LEN // CMP_BLOCK
    for i, pg in enumerate(pages):
        for s in range(2):
            for g in range(A_KV):
                buf_ref[s * A_KV + g, i * PAGE_SIZE:(i + 1) * PAGE_SIZE, :] = pg[:, s, g, :]
    for s in range(2):
        acc = jnp.zeros((A_KV * PAST_NB * ncb, A_DH), F32)
        for r in range(0, CMP_BLOCK, 2):
            parts = []
            for rr in (r, r + 1):
                pe = pe_ref[s, rr:rr + 1, :]
                lhs = [buf_ref[s * A_KV + g, pl.ds(rr, PAST_NB * ncb, stride=CMP_BLOCK), :] for g in range(A_KV)]
                parts.append((jnp.concatenate(lhs, axis=0) + pe).astype(BF))
            acc = acc + _dot(jnp.concatenate(parts, axis=1), w1_ref[s, r * A_DH:(r + 2) * A_DH, :])
        out = _cmp_tail(acc, w2_ref[s], kg_ref[...], s == 0)
        for g in range(A_KV):
            for i in range(PAST_NB):
                o_ref[i, s, g] = out[(g * PAST_NB + i) * ncb:(g * PAST_NB + i + 1) * ncb]


def _compress_past(layer, cache, page_table, pe, w1, w2, kg):
    nb, npg = page_table.shape
    ncb = PAST_LEN // CMP_BLOCK

    def page_spec(i, p):
        return pl.BlockSpec((None, None, PAGE_SIZE, 2, A_KV, A_DH),
                            lambda b, pt: (layer, pt[(b * PAST_NB + i) * npg + p], 0, 0, 0, 0))

    specs = [page_spec(i, p) for i in range(PAST_NB) for p in range(npg)]
    return _call(
        _cmp_past_body, (nb // PAST_NB,),
        specs + [_full((2, CMP_BLOCK, A_DH)), _full((2, CMP_BLOCK * A_DH, A_DH)), _full((2, A_DH, A_DH)), _full((1, A_DH))],
        pl.BlockSpec((PAST_NB, 2, A_KV, ncb, A_DH), lambda b, pt: (b, 0, 0, 0, 0)),
        jax.ShapeDtypeStruct((nb, 2, A_KV, ncb, A_DH), F32),
        scratch=[pltpu.VMEM((2 * A_KV, PAST_NB * PAST_LEN, A_DH), F32)], nsp=1, name="compress_past")(
            page_table.reshape(-1), *([cache] * (PAST_NB * npg)), pe, w1, w2, kg)


def _select(score, nblk):
    j = lax.broadcasted_iota(I32, score.shape, 0)
    rank = jnp.zeros(score.shape, F32)
    for k in range(nblk):
        rk = score[k:k + 1, :]
        rank = rank + ((rk > score) | ((rk == score) & (k < j))).astype(F32)
    return rank


def _cmpattn_body(q_ref, ck_ref, cv_ref, o_ref, sel_ref):
    qi = pl.program_id(1)
    tq = q_ref.shape[0]
    nc = ck_ref.shape[1]
    ns = nc // 2
    base = qi * tq
    sel_rows = []
    for g in range(A_KV):
        q4 = jnp.concatenate([q_ref[:, (g * A_HPG + h) * A_DH:(g * A_HPG + h + 1) * A_DH] for h in range(A_HPG)], axis=0)
        ckg = ck_ref[g].astype(BF)
        cvg = cv_ref[g].astype(BF)
        pos = base + jnp.bitwise_and(lax.broadcasted_iota(I32, (A_HPG * tq, nc), 0), tq - 1)
        blk_end = (lax.broadcasted_iota(I32, (A_HPG * tq, nc), 1) + 1) * CMP_BLOCK - 1
        p = _masked_softmax(_dot_nt(q4, ckg), blk_end <= pos, -1)
        o = _dot(p.astype(BF), cvg)
        for h in range(A_HPG):
            o_ref[:, (g * A_HPG + h) * A_DH:(g * A_HPG + h + 1) * A_DH] = o[h * tq:(h + 1) * tq]
        ckp = jnp.concatenate([ck_ref[g, pl.ds(0, ns, stride=2), :], ck_ref[g, pl.ds(1, ns, stride=2), :]], axis=0).astype(BF)
        row = lax.broadcasted_iota(I32, (nc, A_HPG * tq), 0)
        blk = jnp.where(row < ns, 2 * row, 2 * (row - ns) + 1)
        post = base + jnp.bitwise_and(lax.broadcasted_iota(I32, (nc, A_HPG * tq), 1), tq - 1)
        pt = _masked_softmax(_dot_nt(ckp, q4), (blk + 1) * CMP_BLOCK - 1 <= post, 0)
        imp = pt[:, 0:tq]
        for h in range(1, A_HPG):
            imp = imp + pt[:, h * tq:(h + 1) * tq]
        imp = imp[:ns] + imp[ns:]
        j = lax.broadcasted_iota(I32, (ns, tq), 0)
        pos2 = base + lax.broadcasted_iota(I32, (ns, tq), 1)
        cur = jnp.right_shift(pos2, int(math.log2(SEL_BLOCK)))
        forced = (j == 0) | (j == cur) | (j == cur - 1)
        score = jnp.where(j * SEL_BLOCK <= pos2, imp + jnp.where(forced, FORCE_BONUS, 0.0), -1e9)
        sel_rows.append((_select(score, ns) < min(N_SEL, ns)).astype(F32))
    pad = jnp.zeros((LANE - A_KV * ns, tq), F32)
    sel_ref[...] = jnp.concatenate(sel_rows + [pad], axis=0).T


def _cmp_attn(nb, t, qn, ck, cv):
    tq = 256
    nc = ck.shape[2]
    cspec = pl.BlockSpec((None, A_KV, nc, A_DH), lambda b, i: (b, 0, 0, 0))
    return _call(
        _cmpattn_body, (nb, t // tq),
        [pl.BlockSpec((tq, A_WIDTH), lambda b, i: (b * (t // tq) + i, 0)), cspec, cspec],
        [pl.BlockSpec((tq, A_WIDTH), lambda b, i: (b * (t // tq) + i, 0)),
         pl.BlockSpec((tq, LANE), lambda b, i: (b * (t // tq) + i, 0))],
        [jax.ShapeDtypeStruct((nb * t, A_WIDTH), F32), jax.ShapeDtypeStruct((nb * t, LANE), F32)],
        name="cmp_attn")(qn, ck, cv)


def _slcwin_body(q_ref, sk_ref, wk_ref, sel_ref, os_ref, ow_ref):
    qi = pl.program_id(1)
    tq = q_ref.shape[0]
    t = sk_ref.shape[0]
    ns = t // SEL_BLOCK
    base = qi * tq
    wkeys = WINDOW + tq
    wstart = pl.multiple_of(jnp.maximum(base - WINDOW, 0), tq)
    selb = sel_ref[...].astype(BF)
    lrow = lax.broadcasted_iota(I32, (LANE, t), 0)
    kblk = jnp.right_shift(lax.broadcasted_iota(I32, (LANE, t), 1), int(math.log2(SEL_BLOCK)))
    qpos = base + jnp.bitwise_and(lax.broadcasted_iota(I32, (A_HPG * tq, t), 0), tq - 1)
    causal = lax.broadcasted_iota(I32, (A_HPG * tq, t), 1) <= qpos
    qpw = base + jnp.bitwise_and(lax.broadcasted_iota(I32, (A_HPG * tq, wkeys), 0), tq - 1)
    kpw = wstart + lax.broadcasted_iota(I32, (A_HPG * tq, wkeys), 1)
    wmask = (kpw <= qpw) & (kpw > qpw - WINDOW)
    for g in range(A_KV):
        ksl = slice(g * A_DH, (g + 1) * A_DH)
        vsl = slice(A_KVW + g * A_DH, A_KVW + (g + 1) * A_DH)
        q4 = jnp.concatenate([q_ref[:, (g * A_HPG + h) * A_DH:(g * A_HPG + h + 1) * A_DH] for h in range(A_HPG)], axis=0)
        expand = (lrow == g * ns + kblk).astype(BF)
        picked = _dot(selb, expand)
        smask = (jnp.concatenate([picked] * A_HPG, axis=0) > 0.5) & causal
        p = _masked_softmax(_dot_nt(q4, sk_ref[:, ksl].astype(BF)), smask, -1)
        o = _dot(p.astype(BF), sk_ref[:, vsl].astype(BF))
        pw = _masked_softmax(_dot_nt(q4, wk_ref[pl.ds(wstart, wkeys), ksl].astype(BF)), wmask, -1)
        ow = _dot(pw.astype(BF), wk_ref[pl.ds(wstart, wkeys), vsl].astype(BF))
        for h in range(A_HPG):
            hs = slice((g * A_HPG + h) * A_DH, (g * A_HPG + h + 1) * A_DH)
            os_ref[:, hs] = o[h * tq:(h + 1) * tq]
            ow_ref[:, hs] = ow[h * tq:(h + 1) * tq]


def _slc_win_attn(nb, t, qr, slc_rows, win_rows, sel):
    tq = 128
    nq = t // tq
    rowspec = pl.BlockSpec((tq, A_WIDTH), lambda b, i: (b * nq + i, 0))
    kvspec = pl.BlockSpec((t, 2 * A_KVW), lambda b, i: (b, 0))
    return _call(
        _slcwin_body, (nb, nq),
        [rowspec, kvspec, kvspec, pl.BlockSpec((tq, LANE), lambda b, i: (b * nq + i, 0))],
        [rowspec, rowspec],
        [jax.ShapeDtypeStruct((nb * t, A_WIDTH), F32)] * 2, name="slc_win_attn")(qr, slc_rows, win_rows, sel)


def _combine_body(g_ref, oc_ref, os_ref, ow_ref, o_ref):
    gt = g_ref[...]
    for hd in range(A_HEADS):
        sl = slice(hd * A_DH, (hd + 1) * A_DH)
        acc = (gt[:, hd:hd + 1] * oc_ref[:, sl] + gt[:, A_HEADS + hd:A_HEADS + hd + 1] * os_ref[:, sl]
               + gt[:, 2 * A_HEADS + hd:2 * A_HEADS + hd + 1] * ow_ref[:, sl])
        o_ref[:, sl] = acc.astype(BF)


def _nsa_combine(n, tm, gates, o_cmp, o_slc, o_win):
    spec = pl.BlockSpec((tm, A_WIDTH), lambda i: (i, 0))
    return _call(_combine_body, (n // tm,), [pl.BlockSpec((tm, LANE), lambda i: (i, 0)), spec, spec, spec], spec,
                 jax.ShapeDtypeStruct((n, A_WIDTH), BF), name="nsa_combine")(gates, o_cmp, o_slc, o_win)


def _cmpattn_s_body(q_ref, cp_ref, cn_ref, o_ref, idx_ref):
    pos = PAST_LEN
    q = q_ref[...]
    ncp = cp_ref.shape[2]
    ns = ncp // 2
    nrow = ns + 8
    hrow = lax.broadcasted_iota(I32, (A_HEADS, A_DH), 0)
    o_all = jnp.zeros((A_HEADS, A_DH), F32)
    idx_ref[...] = jnp.zeros(idx_ref.shape, I32)
    new_ok = (ncp + 1) * CMP_BLOCK - 1 <= pos
    for g in range(A_KV):
        ck = cp_ref[0, g].astype(BF)
        cv = cp_ref[1, g].astype(BF)
        ckn = cn_ref[0, g].astype(BF)
        cvn = cn_ref[1, g]
        vp = (lax.broadcasted_iota(I32, (A_HEADS, ncp), 1) + 1) * CMP_BLOCK - 1 <= pos
        vn = (lax.broadcasted_iota(I32, (A_HEADS, 8), 1) == 0) & new_ok
        sp = jnp.where(vp, _dot_nt(q, ck), -jnp.inf)
        sn = jnp.where(vn, _dot_nt(q, ckn), -jnp.inf)
        m = jnp.maximum(jnp.max(sp, axis=-1, keepdims=True), jnp.max(sn, axis=-1, keepdims=True))
        m = _finite_or_zero(m)
        ep = jnp.where(vp, jnp.exp(sp - m), 0.0)
        en = jnp.where(vn, jnp.exp(sn - m), 0.0)
        den = jnp.maximum(jnp.sum(ep, axis=-1, keepdims=True) + jnp.sum(en, axis=-1, keepdims=True), 1e-30)
        o = _dot((ep / den).astype(BF), cv) + (en / den)[:, 0:1] * cvn[0:1, :]
        o_all = jnp.where((hrow >= g * A_HPG) & (hrow < (g + 1) * A_HPG), o, o_all)
        cke = cp_ref[0, g, pl.ds(0, ns, stride=2), :].astype(BF)
        cko = cp_ref[0, g, pl.ds(1, ns, stride=2), :].astype(BF)
        rowe = lax.broadcasted_iota(I32, (ns, A_HEADS), 0)
        ve = (2 * rowe + 1) * CMP_BLOCK - 1 <= pos
        vo = (2 * rowe + 2) * CMP_BLOCK - 1 <= pos
        vnt = (lax.broadcasted_iota(I32, (8, A_HEADS), 0) == 0) & new_ok
        ste = jnp.where(ve, _dot_nt(cke, q), -jnp.inf)
        sto = jnp.where(vo, _dot_nt(cko, q), -jnp.inf)
        stn = jnp.where(vnt, _dot_nt(ckn, q), -jnp.inf)
        mt = jnp.maximum(jnp.maximum(jnp.max(ste, axis=0, keepdims=True), jnp.max(sto, axis=0, keepdims=True)),
                         jnp.max(stn, axis=0, keepdims=True))
        mt = _finite_or_zero(mt)
        ee = jnp.where(ve, jnp.exp(ste - mt), 0.0)
        eo = jnp.where(vo, jnp.exp(sto - mt), 0.0)
        et = jnp.where(vnt, jnp.exp(stn - mt), 0.0)
        dent = jnp.maximum(jnp.sum(ee, axis=0, keepdims=True) + jnp.sum(eo, axis=0, keepdims=True)
                           + jnp.sum(et, axis=0, keepdims=True), 1e-30)
        hlane = lax.broadcasted_iota(I32, (1, A_HEADS), 1)
        ing = (hlane >= g * A_HPG) & (hlane < (g + 1) * A_HPG)

        def imp_of(e):
            return jnp.sum(jnp.where(ing, e / dent, 0.0), axis=1, keepdims=True)

        imp = jnp.concatenate([imp_of(ee) + imp_of(eo), imp_of(et)], axis=0)
        imp = jnp.broadcast_to(imp, (nrow, LANE))
        j = lax.broadcasted_iota(I32, (nrow, LANE), 0)
        cur = pos // SEL_BLOCK
        nsel = ns + 1
        forced = (j == 0) | (j == cur) | (j == cur - 1)
        score = jnp.where(j * SEL_BLOCK <= pos, imp + jnp.where(forced, FORCE_BONUS, 0.0), -1e9)
        score = jnp.where(j < nsel, score, -3e9)
        rank = _select(score, nsel)
        slot = lax.broadcasted_iota(I32, (nrow, LANE), 1)
        hit = (rank == slot.astype(F32)) & (slot < min(N_SEL, nsel)) & (j < nsel)
        idx_ref[g:g + 1, :] = jnp.sum(jnp.where(hit, j, 0), axis=0, keepdims=True)
    o_ref[...] = o_all


def _cmp_attn_sample(q3, ckv_past, ckv_new):
    nb = q3.shape[0]
    ncp = ckv_past.shape[3]
    return _call(
        _cmpattn_s_body, (nb,),
        [pl.BlockSpec((None, A_HEADS, A_DH), lambda b: (b, 0, 0)),
         pl.BlockSpec((None, 2, A_KV, ncp, A_DH), lambda b: (b, 0, 0, 0, 0)),
         pl.BlockSpec((None, 2, A_KV, 8, A_DH), lambda b: (b, 0, 0, 0, 0))],
        [pl.BlockSpec((None, A_HEADS, A_DH), lambda b: (b, 0, 0)), pl.BlockSpec((None, 8, LANE), lambda b: (b, 0, 0))],
        [jax.ShapeDtypeStruct((nb, A_HEADS, A_DH), F32), jax.ShapeDtypeStruct((nb, 8, LANE), I32)],
        name="cmp_attn_sample")(q3, ckv_past, ckv_new)


KV_ROWS = 2 * A_KV


def _flat_attend(q, xb, kmask, sn, vn):
    s = jnp.where(kmask, _dot_nt(q, xb), -jnp.inf)
    m = _finite_or_zero(jnp.maximum(jnp.max(s, axis=-1, keepdims=True), sn))
    e = jnp.where(kmask, jnp.exp(s - m), 0.0)
    en = jnp.exp(sn - m)
    den = jnp.maximum(jnp.sum(e, axis=-1, keepdims=True) + en, 1e-30)
    return _dot(pltpu.roll(e / den, A_KV, 1).astype(BF), xb) + (en / den) * vn


def _slc_s_body(idx_ref, pt_ref, q_ref, *refs):
    blocks = refs[:A_KV * N_SEL]
    new_ref, o_ref = refs[A_KV * N_SEL:]
    b = pl.program_id(0)
    npast = PAST_LEN // SEL_BLOCK
    nfb = SEL_BLOCK * KV_ROWS
    nk = N_SEL * nfb
    q = q_ref[...]
    new = new_ref[...]
    col = lax.broadcasted_iota(I32, (A_HEADS, nk), 1)
    krow = jnp.bitwise_and(col, KV_ROWS - 1)
    slot = jnp.right_shift(col, int(math.log2(nfb)))
    hrow = lax.broadcasted_iota(I32, (A_HEADS, A_DH), 0)
    o_all = jnp.zeros((A_HEADS, A_DH), F32)
    for g in range(A_KV):
        xb = jnp.concatenate([blocks[g * N_SEL + n][...].astype(BF) for n in range(N_SEL)], axis=0)
        past = jnp.zeros((A_HEADS, nk), I32)
        nfresh = jnp.int32(0)
        for n in range(N_SEL):
            is_past = (idx_ref[(b * A_KV + g) * N_SEL + n] < npast).astype(I32)
            past = jnp.where(slot == n, is_past, past)
            nfresh = nfresh + (1 - is_past)
        fresh = jnp.full((A_HEADS, 1), nfresh, I32) > 0
        sn = jnp.where(fresh, jnp.sum(q.astype(F32) * new[g:g + 1, :], axis=-1, keepdims=True), -jnp.inf)
        o = _flat_attend(q, xb, (past > 0) & (krow == g), sn, new[A_KV + g:A_KV + g + 1, :])
        o_all = jnp.where((hrow >= g * A_HPG) & (hrow < (g + 1) * A_HPG), o, o_all)
    o_ref[...] = o_all


def _slc_attn_sample(layer, q3, cache, idx, page_table, slc_new):
    nb = q3.shape[0]
    npg = page_table.shape[1]
    npast = PAST_LEN // SEL_BLOCK
    bpp = PAGE_SIZE // SEL_BLOCK

    def blk_spec(g, n):
        def im(b, idx_ref, pt_ref):
            i = jnp.minimum(idx_ref[(b * A_KV + g) * N_SEL + n], npast - 1)
            return (layer, pt_ref[b * npg + i // bpp], i % bpp, 0)
        return pl.BlockSpec((None, None, SEL_BLOCK * KV_ROWS, A_DH), im)

    return _call(
        _slc_s_body, (nb,),
        [pl.BlockSpec((None, A_HEADS, A_DH), lambda b, i, p: (b, 0, 0))]
        + [blk_spec(g, n) for g in range(A_KV) for n in range(N_SEL)]
        + [pl.BlockSpec((None, KV_ROWS, A_DH), lambda b, i, p: (b, 0, 0))],
        pl.BlockSpec((None, A_HEADS, A_DH), lambda b, i, p: (b, 0, 0)),
        jax.ShapeDtypeStruct((nb, A_HEADS, A_DH), F32), nsp=2, name="slc_attn_sample")(
            idx, page_table.reshape(-1), q3, *([cache] * (A_KV * N_SEL)), slc_new)


WIN_SB = 4


def _win_s_body(q_ref, buf_ref, new_ref, *refs):
    o_ref, nb_ref = refs[-2:]
    pos = PAST_LEN
    nf = buf_ref.shape[1]
    wb = nf // KV_ROWS
    col = lax.broadcasted_iota(I32, (A_HEADS, nf), 1)
    hrow = lax.broadcasted_iota(I32, (A_HEADS, nf), 0)
    kpos = PAST_LEN - wb + jnp.right_shift(col, int(math.log2(KV_ROWS)))
    kmask = ((jnp.bitwise_and(col, KV_ROWS - 1) == jnp.right_shift(hrow, int(math.log2(A_HPG))))
             & (kpos <= pos) & (kpos > pos - WINDOW))
    h1 = lax.broadcasted_iota(I32, (A_HEADS, 1), 0)
    for i in range(WIN_SB):
        q = q_ref[i]
        new = new_ref[i]
        qf = q.astype(F32)
        sn = jnp.zeros((A_HEADS, 1), F32)
        vn = jnp.zeros((A_HEADS, A_DH), F32)
        for g in range(A_KV):
            ing = (h1 >= g * A_HPG) & (h1 < (g + 1) * A_HPG)
            sn = jnp.where(ing, jnp.sum(qf * new[g:g + 1, :], axis=-1, keepdims=True), sn)
            vn = jnp.where(ing, new[A_KV + g:A_KV + g + 1, :], vn)
        o_ref[i] = _flat_attend(q, buf_ref[i].astype(BF), kmask, sn, vn)
        nb_ref[i, pl.ds(0, nf - KV_ROWS), :] = buf_ref[i, pl.ds(KV_ROWS, nf - KV_ROWS), :]
        nb_ref[i, pl.ds(nf - KV_ROWS, KV_ROWS), :] = new


def _win_attn_sample(layer, q3, cache, win_new, acc):
    depth, nb, nf, _ = cache.shape
    ins = [pl.BlockSpec((WIN_SB, A_HEADS, A_DH), lambda b: (b, 0, 0)),
           pl.BlockSpec((None, WIN_SB, nf, A_DH), lambda b: (layer, b, 0, 0)),
           pl.BlockSpec((WIN_SB, KV_ROWS, A_DH), lambda b: (b, 0, 0))]
    args = [q3, cache, win_new]
    if acc is not None:
        ins.append(pl.BlockSpec(memory_space=pl.ANY))
        args.append(acc)
    return _call(
        _win_s_body, (nb // WIN_SB,), ins,
        [pl.BlockSpec((WIN_SB, A_HEADS, A_DH), lambda b: (b, 0, 0)),
         pl.BlockSpec((None, WIN_SB, nf, A_DH), lambda b: (layer, b, 0, 0))],
        [jax.ShapeDtypeStruct((nb, A_HEADS, A_DH), F32), jax.ShapeDtypeStruct(cache.shape, F32)],
        name="win_attn_sample", aliases=None if acc is None else {3: 1})(*args)


def _ssm_disc_body(are_ref, aim_ref, ldt_ref, bre_ref, bim_ref, lre_ref, lim_ref, bbre_ref, bbim_ref):
    a_re, a_im = are_ref[...], aim_ref[...]
    dt = jnp.exp(ldt_ref[...])
    mag = jnp.exp(a_re * dt)
    lam_re = mag * jnp.cos(a_im * dt)
    lam_im = mag * jnp.sin(a_im * dt)
    den = a_re * a_re + a_im * a_im
    co_re = ((lam_re - 1.0) * a_re + lam_im * a_im) / den
    co_im = (lam_im * a_re - (lam_re - 1.0) * a_im) / den
    lre_ref[...] = lam_re
    lim_ref[...] = lam_im
    bbre_ref[...] = co_re * bre_ref[...] - co_im * bim_ref[...]
    bbim_ref[...] = co_re * bim_ref[...] + co_im * bre_ref[...]


def _ssm_disc(a_re, a_im, log_dt, b_re, b_im):
    flat = lambda a: a.reshape(1, S_FLAT)
    bt = lambda b: jnp.transpose(b, (2, 0, 1)).reshape(S_CH, S_FLAT)
    ldt = jnp.broadcast_to(log_dt[:, None], (S_GROUPS, S_STATE))
    v = jax.ShapeDtypeStruct((1, S_FLAT), F32)
    m = jax.ShapeDtypeStruct((S_CH, S_FLAT), F32)
    return pl.pallas_call(_ssm_disc_body, out_shape=[v, v, m, m], name="ssm_disc")(
        flat(a_re), flat(a_im), flat(ldt), bt(b_re), bt(b_im))


def _ssm_tail(hr, hi, u, cre, cim, d, wg, bg):
    y = _dot(hr.astype(BF), cre) - _dot(hi.astype(BF), cim) + d * u
    y = jax.nn.gelu(y)
    return (y * jax.nn.sigmoid(_dot(y.astype(BF), wg) + bg)).astype(BF)


def _ssm_body(u_ref, lre_ref, lim_ref, bbre_ref, bbim_ref, cre_ref, cim_ref, d_ref, wg_ref, bg_ref,
              o_ref, st_ref, xre, xim, hre, him):
    @pl.when(pl.program_id(1) == 0)
    def _():
        st_ref[...] = jnp.zeros_like(st_ref)

    tt = u_ref.shape[0]
    u = u_ref[...]
    ub = u.astype(BF)
    xre[...] = _dot(ub, bbre_ref[...])
    xim[...] = _dot(ub, bbim_ref[...])
    lre, lim = lre_ref[...], lim_ref[...]

    def step(t, c):
        hr, hi = c
        nr = lre * hr - lim * hi + xre[pl.ds(t, 1), :]
        ni = lre * hi + lim * hr + xim[pl.ds(t, 1), :]
        hre[pl.ds(t, 1), :] = nr
        him[pl.ds(t, 1), :] = ni
        return nr, ni

    hr, hi = lax.fori_loop(0, tt, step, (st_ref[0:1, :], st_ref[1:2, :]), unroll=4)
    st_ref[0:1, :] = hr
    st_ref[1:2, :] = hi
    o_ref[...] = _ssm_tail(hre[...], him[...], u, cre_ref[...], cim_ref[...], d_ref[...], wg_ref[...], bg_ref[...])


def _ssm_prompt(nb, t, z, sp):
    tt = 256
    nt = t // tt
    cs = [_full(a.shape) for a in sp]
    return _call(
        _ssm_body, (nb, nt),
        [pl.BlockSpec((tt, S_WIDTH), lambda b, i: (b * nt + i, ZSSM // S_WIDTH))] + cs,
        [pl.BlockSpec((tt, S_WIDTH), lambda b, i: (b * nt + i, 0)), pl.BlockSpec((None, 2, S_FLAT), lambda b, i: (b, 0, 0))],
        [jax.ShapeDtypeStruct((nb * t, S_WIDTH), BF), jax.ShapeDtypeStruct((nb, 2, S_FLAT), F32)],
        scratch=[pltpu.VMEM((tt, S_FLAT), F32)] * 4, name="ssm_prompt")(z, *sp)


def _ssm_s_body(u_ref, h0r_ref, h0i_ref, lre_ref, lim_ref, bbre_ref, bbim_ref, cre_ref, cim_ref, d_ref, wg_ref, bg_ref,
                o_ref, hr_ref, hi_ref):
    u = u_ref[...]
    ub = u.astype(BF)
    lre, lim = lre_ref[...], lim_ref[...]
    h0r, h0i = h0r_ref[...], h0i_ref[...]
    hr = _dot(ub, bbre_ref[...]) + (lre * h0r - lim * h0i)
    hi = _dot(ub, bbim_ref[...]) + (lre * h0i + lim * h0r)
    hr_ref[...] = hr
    hi_ref[...] = hi
    o_ref[...] = _ssm_tail(hr, hi, u, cre_ref[...], cim_ref[...], d_ref[...], wg_ref[...], bg_ref[...])


def _ssm_sample(z, h0r, h0i, sp):
    n = z.shape[0]
    st = pl.BlockSpec((n, S_FLAT), lambda i: (0, 0))
    return _call(
        _ssm_s_body, (1,),
        [pl.BlockSpec((n, S_WIDTH), lambda i: (0, ZSSM // S_WIDTH)), st, st] + [_full(a.shape) for a in sp],
        [pl.BlockSpec((n, S_WIDTH), lambda i: (0, 0)), st, st],
        [jax.ShapeDtypeStruct((n, S_WIDTH), BF), jax.ShapeDtypeStruct((n, S_FLAT), F32), jax.ShapeDtypeStruct((n, S_FLAT), F32)],
        name="ssm_sample")(z, h0r, h0i, *sp)


def _rwkv_prep_body(zr_ref, zk_ref, zv_ref, zwa_ref, zgl_ref, pr_ref, pk_ref, pv_ref, pwa_ref, pgl_ref,
                    mr_ref, mk_ref, mv_ref, mwa_ref, mgl_ref, w0_ref, w2_ref, a0_ref, a2_ref, g2_ref,
                    kkw_ref, kaw_ref, rk_ref, hs_ref,
                    r_ref, w_ref, k_ref, kk_ref, ka_ref, v_ref, g_ref, bonus_ref):
    def mix(z_ref, p_ref, m_ref):
        z = z_ref[...]
        return z + (p_ref[...] - z) * m_ref[...]

    r = mix(zr_ref, pr_ref, mr_ref)
    k = mix(zk_ref, pk_ref, mk_ref)
    v = mix(zv_ref, pv_ref, mv_ref)
    wa = mix(zwa_ref, pwa_ref, mwa_ref)
    gl = mix(zgl_ref, pgl_ref, mgl_ref)
    w = -jax.nn.softplus(-(w0_ref[...] + _dot(jnp.tanh(wa).astype(BF), w2_ref[...]))) - 0.5
    a = jax.nn.sigmoid(a0_ref[...] + _dot(wa.astype(BF), a2_ref[...]))
    hs = hs_ref[...]
    kk = k * kkw_ref[...]
    kk = kk / jnp.maximum(jnp.sqrt(_dot(kk * kk, hs, HI)), 1e-12)
    k = k * (1.0 + (a - 1.0) * kaw_ref[...])
    r_ref[...] = r
    w_ref[...] = jnp.exp(-jnp.exp(w))
    k_ref[...] = k
    kk_ref[...] = kk
    ka_ref[...] = kk * a
    v_ref[...] = v
    g_ref[...] = _dot(jax.nn.sigmoid(gl).astype(BF), g2_ref[...])
    bonus_ref[...] = _dot(r * k * rk_ref[...], hs, HI) * v


def _rwkv_prep(n, tm, z, prev, mu, rp):
    def spec(rows, width, off):
        assert off % width == 0
        return pl.BlockSpec((rows, width), (lambda i: (i, off // width)) if rows == tm else (lambda i: (0, off // width)))

    w = R_WIDTH
    offs = [(w, 0), (w, w), (w, 2 * w), (LANE, 3 * w), (LANE, 3 * w + LANE)]
    out = pl.BlockSpec((tm, w), lambda i: (i, 0))
    return _call(
        _rwkv_prep_body, (n // tm,),
        [spec(tm, wd, ZRW + o) for wd, o in offs] + [spec(tm, wd, o) for wd, o in offs] + [spec(1, wd, o) for wd, o in offs]
        + [_full(a.shape) for a in rp],
        [out] * 8, [jax.ShapeDtypeStruct((n, w), F32)] * 8, name="rwkv_prep")(
            *([z] * 5), *([prev] * 5), *([mu] * 5), *rp)


def _rwkv_scan_body(r_ref, w_ref, k_ref, kk_ref, ka_ref, v_ref, o_ref, st_ref, rs, ws, ks, kks, kas, vs, os_, gam_ref,
                    ms_ref, cs_ref):
    @pl.when(pl.program_id(1) == 0)
    def _():
        st_ref[...] = jnp.zeros_like(st_ref)

    tt = r_ref.shape[0]
    n = R_DH
    lc = RWKV_CHUNK
    for h in range(R_HEADS):
        sl = slice(h * n, (h + 1) * n)
        for src, dst in ((r_ref, rs), (w_ref, ws), (k_ref, ks), (kk_ref, kks), (ka_ref, kas), (v_ref, vs)):
            dst[h] = src[:, sl]
    eye = (lax.broadcasted_iota(I32, (n, n), 0) == lax.broadcasted_iota(I32, (n, n), 1)).astype(F32)
    m0 = jnp.concatenate([eye, jnp.zeros((lc, n), F32)], axis=0)
    xrow = lax.broadcasted_iota(I32, (2 * lc, lc), 0)
    xcol = lax.broadcasted_iota(I32, (2 * lc, lc), 1)
    xsign = jnp.where(xrow < lc, -1.0, 1.0)
    xkeep = jnp.where(xrow < lc, xrow, xrow - lc) <= xcol
    heads = range(R_HEADS)

    def rows_chunk(ci, carry):
        t0 = pl.multiple_of(ci * lc, lc)
        gs = [jnp.ones((1, n), F32)] * R_HEADS
        for h in heads:
            ms_ref[ci, h] = m0
        for s in range(lc):
            t = t0 + s
            for h in heads:
                w = ws[h, pl.ds(t, 1), :]
                c = jnp.sum(ms_ref[ci, h] * kks[h, pl.ds(t, 1), :], axis=1, keepdims=True)
                cs_ref[ci, h, :, s:s + 1] = c
                ms_ref[ci, h] = ms_ref[ci, h] * w - c * kas[h, pl.ds(t, 1), :]
                ms_ref[ci, h, n + s:n + s + 1, :] = ks[h, pl.ds(t, 1), :]
                gs[h] = gs[h] * w
                gam_ref[h, pl.ds(t, 1), :] = gs[h]
        return carry

    def state_chunk(ci, carry):
        sl16 = pl.ds(pl.multiple_of(ci * lc, lc), lc)
        part = []
        for h in heads:
            g = gam_ref[h, sl16, :]
            ginv = 1.0 / g
            rt = g * rs[h, sl16, :]
            v16 = vs[h, sl16, :]
            st0 = st_ref[h]
            stack = jnp.concatenate([st0, v16], axis=0)
            x = _dot_nt(jnp.concatenate([kas[h, sl16, :] * ginv, ks[h, sl16, :] * ginv], axis=0), rt, HI)
            x = jnp.where(xkeep, x * xsign, 0.0)
            cst = _dot_tn(cs_ref[ci, h], stack, HI)
            part.append((x, cst, v16, _dot(rt, st0, HI)))
            st_ref[h] = _dot_tn(ms_ref[ci, h], stack, HI)
        for h in heads:
            x, cst, v16, o0 = part[h]
            os_[h, sl16, :] = o0 + _dot_tn(x, jnp.concatenate([cst, v16], axis=0), HI)
        return carry

    lax.fori_loop(0, tt // lc, rows_chunk, 0)
    lax.fori_loop(0, tt // lc, state_chunk, 0)
    for h in range(R_HEADS):
        o_ref[:, h * n:(h + 1) * n] = os_[h]


def _rwkv_scan(nb, t, r, w, k, kk, ka, v):
    tt = 256
    nt = t // tt
    row = pl.BlockSpec((tt, R_WIDTH), lambda b, i: (b * nt + i, 0))
    return _call(
        _rwkv_scan_body, (nb, nt), [row] * 6,
        [row, pl.BlockSpec((None, R_HEADS, R_DH, R_DH), lambda b, i: (b, 0, 0, 0))],
        [jax.ShapeDtypeStruct((nb * t, R_WIDTH), F32), jax.ShapeDtypeStruct((nb, R_HEADS, R_DH, R_DH), F32)],
        scratch=[pltpu.VMEM((R_HEADS, tt, R_DH), F32)] * 8
        + [pltpu.VMEM((tt // RWKV_CHUNK, R_HEADS, R_DH + RWKV_CHUNK, R_DH), F32),
           pltpu.VMEM((tt // RWKV_CHUNK, R_HEADS, R_DH + RWKV_CHUNK, RWKV_CHUNK), F32)],
        name="rwkv_scan")(r, w, k, kk, ka, v)


RWKV_SB = 8


def _rwkv_step_body(r_ref, w_ref, k_ref, kk_ref, ka_ref, vt_ref, s_ref, ot_ref, so_ref):
    n = R_DH
    lane = lax.broadcasted_iota(I32, (n, RWKV_SB), 1)
    for h in range(R_HEADS):
        sl = slice(h * n, (h + 1) * n)
        ot = jnp.zeros((n, RWKV_SB), F32)
        for s in range(RWKV_SB):
            st = s_ref[s, h]
            rem = jnp.sum(st * kk_ref[s:s + 1, sl], axis=1, keepdims=True)
            st = st * w_ref[s:s + 1, sl] - rem * ka_ref[s:s + 1, sl] + vt_ref[sl, s:s + 1] * k_ref[s:s + 1, sl]
            so_ref[s, h] = st
            ot = jnp.where(lane == s, jnp.sum(st * r_ref[s:s + 1, sl], axis=1, keepdims=True), ot)
        ot_ref[sl, :] = ot


def _rwkv_step(r, w, k, kk, ka, v, state):
    nb = r.shape[0]
    nblk = nb // RWKV_SB
    vt = jnp.transpose(v.reshape(nblk, RWKV_SB, R_WIDTH), (0, 2, 1))
    row = pl.BlockSpec((RWKV_SB, R_WIDTH), lambda i: (i, 0))
    col = pl.BlockSpec((None, R_WIDTH, RWKV_SB), lambda i: (i, 0, 0))
    sts = pl.BlockSpec((RWKV_SB, R_HEADS, R_DH, R_DH), lambda i: (i, 0, 0, 0))
    ot, so = _call(
        _rwkv_step_body, (nblk,), [row] * 5 + [col, sts], [col, sts],
        [jax.ShapeDtypeStruct((nblk, R_WIDTH, RWKV_SB), F32), jax.ShapeDtypeStruct(state.shape, F32)],
        name="rwkv_step")(r, w, k, kk, ka, vt, state)
    return jnp.transpose(ot, (0, 2, 1)).reshape(nb, R_WIDTH), so


def _rwkv_post_body(o_ref, g_ref, bonus_ref, lng_ref, lnb_ref, ha_ref, out_ref):
    o = o_ref[...]
    ha = ha_ref[...]
    cen = o - _dot(o, ha, HI)
    var = _dot(cen * cen, ha, HI)
    y = cen * lax.rsqrt(var + GN_EPS) * lng_ref[...] + lnb_ref[...]
    out_ref[...] = ((y + bonus_ref[...]) * g_ref[...]).astype(BF)


def _rwkv_post(n, tm, o, g, bonus, ln_g, ln_b, havg):
    row = pl.BlockSpec((tm, R_WIDTH), lambda i: (i, 0))
    vec = _full((1, R_WIDTH))
    return _call(_rwkv_post_body, (n // tm,), [row, row, row, vec, vec, _full(havg.shape)], row,
                 jax.ShapeDtypeStruct((n, R_WIDTH), BF), name="rwkv_post")(o, g, bonus, ln_g, ln_b, havg)


def _block_diag(blocks):
    g, a, b = blocks.shape
    return jnp.einsum('gab,gh->gahb', blocks, jnp.eye(g, dtype=blocks.dtype)).reshape(g * a, g * b)


class _LayerParams(NamedTuple):
    norm_g: jax.Array
    ffn: tuple
    w_in: jax.Array
    w_out: jax.Array
    q_gain: jax.Array
    k_gain: jax.Array
    cmp: tuple
    ssm: tuple
    rwkv_mu: jax.Array
    rwkv: tuple
    rwkv_ln: tuple


def _layer_params(l, p):
    d = D_MODEL
    w_in = p['w_in'][l]
    w_in_r = jnp.concatenate([w_in[:, :OFF_GATE], w_in[:, OFF_SSM:], w_in[:, OFF_GATE:OFF_SSM],
                              jnp.zeros((d, ZW - ZGATE - 3 * A_HEADS), F32)], axis=1).astype(BF)
    ffn = tuple((p['ffn_w1'][l, i].astype(BF), p['ffn_w3'][l, i].astype(BF), p['ffn_w2'][l, i].astype(BF)) for i in range(2))
    cmp = (p['nsa_cmp_pe'][l], p['nsa_cmp_w1'][l].astype(BF), p['nsa_cmp_w2'][l].astype(BF), p['nsa_k_gain'][l, 0:1])
    lre, lim, bbre, bbim = _ssm_disc(p['ssm_a_re'][l], p['ssm_a_im'][l], p['ssm_log_dt'][l], p['ssm_b_re'][l], p['ssm_b_im'][l])
    to_gcp = lambda m: jnp.transpose(m.reshape(S_CH, S_GROUPS, S_STATE), (1, 0, 2))
    ssm = (lre, lim, _block_diag(to_gcp(bbre)).astype(BF), _block_diag(to_gcp(bbim)).astype(BF),
           _block_diag(jnp.transpose(p['ssm_c_re'][l], (0, 2, 1))).astype(BF),
           _block_diag(jnp.transpose(p['ssm_c_im'][l], (0, 2, 1))).astype(BF),
           p['ssm_d'][l].reshape(1, S_WIDTH), p['ssm_w_glu'][l].astype(BF), p['ssm_b_glu'][l].reshape(1, S_WIDTH))
    zl = jnp.zeros((R_DECAY_LORA, R_WIDTH), F32)
    vec = lambda a: a.reshape(1, R_WIDTH)
    hsum = _block_diag(jnp.ones((R_HEADS, R_DH, R_DH), F32))
    rwkv = (vec(p['rwkv_w0'][l]), jnp.concatenate([p['rwkv_w2'][l], zl], axis=0).astype(BF),
            vec(p['rwkv_a0'][l]), jnp.concatenate([zl, p['rwkv_a2'][l]], axis=0).astype(BF), p['rwkv_g2'][l].astype(BF),
            vec(p['rwkv_k_k'][l]), vec(p['rwkv_k_a'][l]), vec(p['rwkv_r_k'][l]), hsum)
    return _LayerParams(p['norm_g'][l], ffn, w_in_r, p['w_out'][l].astype(BF), p['nsa_q_gain'][l], p['nsa_k_gain'][l], cmp,
                        ssm, p['rwkv_mu'][l].reshape(1, R_IN), rwkv,
                        (vec(p['rwkv_ln_g'][l]), vec(p['rwkv_ln_b'][l]), hsum / R_DH))


def _layer_prompt(rows, nb, t, x, lp, tabs):
    n = rows.n
    h = _ffn(rows, x, lp.norm_g[0:1], 0, *lp.ffn[0])
    z = _projin(rows, h, lp.norm_g[1:2], 3, lp.w_in)
    qn, qr, slc_rows, win_rows, gates = _nsa_prep(n, 256, z, tabs, t // 256, lp.q_gain, lp.k_gain)
    ncb = t // CMP_BLOCK
    cmp_rows = z[:, ZCMP:ZSLC]
    xcmp = jnp.transpose(cmp_rows.reshape(nb, ncb, CMP_BLOCK, 2, A_KV, A_DH), (3, 0, 4, 1, 2, 5))
    ckv = _compress(xcmp.reshape(2, nb * A_KV * ncb, CMP_BLOCK * A_DH), *lp.cmp).reshape(2, nb, A_KV, ncb, A_DH)
    o_cmp, sel = _cmp_attn(nb, t, qn, ckv[0], ckv[1])
    o_slc, o_win = _slc_win_attn(nb, t, qr, slc_rows, win_rows, sel)
    o_nsa = _nsa_combine(n, 256, gates, o_cmp, o_slc, o_win)
    o_ssm, ssm_st = _ssm_prompt(nb, t, z, lp.ssm)
    zr = z[:, ZRW:ZGATE].reshape(nb, t, R_IN)
    prev = jnp.concatenate([jnp.zeros((nb, 1, R_IN), F32), zr[:, :-1]], axis=1).reshape(n, R_IN)
    r, w, k, kk, ka, v, g, bonus = _rwkv_prep(n, 256, z, prev, lp.rwkv_mu, lp.rwkv)
    o_scan, wkv_t = _rwkv_scan(nb, t, r, w, k, kk, ka, v)
    o_rwkv = _rwkv_post(n, 256, o_scan, g, bonus, *lp.rwkv_ln)
    h = _projout(rows, h, o_nsa, o_ssm, o_rwkv, lp.w_out)
    y = _ffn(rows, h, lp.norm_g[2:3], 6, *lp.ffn[1])
    keep = min(WINDOW, t)
    shape6 = lambda a: a.reshape(nb, -1, 2, A_KV, A_DH)
    state = (shape6(cmp_rows), shape6(slc_rows), shape6(win_rows)[:, t - keep:],
             jnp.stack([ssm_st[:, 0], ssm_st[:, 1]], axis=-1).reshape(nb, S_GROUPS, S_STATE, 2),
             zr[:, -1], jnp.swapaxes(wkv_t, -1, -2))
    return y, state


def _layer_sample(rows, layer, x, lp, tabs, cache_cmp, cache_slc, cache_win, win_acc, page_table, ssm0, shift0, wkv0):
    n = rows.n
    h = _ffn(rows, x, lp.norm_g[0:1], 0, *lp.ffn[0])
    z = _projin(rows, h, lp.norm_g[1:2], 3, lp.w_in)
    qn, qr, slc_new, win_new, gates = _nsa_prep(n, n, z, tabs, 1, lp.q_gain, lp.k_gain)
    cmp_new = z[:, ZCMP:ZSLC]
    ckv_past = _compress_past(layer, cache_cmp, page_table, *lp.cmp)
    xnew = jnp.transpose(cmp_new.reshape(n, 2, A_KV, A_DH), (1, 0, 2, 3)).reshape(2, n * A_KV, A_DH)
    xnew = jnp.pad(xnew, ((0, 0), (0, 0), (0, (CMP_BLOCK - 1) * A_DH)))
    ckv_new = _compress(xnew, *lp.cmp).reshape(2, n, A_KV, 1, A_DH)
    ckv_new = jnp.pad(jnp.transpose(ckv_new, (1, 0, 2, 3, 4)), ((0, 0), (0, 0), (0, 0), (0, 7), (0, 0)))
    q3n = qn.reshape(n, A_HEADS, A_DH)
    q3r = qr.reshape(n, A_HEADS, A_DH)
    o_cmp, idx = _cmp_attn_sample(q3n, ckv_past, ckv_new)
    idx_flat = idx[:, :A_KV, :N_SEL].reshape(-1)
    o_slc = _slc_attn_sample(layer, q3r, cache_slc, idx_flat, page_table, slc_new.reshape(n, KV_ROWS, A_DH))
    o_win, win_acc = _win_attn_sample(layer, q3r, cache_win, win_new.reshape(n, KV_ROWS, A_DH), win_acc)
    flat = lambda a: a.reshape(n, A_WIDTH)
    o_nsa = _nsa_combine(n, n, gates, flat(o_cmp), flat(o_slc), flat(o_win))
    o_ssm, hr, hi = _ssm_sample(z, ssm0[..., 0].reshape(n, S_FLAT), ssm0[..., 1].reshape(n, S_FLAT), lp.ssm)
    r, w, k, kk, ka, v, g, bonus = _rwkv_prep(n, n, z, shift0, lp.rwkv_mu, lp.rwkv)
    o_step, wkv = _rwkv_step(r, w, k, kk, ka, v, wkv0)
    o_rwkv = _rwkv_post(n, n, o_step, g, bonus, *lp.rwkv_ln)
    h = _projout(rows, h, o_nsa, o_ssm, o_rwkv, lp.w_out)
    y = _ffn(rows, h, lp.norm_g[2:3], 6, *lp.ffn[1])
    shape6 = lambda a: a.reshape(n, 1, 2, A_KV, A_DH)
    state = (shape6(cmp_new), shape6(slc_new), None,
             jnp.stack([hr, hi], axis=-1).reshape(n, S_GROUPS, S_STATE, 2), z[:, ZRW:ZGATE], wkv)
    return y, state, win_acc


def kernel(x_prompt, x_sample, cache_nsa_cmp, cache_nsa_slc, cache_nsa_win, state_ssm, state_rwkv_shift, state_rwkv_wkv, page_table, c_prompt, c_sample, norm_g, w_ada, b_ada, ffn_w1, ffn_w3, ffn_w2, w_in, w_out, nsa_q_gain, nsa_k_gain, nsa_cmp_pe, nsa_cmp_w1, nsa_cmp_w2, ssm_a_re, ssm_a_im, ssm_log_dt, ssm_b_re, ssm_b_im, ssm_c_re, ssm_c_im, ssm_d, ssm_w_glu, ssm_b_glu, rwkv_mu, rwkv_w0, rwkv_w2, rwkv_a0, rwkv_a2, rwkv_g2, rwkv_k_k, rwkv_k_a, rwkv_r_k, rwkv_ln_g, rwkv_ln_b):
    p = dict(norm_g=norm_g, ffn_w1=ffn_w1, ffn_w3=ffn_w3, ffn_w2=ffn_w2, w_in=w_in, w_out=w_out, nsa_q_gain=nsa_q_gain,
             nsa_k_gain=nsa_k_gain, nsa_cmp_pe=nsa_cmp_pe, nsa_cmp_w1=nsa_cmp_w1, nsa_cmp_w2=nsa_cmp_w2, ssm_a_re=ssm_a_re,
             ssm_a_im=ssm_a_im, ssm_log_dt=ssm_log_dt, ssm_b_re=ssm_b_re, ssm_b_im=ssm_b_im, ssm_c_re=ssm_c_re,
             ssm_c_im=ssm_c_im, ssm_d=ssm_d, ssm_w_glu=ssm_w_glu, ssm_b_glu=ssm_b_glu, rwkv_mu=rwkv_mu, rwkv_w0=rwkv_w0,
             rwkv_w2=rwkv_w2, rwkv_a0=rwkv_a0, rwkv_a2=rwkv_a2, rwkv_g2=rwkv_g2, rwkv_k_k=rwkv_k_k, rwkv_k_a=rwkv_k_a,
             rwkv_r_k=rwkv_r_k, rwkv_ln_g=rwkv_ln_g, rwkv_ln_b=rwkv_ln_b)
    depth = w_in.shape[0]
    nbp, t, d = x_prompt.shape
    nbs, ts, _ = x_sample.shape
    assert ts == 1 and nbp <= 8 and d == D_MODEL
    cp8 = jnp.pad(c_prompt, ((0, 8 - nbp), (0, 0)))
    mod_p, mod_s = _ada(cp8, c_sample, w_ada, b_ada)
    tabs_p = _rope_tables(jnp.arange(t))
    tabs_s = _rope_tables(jnp.full((nbs,), PAST_LEN))
    hp = x_prompt.reshape(nbp * t, d)
    hs = x_sample.reshape(nbs, d)
    st_p, st_s = [], []
    slc_flat = cache_nsa_slc.reshape(cache_nsa_slc.shape[:2] + (PAGE_SIZE * KV_ROWS, A_DH))
    win_flat = cache_nsa_win.reshape(cache_nsa_win.shape[:2] + (cache_nsa_win.shape[2] * KV_ROWS, A_DH))
    win_acc = None
    for l in range(depth):
        lp = _layer_params(l, p)
        rows_p = _Rows(nbp * t, t, mod_p[l].reshape(8, 1, 9 * d), False, 512)
        rows_s = _Rows(nbs, 1, mod_s[l], True, nbs)
        hp, sp = _layer_prompt(rows_p, nbp, t, hp, lp, tabs_p)
        hs, ss, win_acc = _layer_sample(rows_s, l, hs, lp, tabs_s, cache_nsa_cmp, slc_flat, win_flat, win_acc, page_table,
                                        state_ssm[l], state_rwkv_shift[l], state_rwkv_wkv[l])
        st_p.append(sp)
        st_s.append(ss)
    outs = [hp.reshape(nbp, t, d), hs.reshape(nbs, 1, d)]
    for i in range(6):
        outs.append(jnp.stack([s[i] for s in st_p]))
        outs.append(win_acc.reshape(cache_nsa_win.shape) if i == 2 else jnp.stack([s[i] for s in st_s]))
    return tuple(outs)
```

```python
import functools
import math
from typing import NamedTuple

import jax
import jax.numpy as jnp
from jax import lax
from jax.experimental import pallas as pl
from jax.experimental.pallas import tpu as pltpu

F32 = jnp.float32
BF = jnp.bfloat16
I32 = jnp.int32
HI = lax.Precision.HIGHEST

D_MODEL = 2048
PAST_LEN = 2048
PAGE_SIZE = 128
A_HEADS, A_KV, A_HPG, A_DH = 8, 2, 4, 128
A_WIDTH = A_HEADS * A_DH
A_KVW = A_KV * A_DH
CMP_BLOCK, SEL_BLOCK, N_SEL, WINDOW = 32, 64, 16, 512
ROT_DIM = A_DH // 4
ROPE_THETA = 500000.0
ATT_SCALE = A_DH ** -0.5
FORCE_BONUS = 1.0e4
S_GROUPS, S_CH, S_STATE = 32, 16, 64
S_WIDTH = S_GROUPS * S_CH
S_FLAT = S_GROUPS * S_STATE
R_HEADS, R_DH = 8, 64
R_WIDTH = R_HEADS * R_DH
R_DECAY_LORA, R_A_LORA, R_GATE_LORA = 64, 64, 128
R_IN = 3 * R_WIDTH + R_DECAY_LORA + R_A_LORA + R_GATE_LORA
GN_EPS = 64e-5
OFF_KV = A_WIDTH
OFF_GATE = OFF_KV + 6 * A_KVW
OFF_SSM = OFF_GATE + 3 * A_HEADS
OFF_RWKV = OFF_SSM + S_WIDTH

ZQ, ZCMP, ZSLC, ZWIN, ZSSM, ZRW = 0, 1024, 1536, 2048, 2560, 3072
ZGATE = ZRW + R_IN
ZW = ZGATE + 128
LANE = 128
RWKV_CHUNK = 16


def _dot(a, b, precision=None):
    return jnp.dot(a, b, preferred_element_type=F32, precision=precision)


def _dot_nt(a, b, precision=None):
    return lax.dot_general(a, b, (((1,), (1,)), ((), ())), preferred_element_type=F32, precision=precision)


def _dot_tn(a, b, precision=None):
    return lax.dot_general(a, b, (((0,), (0,)), ((), ())), preferred_element_type=F32, precision=precision)


def _call(body, grid, in_specs, out_specs, out_shape, scratch=(), nsp=0, name=None, aliases=None):
    gs = pltpu.PrefetchScalarGridSpec(num_scalar_prefetch=nsp, grid=grid, in_specs=in_specs, out_specs=out_specs,
                                      scratch_shapes=list(scratch))
    return pl.pallas_call(body, grid_spec=gs, out_shape=out_shape, name=name, input_output_aliases=aliases or {},
                          compiler_params=pltpu.CompilerParams(dimension_semantics=("arbitrary",) * len(grid)))


def _rms(x, g):
    return x * lax.rsqrt(jnp.mean(x * x, axis=-1, keepdims=True) + 1e-6) * g


def _finite_or_zero(m):
    return jnp.where(jnp.abs(m) < jnp.inf, m, 0.0)


def _masked_softmax(s, mask, axis):
    s = jnp.where(mask, s, -jnp.inf)
    m = _finite_or_zero(jnp.max(s, axis=axis, keepdims=True))
    e = jnp.where(mask, jnp.exp(s - m), 0.0)
    return e / jnp.maximum(jnp.sum(e, axis=axis, keepdims=True), 1e-30)


class _Rows(NamedTuple):
    n: int
    seq: int
    mod: jax.Array
    per_row: bool
    tm: int


def _mod_spec(rows, tm, col, width=D_MODEL, jdep=False):
    per = D_MODEL // width
    if rows.per_row:
        return pl.BlockSpec((tm, width), lambda i, j: (i, col * per + (j if jdep else 0)))
    tpb = rows.seq // tm
    return pl.BlockSpec((None, 1, width), lambda i, j: (i // tpb, 0, col * per + (j if jdep else 0)))


def _zspec(tm, width, off):
    assert off % width == 0
    return pl.BlockSpec((tm, width), lambda i, j: (i, off // width))


def _full(shape):
    nd = len(shape)
    return pl.BlockSpec(shape, lambda *a: (0,) * nd)


def _ada_body(cp_ref, cs_ref, w_ref, b_ref, op_ref, os_ref):
    w = w_ref[...].astype(BF)
    b = b_ref[...]
    for c_ref, o_ref in ((cp_ref, op_ref), (cs_ref, os_ref)):
        c = c_ref[...]
        o_ref[...] = _dot((c * jax.nn.sigmoid(c)).astype(BF), w) + b


def _ada(cp8, cs, w_ada, b_ada):
    nl, d, n = w_ada.shape
    tn = 1024
    ns = cs.shape[0]
    return _call(
        _ada_body, (nl, n // tn),
        [_full((8, d)), _full((ns, d)),
         pl.BlockSpec((None, d, tn), lambda l, j: (l, 0, j)),
         pl.BlockSpec((None, 1, tn), lambda l, j: (l, 0, j))],
        [pl.BlockSpec((None, 8, tn), lambda l, j: (l, 0, j)),
         pl.BlockSpec((None, ns, tn), lambda l, j: (l, 0, j))],
        [jax.ShapeDtypeStruct((nl, 8, n), F32), jax.ShapeDtypeStruct((nl, ns, n), F32)],
        name="ada")(cp8, cs, w_ada, b_ada.reshape(nl, 1, n))


def _norm_mod(x, g, scale, shift):
    return _rms(x, g) * (1.0 + scale) + shift


def _ffn_body(x_ref, g_ref, sh_ref, sc_ref, gt_ref, w1_ref, w3_ref, w2_ref, o_ref, xn_ref):
    f = pl.program_id(1)

    @pl.when(f == 0)
    def _():
        xn_ref[...] = _norm_mod(x_ref[...], g_ref[...], sc_ref[...], sh_ref[...]).astype(BF)
        o_ref[...] = jnp.zeros_like(o_ref)

    xn = xn_ref[...]
    h1 = _dot(xn, w1_ref[...])
    h3 = _dot(xn, w3_ref[...])
    o_ref[...] += _dot((h1 * jax.nn.sigmoid(h1) * h3).astype(BF), w2_ref[...])

    @pl.when(f == pl.num_programs(1) - 1)
    def _():
        o_ref[...] = x_ref[...] + 0.5 * gt_ref[...] * o_ref[...]


def _ffn(rows, x, g, col0, w1, w3, w2, layer, which):
    d, ff = w1.shape[2:]
    tm, tf = rows.tm, 512
    return _call(
        _ffn_body, (rows.n // tm, ff // tf),
        [pl.BlockSpec((tm, d), lambda i, f: (i, 0)), _full((1, d)),
         _mod_spec(rows, tm, col0), _mod_spec(rows, tm, col0 + 1), _mod_spec(rows, tm, col0 + 2),
         pl.BlockSpec((None, None, d, tf), lambda i, f: (layer, which, 0, f)),
         pl.BlockSpec((None, None, d, tf), lambda i, f: (layer, which, 0, f)),
         pl.BlockSpec((None, None, tf, d), lambda i, f: (layer, which, f, 0))],
        pl.BlockSpec((tm, d), lambda i, f: (i, 0)),
        jax.ShapeDtypeStruct((rows.n, d), F32),
        scratch=[pltpu.VMEM((tm, d), BF)], name="ffn")(x, g, rows.mod, rows.mod, rows.mod, w1, w3, w2)


def _projin_body(x_ref, g_ref, sh_ref, sc_ref, w_ref, o_ref, xn_ref):
    @pl.when(pl.program_id(1) == 0)
    def _():
        xn_ref[...] = _norm_mod(x_ref[...], g_ref[...], sc_ref[...], sh_ref[...]).astype(BF)

    o_ref[...] = _dot(xn_ref[...], w_ref[...])


def _projin(rows, x, g, col0, w, layer):
    d, n = w.shape[1:]
    tm, tn = rows.tm, 1664
    return _call(
        _projin_body, (rows.n // tm, n // tn),
        [pl.BlockSpec((tm, d), lambda i, j: (i, 0)), _full((1, d)),
         _mod_spec(rows, tm, col0), _mod_spec(rows, tm, col0 + 1),
         pl.BlockSpec((None, d, tn), lambda i, j: (layer, 0, j))],
        pl.BlockSpec((tm, tn), lambda i, j: (i, j)),
        jax.ShapeDtypeStruct((rows.n, n), F32),
        scratch=[pltpu.VMEM((tm, d), BF)], name="proj_in")(x, g, rows.mod, rows.mod, w)


def _projout_body(h_ref, gt_ref, a_ref, b_ref, c_ref, wa_ref, wb_ref, wc_ref, o_ref):
    mix = _dot(a_ref[...], wa_ref[...]) + _dot(b_ref[...], wb_ref[...]) + _dot(c_ref[...], wc_ref[...])
    o_ref[...] = h_ref[...] + gt_ref[...] * mix


def _projout(rows, h, o_nsa, o_ssm, o_rwkv, w):
    d = h.shape[1]
    tm, tn = rows.tm, 512
    nblk = A_WIDTH // S_WIDTH
    return _call(
        _projout_body, (rows.n // tm, d // tn),
        [pl.BlockSpec((tm, tn), lambda i, j: (i, j)), _mod_spec(rows, tm, 5, tn, True),
         pl.BlockSpec((tm, A_WIDTH), lambda i, j: (i, 0)), pl.BlockSpec((tm, S_WIDTH), lambda i, j: (i, 0)),
         pl.BlockSpec((tm, R_WIDTH), lambda i, j: (i, 0)),
         pl.BlockSpec((A_WIDTH, tn), lambda i, j: (0, j)), pl.BlockSpec((S_WIDTH, tn), lambda i, j: (nblk, j)),
         pl.BlockSpec((R_WIDTH, tn), lambda i, j: (nblk + 1, j))],
        pl.BlockSpec((tm, tn), lambda i, j: (i, j)),
        jax.ShapeDtypeStruct((rows.n, d), F32), name="proj_out")(h, rows.mod, o_nsa, o_ssm, o_rwkv, w, w, w)


def _rope_tables(pos):
    half = ROT_DIM // 2
    inv = ROPE_THETA ** (-2.0 * jnp.arange(half, dtype=F32) / ROT_DIM)
    ang = pos.astype(F32)[:, None] * inv[None, :]
    cos, sin = jnp.cos(ang), jnp.sin(ang)
    n = pos.shape[0]
    ct = jnp.concatenate([cos, cos, jnp.ones((n, A_DH - ROT_DIM), F32)], axis=1)
    sa = jnp.concatenate([-sin, jnp.zeros((n, A_DH - half), F32)], axis=1)
    sb = jnp.concatenate([jnp.zeros((n, half), F32), sin, jnp.zeros((n, A_DH - ROT_DIM), F32)], axis=1)
    return ct, sa, sb


def _prep_body(q_ref, ks_ref, kw_ref, gt_ref, ct_ref, sa_ref, sb_ref, qg_ref, kg1_ref, kg2_ref,
               qn_ref, qr_ref, slc_ref, win_ref, go_ref):
    ct, sa, sb = ct_ref[...], sa_ref[...], sb_ref[...]
    half = ROT_DIM // 2

    def rope(x):
        return x * ct + pltpu.roll(x, A_DH - half, 1) * sa + pltpu.roll(x, half, 1) * sb

    for h in range(A_HEADS):
        sl = slice(h * A_DH, (h + 1) * A_DH)
        x = _rms(q_ref[:, sl], qg_ref[...])
        qn_ref[:, sl] = (x * ATT_SCALE).astype(BF)
        qr_ref[:, sl] = (rope(x) * ATT_SCALE).astype(BF)
    for src, dst, kg in ((ks_ref, slc_ref, kg1_ref), (kw_ref, win_ref, kg2_ref)):
        for g in range(A_KV):
            sl = slice(g * A_DH, (g + 1) * A_DH)
            dst[:, sl] = rope(_rms(src[:, sl], kg[...]))
        dst[:, A_KVW:] = src[:, A_KVW:]
    go_ref[...] = jax.nn.sigmoid(gt_ref[...])


def _nsa_prep(n, tm, z, tabs, tab_blocks, q_gain, k_gain):
    tspec = pl.BlockSpec((tm, LANE), lambda i, j: (i % tab_blocks, 0))
    vspec = _full((1, A_DH))
    return _call(
        _prep_body, (n // tm, 1),
        [_zspec(tm, A_WIDTH, ZQ), _zspec(tm, 2 * A_KVW, ZSLC), _zspec(tm, 2 * A_KVW, ZWIN), _zspec(tm, LANE, ZGATE),
         tspec, tspec, tspec, vspec, vspec, vspec],
        [pl.BlockSpec((tm, A_WIDTH), lambda i, j: (i, 0)), pl.BlockSpec((tm, A_WIDTH), lambda i, j: (i, 0)),
         pl.BlockSpec((tm, 2 * A_KVW), lambda i, j: (i, 0)), pl.BlockSpec((tm, 2 * A_KVW), lambda i, j: (i, 0)),
         pl.BlockSpec((tm, LANE), lambda i, j: (i, 0))],
        [jax.ShapeDtypeStruct((n, A_WIDTH), BF), jax.ShapeDtypeStruct((n, A_WIDTH), BF),
         jax.ShapeDtypeStruct((n, 2 * A_KVW), F32), jax.ShapeDtypeStruct((n, 2 * A_KVW), F32),
         jax.ShapeDtypeStruct((n, LANE), F32)],
        name="nsa_prep")(z, z, z, z, *tabs, q_gain.reshape(1, A_DH), k_gain[1:2], k_gain[2:3])


def _cmp_tail(acc, w2, kg, is_k):
    out = _dot(jax.nn.gelu(acc).astype(BF), w2)
    return jnp.where(is_k, _rms(out, kg), out)


def _compress_body(x_ref, pe_ref, w1_ref, w2_ref, kg_ref, o_ref):
    x = (x_ref[...] + pe_ref[...]).astype(BF)
    o_ref[...] = _cmp_tail(_dot(x, w1_ref[...]), w2_ref[...], kg_ref[...], pl.program_id(0) == 0)


def _compress(x, pe, w1, w2, kg):
    _, m, f = x.shape
    tm = min(m, 256)
    return _call(
        _compress_body, (2, m // tm),
        [pl.BlockSpec((None, tm, f), lambda s, i: (s, i, 0)), pl.BlockSpec((None, 1, f), lambda s, i: (s, 0, 0)),
         pl.BlockSpec((None, f, A_DH), lambda s, i: (s, 0, 0)), pl.BlockSpec((None, A_DH, A_DH), lambda s, i: (s, 0, 0)),
         _full((1, A_DH))],
        pl.BlockSpec((None, tm, A_DH), lambda s, i: (s, i, 0)),
        jax.ShapeDtypeStruct((2, m, A_DH), F32), name="compress")(x, pe.reshape(2, 1, f), w1, w2, kg)


PAST_NB = 2


SUB = 8
CMP_RPT = SUB // (2 * A_KV)
CMP_QUAD = 2 * CMP_RPT


def _cmp_past_body(pt_ref, *refs):
    npg = PAST_LEN // PAGE_SIZE
    pages = refs[:PAST_NB * npg]
    pe_ref, w1_ref, w2_ref, kg_ref, o_ref = refs[PAST_NB * npg:]
    kvr = 2 * A_KV
    bpp = PAGE_SIZE // CMP_BLOCK
    nblk = PAST_NB * (PAST_LEN // CMP_BLOCK)
    rows = nblk * SUB
    acc = jnp.zeros((rows, CMP_QUAD * A_DH), F32)
    for q in range(CMP_BLOCK // CMP_QUAD):
        halves = []
        for half in range(2):
            off = (q * 2 + half) * SUB
            tiles = jnp.stack([pg[c * CMP_BLOCK * kvr + off:c * CMP_BLOCK * kvr + off + SUB, :]
                               for pg in pages for c in range(bpp)], axis=0)
            halves.append((tiles + pe_ref[off:off + SUB, :][None]).reshape(rows, A_DH).astype(BF))
        acc = acc + _dot(jnp.concatenate(halves, axis=1), w1_ref[q])
    j = jnp.bitwise_and(lax.broadcasted_iota(I32, (rows, A_DH), 0), SUB - 1)
    is_v = jnp.bitwise_and(j, kvr - 1) >= A_KV
    blk = jnp.right_shift(j, int(math.log2(kvr))) * 2 + is_v.astype(I32)
    sel = jnp.zeros((rows, A_DH), F32)
    for b in range(CMP_QUAD):
        sel = jnp.where(blk == b, acc[:, b * A_DH:(b + 1) * A_DH], sel)
    hid = jax.nn.gelu(sel + pltpu.roll(sel, rows - kvr, 0)).astype(BF)
    out2 = _dot(hid, w2_ref[...])
    out = jnp.where(is_v, out2[:, A_DH:], _rms(out2[:, :A_DH], kg_ref[...]))
    o_ref[...] = out.reshape(nblk, SUB, A_DH)


def _compress_past(layer, cache, page_table, pe, w1, w2, kg):
    nb, npg = page_table.shape
    ncb = PAST_LEN // CMP_BLOCK
    kvr = 2 * A_KV
    nq = CMP_BLOCK // CMP_QUAD

    def page_spec(i, p):
        return pl.BlockSpec((None, None, PAGE_SIZE * kvr, A_DH), lambda b, pt: (layer, pt[(b * PAST_NB + i) * npg + p], 0, 0))

    pe_flat = jnp.broadcast_to(jnp.transpose(pe, (1, 0, 2))[:, :, None, :], (CMP_BLOCK, 2, A_KV, A_DH)).reshape(CMP_BLOCK * kvr, A_DH)
    w1q = jnp.transpose(w1.reshape(2, nq, 2, CMP_RPT, A_DH, A_DH), (1, 2, 4, 3, 0, 5)).reshape(nq, 2 * A_DH, CMP_QUAD * A_DH)
    w2c = jnp.concatenate([w2[0], w2[1]], axis=1)
    specs = [page_spec(i, p) for i in range(PAST_NB) for p in range(npg)]
    out = _call(
        _cmp_past_body, (nb // PAST_NB,),
        specs + [_full(pe_flat.shape), _full(w1q.shape), _full(w2c.shape), _full((1, A_DH))],
        pl.BlockSpec((PAST_NB * ncb, SUB, A_DH), lambda b, pt: (b, 0, 0)),
        jax.ShapeDtypeStruct((nb * ncb, SUB, A_DH), F32), nsp=1, name="compress_past")(
            page_table.reshape(-1), *([cache] * (PAST_NB * npg)), pe_flat, w1q, w2c, kg)
    return jnp.transpose(out[:, :kvr].reshape(nb, ncb, 2, A_KV, A_DH), (0, 2, 3, 1, 4))


def _select(score, nblk):
    j = lax.broadcasted_iota(I32, score.shape, 0)
    rank = jnp.zeros(score.shape, F32)
    for k in range(nblk):
        rk = score[k:k + 1, :]
        rank = rank + ((rk > score) | ((rk == score) & (k < j))).astype(F32)
    return rank


def _cmpattn_body(q_ref, ck_ref, cv_ref, o_ref, sel_ref):
    qi = pl.program_id(1)
    tq = q_ref.shape[0]
    nc = ck_ref.shape[1]
    ns = nc // 2
    base = qi * tq
    sel_rows = []
    for g in range(A_KV):
        q4 = jnp.concatenate([q_ref[:, (g * A_HPG + h) * A_DH:(g * A_HPG + h + 1) * A_DH] for h in range(A_HPG)], axis=0)
        ckg = ck_ref[g].astype(BF)
        cvg = cv_ref[g].astype(BF)
        pos = base + jnp.bitwise_and(lax.broadcasted_iota(I32, (A_HPG * tq, nc), 0), tq - 1)
        blk_end = (lax.broadcasted_iota(I32, (A_HPG * tq, nc), 1) + 1) * CMP_BLOCK - 1
        p = _masked_softmax(_dot_nt(q4, ckg), blk_end <= pos, -1)
        o = _dot(p.astype(BF), cvg)
        for h in range(A_HPG):
            o_ref[:, (g * A_HPG + h) * A_DH:(g * A_HPG + h + 1) * A_DH] = o[h * tq:(h + 1) * tq]
        ckp = jnp.concatenate([ck_ref[g, pl.ds(0, ns, stride=2), :], ck_ref[g, pl.ds(1, ns, stride=2), :]], axis=0).astype(BF)
        row = lax.broadcasted_iota(I32, (nc, A_HPG * tq), 0)
        blk = jnp.where(row < ns, 2 * row, 2 * (row - ns) + 1)
        post = base + jnp.bitwise_and(lax.broadcasted_iota(I32, (nc, A_HPG * tq), 1), tq - 1)
        pt = _masked_softmax(_dot_nt(ckp, q4), (blk + 1) * CMP_BLOCK - 1 <= post, 0)
        imp = pt[:, 0:tq]
        for h in range(1, A_HPG):
            imp = imp + pt[:, h * tq:(h + 1) * tq]
        imp = imp[:ns] + imp[ns:]
        j = lax.broadcasted_iota(I32, (ns, tq), 0)
        pos2 = base + lax.broadcasted_iota(I32, (ns, tq), 1)
        cur = jnp.right_shift(pos2, int(math.log2(SEL_BLOCK)))
        forced = (j == 0) | (j == cur) | (j == cur - 1)
        score = jnp.where(j * SEL_BLOCK <= pos2, imp + jnp.where(forced, FORCE_BONUS, 0.0), -1e9)
        sel_rows.append((_select(score, ns) < min(N_SEL, ns)).astype(F32))
    pad = jnp.zeros((LANE - A_KV * ns, tq), F32)
    sel_ref[...] = jnp.concatenate(sel_rows + [pad], axis=0).T


def _cmp_attn(nb, t, qn, ck, cv):
    tq = 256
    nc = ck.shape[2]
    cspec = pl.BlockSpec((None, A_KV, nc, A_DH), lambda b, i: (b, 0, 0, 0))
    return _call(
        _cmpattn_body, (nb, t // tq),
        [pl.BlockSpec((tq, A_WIDTH), lambda b, i: (b * (t // tq) + i, 0)), cspec, cspec],
        [pl.BlockSpec((tq, A_WIDTH), lambda b, i: (b * (t // tq) + i, 0)),
         pl.BlockSpec((tq, LANE), lambda b, i: (b * (t // tq) + i, 0))],
        [jax.ShapeDtypeStruct((nb * t, A_WIDTH), F32), jax.ShapeDtypeStruct((nb * t, LANE), F32)],
        name="cmp_attn")(qn, ck, cv)


SLC_SPAN = 512


def _slcwin_body(q_ref, sk_ref, wk_ref, sel_ref, os_ref, ow_ref):
    qi = pl.program_id(1)
    tq = q_ref.shape[0]
    t = sk_ref.shape[0]
    ns = t // SEL_BLOCK
    base = qi * tq
    wkeys = WINDOW + tq
    wstart = pl.multiple_of(jnp.maximum(base - WINDOW, 0), tq)
    selb = sel_ref[...].astype(BF)
    qpw = base + jnp.bitwise_and(lax.broadcasted_iota(I32, (A_HPG * tq, wkeys), 0), tq - 1)
    kpw = wstart + lax.broadcasted_iota(I32, (A_HPG * tq, wkeys), 1)
    wmask = (kpw <= qpw) & (kpw > qpw - WINDOW)

    def heads_of(g):
        return jnp.concatenate([q_ref[:, (g * A_HPG + h) * A_DH:(g * A_HPG + h + 1) * A_DH] for h in range(A_HPG)], axis=0)

    def put(ref, g, o):
        for h in range(A_HPG):
            ref[:, (g * A_HPG + h) * A_DH:(g * A_HPG + h + 1) * A_DH] = o[h * tq:(h + 1) * tq]

    for g in range(A_KV):
        ksl = slice(g * A_DH, (g + 1) * A_DH)
        vsl = slice(A_KVW + g * A_DH, A_KVW + (g + 1) * A_DH)
        pw = _masked_softmax(_dot_nt(heads_of(g), wk_ref[pl.ds(wstart, wkeys), ksl].astype(BF)), wmask, -1)
        put(ow_ref, g, _dot(pw.astype(BF), wk_ref[pl.ds(wstart, wkeys), vsl].astype(BF)))

    nspan = t // SLC_SPAN
    for c in range(nspan):
        @pl.when((base + tq - 1) // SLC_SPAN == c)
        def _(c=c):
            nk = (c + 1) * SLC_SPAN
            lrow = lax.broadcasted_iota(I32, (LANE, nk), 0)
            kblk = jnp.right_shift(lax.broadcasted_iota(I32, (LANE, nk), 1), int(math.log2(SEL_BLOCK)))
            qpos = base + jnp.bitwise_and(lax.broadcasted_iota(I32, (A_HPG * tq, nk), 0), tq - 1)
            causal = lax.broadcasted_iota(I32, (A_HPG * tq, nk), 1) <= qpos
            for g in range(A_KV):
                expand = (lrow == g * ns + kblk).astype(BF)
                picked = _dot(selb, expand)
                smask = (jnp.concatenate([picked] * A_HPG, axis=0) > 0.5) & causal
                p = _masked_softmax(_dot_nt(heads_of(g), sk_ref[0:nk, g * A_DH:(g + 1) * A_DH].astype(BF)), smask, -1)
                put(os_ref, g, _dot(p.astype(BF), sk_ref[0:nk, A_KVW + g * A_DH:A_KVW + (g + 1) * A_DH].astype(BF)))


def _slc_win_attn(nb, t, qr, slc_rows, win_rows, sel):
    tq = 128
    nq = t // tq
    rowspec = pl.BlockSpec((tq, A_WIDTH), lambda b, i: (b * nq + i, 0))
    kvspec = pl.BlockSpec((t, 2 * A_KVW), lambda b, i: (b, 0))
    return _call(
        _slcwin_body, (nb, nq),
        [rowspec, kvspec, kvspec, pl.BlockSpec((tq, LANE), lambda b, i: (b * nq + i, 0))],
        [rowspec, rowspec],
        [jax.ShapeDtypeStruct((nb * t, A_WIDTH), F32)] * 2, name="slc_win_attn")(qr, slc_rows, win_rows, sel)


def _combine_body(g_ref, oc_ref, os_ref, ow_ref, o_ref):
    gt = g_ref[...]
    for hd in range(A_HEADS):
        sl = slice(hd * A_DH, (hd + 1) * A_DH)
        acc = (gt[:, hd:hd + 1] * oc_ref[:, sl] + gt[:, A_HEADS + hd:A_HEADS + hd + 1] * os_ref[:, sl]
               + gt[:, 2 * A_HEADS + hd:2 * A_HEADS + hd + 1] * ow_ref[:, sl])
        o_ref[:, sl] = acc.astype(BF)


def _nsa_combine(n, tm, gates, o_cmp, o_slc, o_win):
    spec = pl.BlockSpec((tm, A_WIDTH), lambda i: (i, 0))
    return _call(_combine_body, (n // tm,), [pl.BlockSpec((tm, LANE), lambda i: (i, 0)), spec, spec, spec], spec,
                 jax.ShapeDtypeStruct((n, A_WIDTH), BF), name="nsa_combine")(gates, o_cmp, o_slc, o_win)


def _cmpattn_s_body(q_ref, cp_ref, cn_ref, o_ref, idx_ref):
    pos = PAST_LEN
    q = q_ref[...]
    ncp = cp_ref.shape[2]
    ns = ncp // 2
    nrow = ns + 8
    hrow = lax.broadcasted_iota(I32, (A_HEADS, A_DH), 0)
    o_all = jnp.zeros((A_HEADS, A_DH), F32)
    idx_ref[...] = jnp.zeros(idx_ref.shape, I32)
    new_ok = (ncp + 1) * CMP_BLOCK - 1 <= pos
    for g in range(A_KV):
        ck = cp_ref[0, g].astype(BF)
        cv = cp_ref[1, g].astype(BF)
        ckn = cn_ref[0, g].astype(BF)
        cvn = cn_ref[1, g]
        vp = (lax.broadcasted_iota(I32, (A_HEADS, ncp), 1) + 1) * CMP_BLOCK - 1 <= pos
        vn = (lax.broadcasted_iota(I32, (A_HEADS, 8), 1) == 0) & new_ok
        sp = jnp.where(vp, _dot_nt(q, ck), -jnp.inf)
        sn = jnp.where(vn, _dot_nt(q, ckn), -jnp.inf)
        m = jnp.maximum(jnp.max(sp, axis=-1, keepdims=True), jnp.max(sn, axis=-1, keepdims=True))
        m = _finite_or_zero(m)
        ep = jnp.where(vp, jnp.exp(sp - m), 0.0)
        en = jnp.where(vn, jnp.exp(sn - m), 0.0)
        den = jnp.maximum(jnp.sum(ep, axis=-1, keepdims=True) + jnp.sum(en, axis=-1, keepdims=True), 1e-30)
        o = _dot((ep / den).astype(BF), cv) + (en / den)[:, 0:1] * cvn[0:1, :]
        o_all = jnp.where((hrow >= g * A_HPG) & (hrow < (g + 1) * A_HPG), o, o_all)
        cke = cp_ref[0, g, pl.ds(0, ns, stride=2), :].astype(BF)
        cko = cp_ref[0, g, pl.ds(1, ns, stride=2), :].astype(BF)
        rowe = lax.broadcasted_iota(I32, (ns, A_HEADS), 0)
        ve = (2 * rowe + 1) * CMP_BLOCK - 1 <= pos
        vo = (2 * rowe + 2) * CMP_BLOCK - 1 <= pos
        vnt = (lax.broadcasted_iota(I32, (8, A_HEADS), 0) == 0) & new_ok
        ste = jnp.where(ve, _dot_nt(cke, q), -jnp.inf)
        sto = jnp.where(vo, _dot_nt(cko, q), -jnp.inf)
        stn = jnp.where(vnt, _dot_nt(ckn, q), -jnp.inf)
        mt = jnp.maximum(jnp.maximum(jnp.max(ste, axis=0, keepdims=True), jnp.max(sto, axis=0, keepdims=True)),
                         jnp.max(stn, axis=0, keepdims=True))
        mt = _finite_or_zero(mt)
        ee = jnp.where(ve, jnp.exp(ste - mt), 0.0)
        eo = jnp.where(vo, jnp.exp(sto - mt), 0.0)
        et = jnp.where(vnt, jnp.exp(stn - mt), 0.0)
        dent = jnp.maximum(jnp.sum(ee, axis=0, keepdims=True) + jnp.sum(eo, axis=0, keepdims=True)
                           + jnp.sum(et, axis=0, keepdims=True), 1e-30)
        hlane = lax.broadcasted_iota(I32, (1, A_HEADS), 1)
        ing = (hlane >= g * A_HPG) & (hlane < (g + 1) * A_HPG)

        def imp_of(e):
            return jnp.sum(jnp.where(ing, e / dent, 0.0), axis=1, keepdims=True)

        imp = jnp.concatenate([imp_of(ee) + imp_of(eo), imp_of(et)], axis=0)
        imp = jnp.broadcast_to(imp, (nrow, LANE))
        j = lax.broadcasted_iota(I32, (nrow, LANE), 0)
        cur = pos // SEL_BLOCK
        nsel = ns + 1
        forced = (j == 0) | (j == cur) | (j == cur - 1)
        score = jnp.where(j * SEL_BLOCK <= pos, imp + jnp.where(forced, FORCE_BONUS, 0.0), -1e9)
        score = jnp.where(j < nsel, score, -3e9)
        rank = _select(score, nsel)
        slot = lax.broadcasted_iota(I32, (nrow, LANE), 1)
        hit = (rank == slot.astype(F32)) & (slot < min(N_SEL, nsel)) & (j < nsel)
        idx_ref[g:g + 1, :] = jnp.sum(jnp.where(hit, j, 0), axis=0, keepdims=True)
    o_ref[...] = o_all


def _cmp_attn_sample(q3, ckv_past, ckv_new):
    nb = q3.shape[0]
    ncp = ckv_past.shape[3]
    return _call(
        _cmpattn_s_body, (nb,),
        [pl.BlockSpec((None, A_HEADS, A_DH), lambda b: (b, 0, 0)),
         pl.BlockSpec((None, 2, A_KV, ncp, A_DH), lambda b: (b, 0, 0, 0, 0)),
         pl.BlockSpec((None, 2, A_KV, 8, A_DH), lambda b: (b, 0, 0, 0, 0))],
        [pl.BlockSpec((None, A_HEADS, A_DH), lambda b: (b, 0, 0)), pl.BlockSpec((None, 8, LANE), lambda b: (b, 0, 0))],
        [jax.ShapeDtypeStruct((nb, A_HEADS, A_DH), F32), jax.ShapeDtypeStruct((nb, 8, LANE), I32)],
        name="cmp_attn_sample")(q3, ckv_past, ckv_new)


KV_ROWS = 2 * A_KV


def _flat_attend(q, xb, kmask, sn, vn):
    s = jnp.where(kmask, _dot_nt(q, xb), -jnp.inf)
    m = _finite_or_zero(jnp.maximum(jnp.max(s, axis=-1, keepdims=True), sn))
    e = jnp.where(kmask, jnp.exp(s - m), 0.0)
    en = jnp.exp(sn - m)
    den = jnp.maximum(jnp.sum(e, axis=-1, keepdims=True) + en, 1e-30)
    return _dot(pltpu.roll(e / den, A_KV, 1).astype(BF), xb) + (en / den) * vn


def _slc_s_body(idx_ref, pt_ref, q_ref, *refs):
    blocks = refs[:A_KV * N_SEL]
    new_ref, o_ref = refs[A_KV * N_SEL:]
    b = pl.program_id(0)
    npast = PAST_LEN // SEL_BLOCK
    nfb = SEL_BLOCK * KV_ROWS
    nk = N_SEL * nfb
    q = q_ref[...]
    new = new_ref[...]
    col = lax.broadcasted_iota(I32, (A_HEADS, nk), 1)
    krow = jnp.bitwise_and(col, KV_ROWS - 1)
    slot = jnp.right_shift(col, int(math.log2(nfb)))
    hrow = lax.broadcasted_iota(I32, (A_HEADS, A_DH), 0)
    o_all = jnp.zeros((A_HEADS, A_DH), F32)
    for g in range(A_KV):
        xb = jnp.concatenate([blocks[g * N_SEL + n][...].astype(BF) for n in range(N_SEL)], axis=0)
        past = jnp.zeros((A_HEADS, nk), I32)
        nfresh = jnp.int32(0)
        for n in range(N_SEL):
            is_past = (idx_ref[(b * A_KV + g) * N_SEL + n] < npast).astype(I32)
            past = jnp.where(slot == n, is_past, past)
            nfresh = nfresh + (1 - is_past)
        fresh = jnp.full((A_HEADS, 1), nfresh, I32) > 0
        sn = jnp.where(fresh, jnp.sum(q.astype(F32) * new[g:g + 1, :], axis=-1, keepdims=True), -jnp.inf)
        o = _flat_attend(q, xb, (past > 0) & (krow == g), sn, new[A_KV + g:A_KV + g + 1, :])
        o_all = jnp.where((hrow >= g * A_HPG) & (hrow < (g + 1) * A_HPG), o, o_all)
    o_ref[...] = o_all


def _slc_attn_sample(layer, q3, cache, idx, page_table, slc_new):
    nb = q3.shape[0]
    npg = page_table.shape[1]
    npast = PAST_LEN // SEL_BLOCK
    bpp = PAGE_SIZE // SEL_BLOCK

    def blk_spec(g, n):
        def im(b, idx_ref, pt_ref):
            i = jnp.minimum(idx_ref[(b * A_KV + g) * N_SEL + n], npast - 1)
            return (layer, pt_ref[b * npg + i // bpp], i % bpp, 0)
        return pl.BlockSpec((None, None, SEL_BLOCK * KV_ROWS, A_DH), im)

    return _call(
        _slc_s_body, (nb,),
        [pl.BlockSpec((None, A_HEADS, A_DH), lambda b, i, p: (b, 0, 0))]
        + [blk_spec(g, n) for g in range(A_KV) for n in range(N_SEL)]
        + [pl.BlockSpec((None, KV_ROWS, A_DH), lambda b, i, p: (b, 0, 0))],
        pl.BlockSpec((None, A_HEADS, A_DH), lambda b, i, p: (b, 0, 0)),
        jax.ShapeDtypeStruct((nb, A_HEADS, A_DH), F32), nsp=2, name="slc_attn_sample")(
            idx, page_table.reshape(-1), q3, *([cache] * (A_KV * N_SEL)), slc_new)


WIN_SB = 4


def _win_s_body(q_ref, buf_ref, new_ref, *refs):
    o_ref, nb_ref = refs[-2:]
    pos = PAST_LEN
    nf = buf_ref.shape[1]
    wb = nf // KV_ROWS
    col = lax.broadcasted_iota(I32, (A_HEADS, nf), 1)
    hrow = lax.broadcasted_iota(I32, (A_HEADS, nf), 0)
    kpos = PAST_LEN - wb + jnp.right_shift(col, int(math.log2(KV_ROWS)))
    kmask = ((jnp.bitwise_and(col, KV_ROWS - 1) == jnp.right_shift(hrow, int(math.log2(A_HPG))))
             & (kpos <= pos) & (kpos > pos - WINDOW))
    h1 = lax.broadcasted_iota(I32, (A_HEADS, 1), 0)
    for i in range(WIN_SB):
        q = q_ref[i]
        new = new_ref[i]
        qf = q.astype(F32)
        sn = jnp.zeros((A_HEADS, 1), F32)
        vn = jnp.zeros((A_HEADS, A_DH), F32)
        for g in range(A_KV):
            ing = (h1 >= g * A_HPG) & (h1 < (g + 1) * A_HPG)
            sn = jnp.where(ing, jnp.sum(qf * new[g:g + 1, :], axis=-1, keepdims=True), sn)
            vn = jnp.where(ing, new[A_KV + g:A_KV + g + 1, :], vn)
        o_ref[i] = _flat_attend(q, buf_ref[i].astype(BF), kmask, sn, vn)
        nb_ref[i, pl.ds(0, nf - KV_ROWS), :] = buf_ref[i, pl.ds(KV_ROWS, nf - KV_ROWS), :]
        nb_ref[i, pl.ds(nf - KV_ROWS, KV_ROWS), :] = new


def _win_attn_sample(layer, q3, cache, win_new, acc):
    depth, nb, nf, _ = cache.shape
    ins = [pl.BlockSpec((WIN_SB, A_HEADS, A_DH), lambda b: (b, 0, 0)),
           pl.BlockSpec((None, WIN_SB, nf, A_DH), lambda b: (layer, b, 0, 0)),
           pl.BlockSpec((WIN_SB, KV_ROWS, A_DH), lambda b: (b, 0, 0))]
    args = [q3, cache, win_new]
    if acc is not None:
        ins.append(pl.BlockSpec(memory_space=pl.ANY))
        args.append(acc)
    return _call(
        _win_s_body, (nb // WIN_SB,), ins,
        [pl.BlockSpec((WIN_SB, A_HEADS, A_DH), lambda b: (b, 0, 0)),
         pl.BlockSpec((None, WIN_SB, nf, A_DH), lambda b: (layer, b, 0, 0))],
        [jax.ShapeDtypeStruct((nb, A_HEADS, A_DH), F32), jax.ShapeDtypeStruct(cache.shape, F32)],
        name="win_attn_sample", aliases=None if acc is None else {3: 1})(*args)


def _ssm_disc_body(are_ref, aim_ref, ldt_ref, bre_ref, bim_ref, lre_ref, lim_ref, bbre_ref, bbim_ref):
    a_re, a_im = are_ref[...], aim_ref[...]
    dt = jnp.exp(ldt_ref[...])
    mag = jnp.exp(a_re * dt)
    lam_re = mag * jnp.cos(a_im * dt)
    lam_im = mag * jnp.sin(a_im * dt)
    den = a_re * a_re + a_im * a_im
    co_re = ((lam_re - 1.0) * a_re + lam_im * a_im) / den
    co_im = (lam_im * a_re - (lam_re - 1.0) * a_im) / den
    lre_ref[...] = lam_re
    lim_ref[...] = lam_im
    bbre_ref[...] = co_re * bre_ref[...] - co_im * bim_ref[...]
    bbim_ref[...] = co_re * bim_ref[...] + co_im * bre_ref[...]


def _ssm_disc(a_re, a_im, log_dt, b_re, b_im):
    flat = lambda a: a.reshape(1, S_FLAT)
    bt = lambda b: jnp.transpose(b, (2, 0, 1)).reshape(S_CH, S_FLAT)
    ldt = jnp.broadcast_to(log_dt[:, None], (S_GROUPS, S_STATE))
    v = jax.ShapeDtypeStruct((1, S_FLAT), F32)
    m = jax.ShapeDtypeStruct((S_CH, S_FLAT), F32)
    return pl.pallas_call(_ssm_disc_body, out_shape=[v, v, m, m], name="ssm_disc")(
        flat(a_re), flat(a_im), flat(ldt), bt(b_re), bt(b_im))


def _ssm_tail(hr, hi, u, cre, cim, d, wg, bg):
    y = _dot(hr.astype(BF), cre) - _dot(hi.astype(BF), cim) + d * u
    y = jax.nn.gelu(y)
    return (y * jax.nn.sigmoid(_dot(y.astype(BF), wg) + bg)).astype(BF)


def _ssm_body(u_ref, lre_ref, lim_ref, bbre_ref, bbim_ref, cre_ref, cim_ref, d_ref, wg_ref, bg_ref,
              o_ref, st_ref, xre, xim, hre, him):
    @pl.when(pl.program_id(1) == 0)
    def _():
        st_ref[...] = jnp.zeros_like(st_ref)

    tt = u_ref.shape[0]
    u = u_ref[...]
    ub = u.astype(BF)
    xre[...] = _dot(ub, bbre_ref[...])
    xim[...] = _dot(ub, bbim_ref[...])
    lre, lim = lre_ref[...], lim_ref[...]

    def step(t, c):
        hr, hi = c
        nr = lre * hr - lim * hi + xre[pl.ds(t, 1), :]
        ni = lre * hi + lim * hr + xim[pl.ds(t, 1), :]
        hre[pl.ds(t, 1), :] = nr
        him[pl.ds(t, 1), :] = ni
        return nr, ni

    hr, hi = lax.fori_loop(0, tt, step, (st_ref[0:1, :], st_ref[1:2, :]), unroll=4)
    st_ref[0:1, :] = hr
    st_ref[1:2, :] = hi
    o_ref[...] = _ssm_tail(hre[...], him[...], u, cre_ref[...], cim_ref[...], d_ref[...], wg_ref[...], bg_ref[...])


def _ssm_prompt(nb, t, z, sp):
    tt = 256
    nt = t // tt
    cs = [_full(a.shape) for a in sp]
    return _call(
        _ssm_body, (nb, nt),
        [pl.BlockSpec((tt, S_WIDTH), lambda b, i: (b * nt + i, ZSSM // S_WIDTH))] + cs,
        [pl.BlockSpec((tt, S_WIDTH), lambda b, i: (b * nt + i, 0)), pl.BlockSpec((None, 2, S_FLAT), lambda b, i: (b, 0, 0))],
        [jax.ShapeDtypeStruct((nb * t, S_WIDTH), BF), jax.ShapeDtypeStruct((nb, 2, S_FLAT), F32)],
        scratch=[pltpu.VMEM((tt, S_FLAT), F32)] * 4, name="ssm_prompt")(z, *sp)


def _ssm_s_body(u_ref, h0r_ref, h0i_ref, lre_ref, lim_ref, bbre_ref, bbim_ref, cre_ref, cim_ref, d_ref, wg_ref, bg_ref,
                o_ref, hr_ref, hi_ref):
    u = u_ref[...]
    ub = u.astype(BF)
    lre, lim = lre_ref[...], lim_ref[...]
    h0r, h0i = h0r_ref[...], h0i_ref[...]
    hr = _dot(ub, bbre_ref[...]) + (lre * h0r - lim * h0i)
    hi = _dot(ub, bbim_ref[...]) + (lre * h0i + lim * h0r)
    hr_ref[...] = hr
    hi_ref[...] = hi
    o_ref[...] = _ssm_tail(hr, hi, u, cre_ref[...], cim_ref[...], d_ref[...], wg_ref[...], bg_ref[...])


def _ssm_sample(z, h0r, h0i, sp):
    n = z.shape[0]
    st = pl.BlockSpec((n, S_FLAT), lambda i: (0, 0))
    return _call(
        _ssm_s_body, (1,),
        [pl.BlockSpec((n, S_WIDTH), lambda i: (0, ZSSM // S_WIDTH)), st, st] + [_full(a.shape) for a in sp],
        [pl.BlockSpec((n, S_WIDTH), lambda i: (0, 0)), st, st],
        [jax.ShapeDtypeStruct((n, S_WIDTH), BF), jax.ShapeDtypeStruct((n, S_FLAT), F32), jax.ShapeDtypeStruct((n, S_FLAT), F32)],
        name="ssm_sample")(z, h0r, h0i, *sp)


def _rwkv_prep_body(tiles_per_seq, *refs):
    z_refs, refs = refs[:5], refs[5:]
    if tiles_per_seq is None:
        p_refs, refs = refs[:5], refs[5:]
    m_refs, refs = refs[:5], refs[5:]
    w0_ref, w2_ref, a0_ref, a2_ref, g2_ref, kkw_ref, kaw_ref, rk_ref, hs_ref = refs[:9]
    r_ref, w_ref, k_ref, kk_ref, ka_ref, v_ref, g_ref, bonus_ref = refs[9:17]
    carry_refs = refs[17:]

    def mix(i):
        z = z_refs[i][...]
        if tiles_per_seq is None:
            prev = p_refs[i][...]
        else:
            first = jnp.where(pl.program_id(0) % tiles_per_seq == 0, 0.0, carry_refs[i][...])
            prev = jnp.where(lax.broadcasted_iota(I32, z.shape, 0) == 0, first, pltpu.roll(z, 1, 0))
            carry_refs[i][...] = z[z.shape[0] - 1:, :]
        return z + (prev - z) * m_refs[i][...]

    r, k, v = mix(0), mix(1), mix(2)
    wa = mix(3)
    gl = mix(4)
    w = -jax.nn.softplus(-(w0_ref[...] + _dot(jnp.tanh(wa).astype(BF), w2_ref[...]))) - 0.5
    a = jax.nn.sigmoid(a0_ref[...] + _dot(wa.astype(BF), a2_ref[...]))
    hs = hs_ref[...]
    kk = k * kkw_ref[...]
    kk = kk / jnp.maximum(jnp.sqrt(_dot(kk * kk, hs, HI)), 1e-12)
    k = k * (1.0 + (a - 1.0) * kaw_ref[...])
    r_ref[...] = r
    w_ref[...] = jnp.exp(-jnp.exp(w))
    k_ref[...] = k
    kk_ref[...] = kk
    ka_ref[...] = kk * a
    v_ref[...] = v
    g_ref[...] = _dot(jax.nn.sigmoid(gl).astype(BF), g2_ref[...])
    bonus_ref[...] = _dot(r * k * rk_ref[...], hs, HI) * v


def _rwkv_prep(n, tm, z, prev, mu, rp, seq=None):
    def spec(rows, width, off):
        assert off % width == 0
        return pl.BlockSpec((rows, width), (lambda i: (i, off // width)) if rows == tm else (lambda i: (0, off // width)))

    w = R_WIDTH
    offs = [(w, 0), (w, w), (w, 2 * w), (LANE, 3 * w), (LANE, 3 * w + LANE)]
    out = pl.BlockSpec((tm, w), lambda i: (i, 0))
    given = prev is not None
    return _call(
        functools.partial(_rwkv_prep_body, None if given else seq // tm), (n // tm,),
        [spec(tm, wd, ZRW + o) for wd, o in offs] + ([spec(tm, wd, o) for wd, o in offs] if given else [])
        + [spec(1, wd, o) for wd, o in offs] + [_full(a.shape) for a in rp],
        [out] * 8, [jax.ShapeDtypeStruct((n, w), F32)] * 8,
        scratch=[] if given else [pltpu.VMEM((1, wd), F32) for wd, _ in offs], name="rwkv_prep")(
            *([z] * 5), *([prev] * 5 if given else []), *([mu] * 5), *rp)


def _rwkv_scan_body(r_ref, w_ref, k_ref, kk_ref, ka_ref, v_ref, o_ref, st_ref, rs, ws, ks, kks, kas, vs, os_, gam_ref,
                    ms_ref, cs_ref):
    @pl.when(pl.program_id(1) == 0)
    def _():
        st_ref[...] = jnp.zeros_like(st_ref)

    tt = r_ref.shape[0]
    n = R_DH
    lc = RWKV_CHUNK
    for h in range(R_HEADS):
        sl = slice(h * n, (h + 1) * n)
        for src, dst in ((r_ref, rs), (w_ref, ws), (k_ref, ks), (kk_ref, kks), (ka_ref, kas), (v_ref, vs)):
            dst[h] = src[:, sl]
    eye = (lax.broadcasted_iota(I32, (n, n), 0) == lax.broadcasted_iota(I32, (n, n), 1)).astype(F32)
    m0 = jnp.concatenate([eye, jnp.zeros((lc, n), F32)], axis=0)
    xrow = lax.broadcasted_iota(I32, (2 * lc, lc), 0)
    xcol = lax.broadcasted_iota(I32, (2 * lc, lc), 1)
    xsign = jnp.where(xrow < lc, -1.0, 1.0)
    xkeep = jnp.where(xrow < lc, xrow, xrow - lc) <= xcol
    heads = range(R_HEADS)

    def rows_chunk(ci, carry):
        t0 = pl.multiple_of(ci * lc, lc)
        gs = [jnp.ones((1, n), F32)] * R_HEADS
        for h in heads:
            ms_ref[ci, h] = m0
        for s in range(lc):
            t = t0 + s
            for h in heads:
                w = ws[h, pl.ds(t, 1), :]
                c = jnp.sum(ms_ref[ci, h] * kks[h, pl.ds(t, 1), :], axis=1, keepdims=True)
                cs_ref[ci, h, :, s:s + 1] = c
                ms_ref[ci, h] = ms_ref[ci, h] * w - c * kas[h, pl.ds(t, 1), :]
                ms_ref[ci, h, n + s:n + s + 1, :] = ks[h, pl.ds(t, 1), :]
                gs[h] = gs[h] * w
                gam_ref[h, pl.ds(t, 1), :] = gs[h]
        return carry

    def state_chunk(ci, carry):
        sl16 = pl.ds(pl.multiple_of(ci * lc, lc), lc)
        part = []
        for h in heads:
            g = gam_ref[h, sl16, :]
            ginv = 1.0 / g
            rt = g * rs[h, sl16, :]
            v16 = vs[h, sl16, :]
            st0 = st_ref[h]
            stack = jnp.concatenate([st0, v16], axis=0)
            x = _dot_nt(jnp.concatenate([kas[h, sl16, :] * ginv, ks[h, sl16, :] * ginv], axis=0), rt, HI)
            x = jnp.where(xkeep, x * xsign, 0.0)
            cst = _dot_tn(cs_ref[ci, h], stack, HI)
            part.append((x, cst, v16, _dot(rt, st0, HI)))
            st_ref[h] = _dot_tn(ms_ref[ci, h], stack, HI)
        for h in heads:
            x, cst, v16, o0 = part[h]
            os_[h, sl16, :] = o0 + _dot_tn(x, jnp.concatenate([cst, v16], axis=0), HI)
        return carry

    lax.fori_loop(0, tt // lc, rows_chunk, 0)
    lax.fori_loop(0, tt // lc, state_chunk, 0)
    for h in range(R_HEADS):
        o_ref[:, h * n:(h + 1) * n] = os_[h]


def _rwkv_scan(nb, t, r, w, k, kk, ka, v):
    tt = 256
    nt = t // tt
    row = pl.BlockSpec((tt, R_WIDTH), lambda b, i: (b * nt + i, 0))
    return _call(
        _rwkv_scan_body, (nb, nt), [row] * 6,
        [row, pl.BlockSpec((None, R_HEADS, R_DH, R_DH), lambda b, i: (b, 0, 0, 0))],
        [jax.ShapeDtypeStruct((nb * t, R_WIDTH), F32), jax.ShapeDtypeStruct((nb, R_HEADS, R_DH, R_DH), F32)],
        scratch=[pltpu.VMEM((R_HEADS, tt, R_DH), F32)] * 8
        + [pltpu.VMEM((tt // RWKV_CHUNK, R_HEADS, R_DH + RWKV_CHUNK, R_DH), F32),
           pltpu.VMEM((tt // RWKV_CHUNK, R_HEADS, R_DH + RWKV_CHUNK, RWKV_CHUNK), F32)],
        name="rwkv_scan")(r, w, k, kk, ka, v)


RWKV_SB = 8
RWKV_HG = 4


def _rwkv_step_body(r_ref, w_ref, k_ref, kk_ref, ka_ref, vt_ref, s_ref, ot_ref, so_ref):
    n = R_DH
    lane = lax.broadcasted_iota(I32, (n, RWKV_SB), 1)
    samples = range(RWKV_SB)
    for h0 in range(0, R_HEADS, RWKV_HG):
        heads = range(h0, h0 + RWKV_HG)
        rem = {(h, s): jnp.sum(s_ref[s, h] * kk_ref[h, s:s + 1, :], axis=1, keepdims=True) for h in heads for s in samples}
        outs = {}
        for h in heads:
            for s in samples:
                st = (s_ref[s, h] * w_ref[h, s:s + 1, :] - rem[h, s] * ka_ref[h, s:s + 1, :]
                      + vt_ref[h * n:(h + 1) * n, s:s + 1] * k_ref[h, s:s + 1, :])
                so_ref[s, h] = st
                outs[h, s] = jnp.sum(st * r_ref[h, s:s + 1, :], axis=1, keepdims=True)
        for h in heads:
            ot = jnp.zeros((n, RWKV_SB), F32)
            for s in samples:
                ot = jnp.where(lane == s, outs[h, s], ot)
            ot_ref[h * n:(h + 1) * n, :] = ot


def _rwkv_step(r, w, k, kk, ka, v, state, layer):
    nb = r.shape[0]
    nblk = nb // RWKV_SB
    vt = jnp.transpose(v.reshape(nblk, RWKV_SB, R_WIDTH), (0, 2, 1))
    r, w, k, kk, ka = (jnp.transpose(a.reshape(nblk, RWKV_SB, R_HEADS, R_DH), (0, 2, 1, 3)) for a in (r, w, k, kk, ka))
    row = pl.BlockSpec((None, R_HEADS, RWKV_SB, R_DH), lambda i: (i, 0, 0, 0))
    col = pl.BlockSpec((None, R_WIDTH, RWKV_SB), lambda i: (i, 0, 0))
    sts = pl.BlockSpec((RWKV_SB, R_HEADS, R_DH, R_DH), lambda i: (i, 0, 0, 0))
    sti = pl.BlockSpec((None, RWKV_SB, R_HEADS, R_DH, R_DH), lambda i: (layer, i, 0, 0, 0))
    ot, so = _call(
        _rwkv_step_body, (nblk,), [row] * 5 + [col, sti], [col, sts],
        [jax.ShapeDtypeStruct((nblk, R_WIDTH, RWKV_SB), F32), jax.ShapeDtypeStruct(state.shape[1:], F32)],
        name="rwkv_step")(r, w, k, kk, ka, vt, state)
    return jnp.transpose(ot, (0, 2, 1)).reshape(nb, R_WIDTH), so


def _rwkv_post_body(o_ref, g_ref, bonus_ref, lng_ref, lnb_ref, ha_ref, out_ref):
    o = o_ref[...]
    ha = ha_ref[...]
    cen = o - _dot(o, ha, HI)
    var = _dot(cen * cen, ha, HI)
    y = cen * lax.rsqrt(var + GN_EPS) * lng_ref[...] + lnb_ref[...]
    out_ref[...] = ((y + bonus_ref[...]) * g_ref[...]).astype(BF)


def _rwkv_post(n, tm, o, g, bonus, ln_g, ln_b, havg):
    row = pl.BlockSpec((tm, R_WIDTH), lambda i: (i, 0))
    vec = _full((1, R_WIDTH))
    return _call(_rwkv_post_body, (n // tm,), [row, row, row, vec, vec, _full(havg.shape)], row,
                 jax.ShapeDtypeStruct((n, R_WIDTH), BF), name="rwkv_post")(o, g, bonus, ln_g, ln_b, havg)


def _block_diag(blocks):
    g, a, b = blocks.shape
    return jnp.einsum('gab,gh->gahb', blocks, jnp.eye(g, dtype=blocks.dtype)).reshape(g * a, g * b)


class _LayerParams(NamedTuple):
    norm_g: jax.Array
    ffn: tuple
    w_in: tuple
    w_out: jax.Array
    q_gain: jax.Array
    k_gain: jax.Array
    cmp: tuple
    ssm: tuple
    rwkv_mu: jax.Array
    rwkv: tuple
    rwkv_ln: tuple


def _layer_params(l, p):
    w_in_r = (p['w_in_bf16'], l)
    ffn = p['ffn_bf16'] + (l,)
    cmp = (p['nsa_cmp_pe'][l], p['nsa_cmp_w1'][l].astype(BF), p['nsa_cmp_w2'][l].astype(BF), p['nsa_k_gain'][l, 0:1])
    lre, lim, bbre, bbim = _ssm_disc(p['ssm_a_re'][l], p['ssm_a_im'][l], p['ssm_log_dt'][l], p['ssm_b_re'][l], p['ssm_b_im'][l])
    to_gcp = lambda m: jnp.transpose(m.reshape(S_CH, S_GROUPS, S_STATE), (1, 0, 2))
    ssm = (lre, lim, _block_diag(to_gcp(bbre)).astype(BF), _block_diag(to_gcp(bbim)).astype(BF),
           _block_diag(jnp.transpose(p['ssm_c_re'][l], (0, 2, 1))).astype(BF),
           _block_diag(jnp.transpose(p['ssm_c_im'][l], (0, 2, 1))).astype(BF),
           p['ssm_d'][l].reshape(1, S_WIDTH), p['ssm_w_glu'][l].astype(BF), p['ssm_b_glu'][l].reshape(1, S_WIDTH))
    zl = jnp.zeros((R_DECAY_LORA, R_WIDTH), F32)
    vec = lambda a: a.reshape(1, R_WIDTH)
    hsum = _block_diag(jnp.ones((R_HEADS, R_DH, R_DH), F32))
    rwkv = (vec(p['rwkv_w0'][l]), jnp.concatenate([p['rwkv_w2'][l], zl], axis=0).astype(BF),
            vec(p['rwkv_a0'][l]), jnp.concatenate([zl, p['rwkv_a2'][l]], axis=0).astype(BF), p['rwkv_g2'][l].astype(BF),
            vec(p['rwkv_k_k'][l]), vec(p['rwkv_k_a'][l]), vec(p['rwkv_r_k'][l]), hsum)
    return _LayerParams(p['norm_g'][l], ffn, w_in_r, p['w_out'][l].astype(BF), p['nsa_q_gain'][l], p['nsa_k_gain'][l], cmp,
                        ssm, p['rwkv_mu'][l].reshape(1, R_IN), rwkv,
                        (vec(p['rwkv_ln_g'][l]), vec(p['rwkv_ln_b'][l]), hsum / R_DH))


def _layer_prompt(rows, nb, t, x, lp, tabs):
    n = rows.n
    h = _ffn(rows, x, lp.norm_g[0:1], 0, *lp.ffn, 0)
    z = _projin(rows, h, lp.norm_g[1:2], 3, *lp.w_in)
    qn, qr, slc_rows, win_rows, gates = _nsa_prep(n, 256, z, tabs, t // 256, lp.q_gain, lp.k_gain)
    ncb = t // CMP_BLOCK
    cmp_rows = z[:, ZCMP:ZSLC]
    xcmp = jnp.transpose(cmp_rows.reshape(nb, ncb, CMP_BLOCK, 2, A_KV, A_DH), (3, 0, 4, 1, 2, 5))
    ckv = _compress(xcmp.reshape(2, nb * A_KV * ncb, CMP_BLOCK * A_DH), *lp.cmp).reshape(2, nb, A_KV, ncb, A_DH)
    o_cmp, sel = _cmp_attn(nb, t, qn, ckv[0], ckv[1])
    o_slc, o_win = _slc_win_attn(nb, t, qr, slc_rows, win_rows, sel)
    o_nsa = _nsa_combine(n, 256, gates, o_cmp, o_slc, o_win)
    o_ssm, ssm_st = _ssm_prompt(nb, t, z, lp.ssm)
    r, w, k, kk, ka, v, g, bonus = _rwkv_prep(n, 256, z, None, lp.rwkv_mu, lp.rwkv, seq=t)
    o_scan, wkv_t = _rwkv_scan(nb, t, r, w, k, kk, ka, v)
    o_rwkv = _rwkv_post(n, 256, o_scan, g, bonus, *lp.rwkv_ln)
    h = _projout(rows, h, o_nsa, o_ssm, o_rwkv, lp.w_out)
    y = _ffn(rows, h, lp.norm_g[2:3], 6, *lp.ffn, 1)
    keep = min(WINDOW, t)
    shape6 = lambda a: a.reshape(nb, -1, 2, A_KV, A_DH)
    state = (shape6(cmp_rows), shape6(slc_rows), shape6(win_rows)[:, t - keep:],
             jnp.stack([ssm_st[:, 0], ssm_st[:, 1]], axis=-1).reshape(nb, S_GROUPS, S_STATE, 2),
             z.reshape(nb, t, ZW)[:, -1, ZRW:ZGATE], jnp.swapaxes(wkv_t, -1, -2))
    return y, state


def _layer_sample(rows, layer, x, lp, tabs, cache_cmp, cache_slc, cache_win, win_acc, page_table, ssm0, shift0, wkv_all):
    n = rows.n
    h = _ffn(rows, x, lp.norm_g[0:1], 0, *lp.ffn, 0)
    z = _projin(rows, h, lp.norm_g[1:2], 3, *lp.w_in)
    qn, qr, slc_new, win_new, gates = _nsa_prep(n, n, z, tabs, 1, lp.q_gain, lp.k_gain)
    cmp_new = z[:, ZCMP:ZSLC]
    ckv_past = _compress_past(layer, cache_cmp, page_table, *lp.cmp)
    xnew = jnp.transpose(cmp_new.reshape(n, 2, A_KV, A_DH), (1, 0, 2, 3)).reshape(2, n * A_KV, A_DH)
    xnew = jnp.pad(xnew, ((0, 0), (0, 0), (0, (CMP_BLOCK - 1) * A_DH)))
    ckv_new = _compress(xnew, *lp.cmp).reshape(2, n, A_KV, 1, A_DH)
    ckv_new = jnp.pad(jnp.transpose(ckv_new, (1, 0, 2, 3, 4)), ((0, 0), (0, 0), (0, 0), (0, 7), (0, 0)))
    q3n = qn.reshape(n, A_HEADS, A_DH)
    q3r = qr.reshape(n, A_HEADS, A_DH)
    o_cmp, idx = _cmp_attn_sample(q3n, ckv_past, ckv_new)
    idx_flat = idx[:, :A_KV, :N_SEL].reshape(-1)
    o_slc = _slc_attn_sample(layer, q3r, cache_slc, idx_flat, page_table, slc_new.reshape(n, KV_ROWS, A_DH))
    o_win, win_acc = _win_attn_sample(layer, q3r, cache_win, win_new.reshape(n, KV_ROWS, A_DH), win_acc)
    flat = lambda a: a.reshape(n, A_WIDTH)
    o_nsa = _nsa_combine(n, n, gates, flat(o_cmp), flat(o_slc), flat(o_win))
    o_ssm, hr, hi = _ssm_sample(z, ssm0[..., 0].reshape(n, S_FLAT), ssm0[..., 1].reshape(n, S_FLAT), lp.ssm)
    r, w, k, kk, ka, v, g, bonus = _rwkv_prep(n, n, z, shift0, lp.rwkv_mu, lp.rwkv)
    o_step, wkv = _rwkv_step(r, w, k, kk, ka, v, wkv_all, layer)
    o_rwkv = _rwkv_post(n, n, o_step, g, bonus, *lp.rwkv_ln)
    h = _projout(rows, h, o_nsa, o_ssm, o_rwkv, lp.w_out)
    y = _ffn(rows, h, lp.norm_g[2:3], 6, *lp.ffn, 1)
    shape6 = lambda a: a.reshape(n, 1, 2, A_KV, A_DH)
    state = (shape6(cmp_new), shape6(slc_new), None,
             jnp.stack([hr, hi], axis=-1).reshape(n, S_GROUPS, S_STATE, 2), z[:, ZRW:ZGATE], wkv)
    return y, state, win_acc


def kernel(x_prompt, x_sample, cache_nsa_cmp, cache_nsa_slc, cache_nsa_win, state_ssm, state_rwkv_shift, state_rwkv_wkv, page_table, c_prompt, c_sample, norm_g, w_ada, b_ada, ffn_w1, ffn_w3, ffn_w2, w_in, w_out, nsa_q_gain, nsa_k_gain, nsa_cmp_pe, nsa_cmp_w1, nsa_cmp_w2, ssm_a_re, ssm_a_im, ssm_log_dt, ssm_b_re, ssm_b_im, ssm_c_re, ssm_c_im, ssm_d, ssm_w_glu, ssm_b_glu, rwkv_mu, rwkv_w0, rwkv_w2, rwkv_a0, rwkv_a2, rwkv_g2, rwkv_k_k, rwkv_k_a, rwkv_r_k, rwkv_ln_g, rwkv_ln_b):
    p = dict(norm_g=norm_g, ffn_w1=ffn_w1, ffn_w3=ffn_w3, ffn_w2=ffn_w2, w_in=w_in, w_out=w_out, nsa_q_gain=nsa_q_gain,
             nsa_k_gain=nsa_k_gain, nsa_cmp_pe=nsa_cmp_pe, nsa_cmp_w1=nsa_cmp_w1, nsa_cmp_w2=nsa_cmp_w2, ssm_a_re=ssm_a_re,
             ssm_a_im=ssm_a_im, ssm_log_dt=ssm_log_dt, ssm_b_re=ssm_b_re, ssm_b_im=ssm_b_im, ssm_c_re=ssm_c_re,
             ssm_c_im=ssm_c_im, ssm_d=ssm_d, ssm_w_glu=ssm_w_glu, ssm_b_glu=ssm_b_glu, rwkv_mu=rwkv_mu, rwkv_w0=rwkv_w0,
             rwkv_w2=rwkv_w2, rwkv_a0=rwkv_a0, rwkv_a2=rwkv_a2, rwkv_g2=rwkv_g2, rwkv_k_k=rwkv_k_k, rwkv_k_a=rwkv_k_a,
             rwkv_r_k=rwkv_r_k, rwkv_ln_g=rwkv_ln_g, rwkv_ln_b=rwkv_ln_b)
    p['ffn_bf16'] = (ffn_w1.astype(BF), ffn_w3.astype(BF), ffn_w2.astype(BF))
    p['w_in_bf16'] = jnp.concatenate(
        [w_in[:, :, :OFF_GATE], w_in[:, :, OFF_SSM:], w_in[:, :, OFF_GATE:OFF_SSM],
         jnp.zeros(w_in.shape[:2] + (ZW - ZGATE - 3 * A_HEADS,), F32)], axis=2).astype(BF)
    depth = w_in.shape[0]
    nbp, t, d = x_prompt.shape
    nbs, ts, _ = x_sample.shape
    assert ts == 1 and nbp <= 8 and d == D_MODEL
    cp8 = jnp.pad(c_prompt, ((0, 8 - nbp), (0, 0)))
    mod_p, mod_s = _ada(cp8, c_sample, w_ada, b_ada)
    tabs_p = _rope_tables(jnp.arange(t))
    tabs_s = _rope_tables(jnp.full((nbs,), PAST_LEN))
    hp = x_prompt.reshape(nbp * t, d)
    hs = x_sample.reshape(nbs, d)
    st_p, st_s = [], []
    cmp_flat = cache_nsa_cmp.reshape(cache_nsa_cmp.shape[:2] + (PAGE_SIZE * KV_ROWS, A_DH))
    slc_flat = cache_nsa_slc.reshape(cache_nsa_slc.shape[:2] + (PAGE_SIZE * KV_ROWS, A_DH))
    win_flat = cache_nsa_win.reshape(cache_nsa_win.shape[:2] + (cache_nsa_win.shape[2] * KV_ROWS, A_DH))
    win_acc = None
    for l in range(depth):
        lp = _layer_params(l, p)
        rows_p = _Rows(nbp * t, t, mod_p[l].reshape(8, 1, 9 * d), False, 512)
        rows_s = _Rows(nbs, 1, mod_s[l], True, nbs)
        hp, sp = _layer_prompt(rows_p, nbp, t, hp, lp, tabs_p)
        hs, ss, win_acc = _layer_sample(rows_s, l, hs, lp, tabs_s, cmp_flat, slc_flat, win_flat, win_acc, page_table,
                                        state_ssm[l], state_rwkv_shift[l], state_rwkv_wkv)
        st_p.append(sp)
        st_s.append(ss)
    outs = [hp.reshape(nbp, t, d), hs.reshape(nbs, 1, d)]
    for i in range(6):
        outs.append(jnp.stack([s[i] for s in st_p]))
        outs.append(win_acc.reshape(cache_nsa_win.shape) if i == 2 else jnp.stack([s[i] for s in st_s]))
    return tuple(outs)
```

```python
import functools
import math
from typing import NamedTuple

import jax
import jax.numpy as jnp
from jax import lax
from jax.experimental import pallas as pl
from jax.experimental.pallas import tpu as pltpu

F32 = jnp.float32
BF = jnp.bfloat16
I32 = jnp.int32
HI = lax.Precision.HIGHEST

D_MODEL = 2048
PAST_LEN = 2048
PAGE_SIZE = 128
A_HEADS, A_KV, A_HPG, A_DH = 8, 2, 4, 128
A_WIDTH = A_HEADS * A_DH
A_KVW = A_KV * A_DH
CMP_BLOCK, SEL_BLOCK, N_SEL, WINDOW = 32, 64, 16, 512
ROT_DIM = A_DH // 4
ROPE_THETA = 500000.0
ATT_SCALE = A_DH ** -0.5
FORCE_BONUS = 1.0e4
S_GROUPS, S_CH, S_STATE = 32, 16, 64
S_WIDTH = S_GROUPS * S_CH
S_FLAT = S_GROUPS * S_STATE
R_HEADS, R_DH = 8, 64
R_WIDTH = R_HEADS * R_DH
R_DECAY_LORA, R_A_LORA, R_GATE_LORA = 64, 64, 128
R_IN = 3 * R_WIDTH + R_DECAY_LORA + R_A_LORA + R_GATE_LORA
GN_EPS = 64e-5
OFF_KV = A_WIDTH
OFF_GATE = OFF_KV + 6 * A_KVW
OFF_SSM = OFF_GATE + 3 * A_HEADS
OFF_RWKV = OFF_SSM + S_WIDTH

ZQ, ZCMP, ZSLC, ZWIN, ZSSM, ZRW = 0, 1024, 1536, 2048, 2560, 3072
ZGATE = ZRW + R_IN
ZW = ZGATE + 128
LANE = 128
RWKV_CHUNK = 16


def _dot(a, b, precision=None):
    return jnp.dot(a, b, preferred_element_type=F32, precision=precision)


def _dot_nt(a, b, precision=None):
    return lax.dot_general(a, b, (((1,), (1,)), ((), ())), preferred_element_type=F32, precision=precision)


def _dot_tn(a, b, precision=None):
    return lax.dot_general(a, b, (((0,), (0,)), ((), ())), preferred_element_type=F32, precision=precision)


def _call(body, grid, in_specs, out_specs, out_shape, scratch=(), nsp=0, name=None, aliases=None):
    gs = pltpu.PrefetchScalarGridSpec(num_scalar_prefetch=nsp, grid=grid, in_specs=in_specs, out_specs=out_specs,
                                      scratch_shapes=list(scratch))
    return pl.pallas_call(body, grid_spec=gs, out_shape=out_shape, name=name, input_output_aliases=aliases or {},
                          compiler_params=pltpu.CompilerParams(dimension_semantics=("arbitrary",) * len(grid)))


def _rms(x, g):
    return x * lax.rsqrt(jnp.mean(x * x, axis=-1, keepdims=True) + 1e-6) * g


def _finite_or_zero(m):
    return jnp.where(jnp.abs(m) < jnp.inf, m, 0.0)


def _masked_softmax(s, mask, axis):
    s = jnp.where(mask, s, -jnp.inf)
    m = _finite_or_zero(jnp.max(s, axis=axis, keepdims=True))
    e = jnp.where(mask, jnp.exp(s - m), 0.0)
    return e / jnp.maximum(jnp.sum(e, axis=axis, keepdims=True), 1e-30)


class _Rows(NamedTuple):
    n: int
    seq: int
    mod: jax.Array
    per_row: bool
    tm: int


def _mod_spec(rows, tm, col, width=D_MODEL, jdep=False):
    per = D_MODEL // width
    if rows.per_row:
        return pl.BlockSpec((tm, width), lambda i, j: (i, col * per + (j if jdep else 0)))
    tpb = rows.seq // tm
    return pl.BlockSpec((None, 1, width), lambda i, j: (i // tpb, 0, col * per + (j if jdep else 0)))


def _zspec(tm, width, off):
    assert off % width == 0
    return pl.BlockSpec((tm, width), lambda i, j: (i, off // width))


def _full(shape):
    nd = len(shape)
    return pl.BlockSpec(shape, lambda *a: (0,) * nd)


def _ada_body(cp_ref, cs_ref, w_ref, b_ref, op_ref, os_ref):
    w = w_ref[...].astype(BF)
    b = b_ref[...]
    for c_ref, o_ref in ((cp_ref, op_ref), (cs_ref, os_ref)):
        c = c_ref[...]
        o_ref[...] = _dot((c * jax.nn.sigmoid(c)).astype(BF), w) + b


def _ada(cp8, cs, w_ada, b_ada):
    nl, d, n = w_ada.shape
    tn = 1024
    ns = cs.shape[0]
    return _call(
        _ada_body, (nl, n // tn),
        [_full((8, d)), _full((ns, d)),
         pl.BlockSpec((None, d, tn), lambda l, j: (l, 0, j)),
         pl.BlockSpec((None, 1, tn), lambda l, j: (l, 0, j))],
        [pl.BlockSpec((None, 8, tn), lambda l, j: (l, 0, j)),
         pl.BlockSpec((None, ns, tn), lambda l, j: (l, 0, j))],
        [jax.ShapeDtypeStruct((nl, 8, n), F32), jax.ShapeDtypeStruct((nl, ns, n), F32)],
        name="ada")(cp8, cs, w_ada, b_ada.reshape(nl, 1, n))


def _norm_mod(x, g, scale, shift):
    return _rms(x, g) * (1.0 + scale) + shift


def _ffn_body(x_ref, g_ref, sh_ref, sc_ref, gt_ref, w1_ref, w3_ref, w2_ref, o_ref, xn_ref):
    f = pl.program_id(1)

    @pl.when(f == 0)
    def _():
        xn_ref[...] = _norm_mod(x_ref[...], g_ref[...], sc_ref[...], sh_ref[...]).astype(BF)
        o_ref[...] = jnp.zeros_like(o_ref)

    xn = xn_ref[...]
    h1 = _dot(xn, w1_ref[...])
    h3 = _dot(xn, w3_ref[...])
    o_ref[...] += _dot((h1 * jax.nn.sigmoid(h1) * h3).astype(BF), w2_ref[...])

    @pl.when(f == pl.num_programs(1) - 1)
    def _():
        o_ref[...] = x_ref[...] + 0.5 * gt_ref[...] * o_ref[...]


def _ffn(rows, x, g, col0, w1, w3, w2, layer, which):
    d, ff = w1.shape[2:]
    tm = rows.tm
    tf = 512 if tm <= 512 else 256
    return _call(
        _ffn_body, (rows.n // tm, ff // tf),
        [pl.BlockSpec((tm, d), lambda i, f: (i, 0)), _full((1, d)),
         _mod_spec(rows, tm, col0), _mod_spec(rows, tm, col0 + 1), _mod_spec(rows, tm, col0 + 2),
         pl.BlockSpec((None, None, d, tf), lambda i, f: (layer, which, 0, f)),
         pl.BlockSpec((None, None, d, tf), lambda i, f: (layer, which, 0, f)),
         pl.BlockSpec((None, None, tf, d), lambda i, f: (layer, which, f, 0))],
        pl.BlockSpec((tm, d), lambda i, f: (i, 0)),
        jax.ShapeDtypeStruct((rows.n, d), F32),
        scratch=[pltpu.VMEM((tm, d), BF)], name="ffn")(x, g, rows.mod, rows.mod, rows.mod, w1, w3, w2)


def _projin_body(x_ref, g_ref, sh_ref, sc_ref, w_ref, o_ref, xn_ref):
    @pl.when(pl.program_id(1) == 0)
    def _():
        xn_ref[...] = _norm_mod(x_ref[...], g_ref[...], sc_ref[...], sh_ref[...]).astype(BF)

    o_ref[...] = _dot(xn_ref[...], w_ref[...])


def _projin(rows, x, g, col0, w, layer):
    d, n = w.shape[1:]
    tm, tn = rows.tm, 1664
    return _call(
        _projin_body, (rows.n // tm, n // tn),
        [pl.BlockSpec((tm, d), lambda i, j: (i, 0)), _full((1, d)),
         _mod_spec(rows, tm, col0), _mod_spec(rows, tm, col0 + 1),
         pl.BlockSpec((None, d, tn), lambda i, j: (layer, 0, j))],
        pl.BlockSpec((tm, tn), lambda i, j: (i, j)),
        jax.ShapeDtypeStruct((rows.n, n), F32),
        scratch=[pltpu.VMEM((tm, d), BF)], name="proj_in")(x, g, rows.mod, rows.mod, w)


def _projout_body(h_ref, gt_ref, a_ref, b_ref, c_ref, wa_ref, wb_ref, wc_ref, o_ref):
    mix = _dot(a_ref[...], wa_ref[...]) + _dot(b_ref[...], wb_ref[...]) + _dot(c_ref[...], wc_ref[...])
    o_ref[...] = h_ref[...] + gt_ref[...] * mix


def _projout(rows, h, o_nsa, o_ssm, o_rwkv, w):
    d = h.shape[1]
    tm, tn = rows.tm, 512
    nblk = A_WIDTH // S_WIDTH
    return _call(
        _projout_body, (rows.n // tm, d // tn),
        [pl.BlockSpec((tm, tn), lambda i, j: (i, j)), _mod_spec(rows, tm, 5, tn, True),
         pl.BlockSpec((tm, A_WIDTH), lambda i, j: (i, 0)), pl.BlockSpec((tm, S_WIDTH), lambda i, j: (i, 0)),
         pl.BlockSpec((tm, R_WIDTH), lambda i, j: (i, 0)),
         pl.BlockSpec((A_WIDTH, tn), lambda i, j: (0, j)), pl.BlockSpec((S_WIDTH, tn), lambda i, j: (nblk, j)),
         pl.BlockSpec((R_WIDTH, tn), lambda i, j: (nblk + 1, j))],
        pl.BlockSpec((tm, tn), lambda i, j: (i, j)),
        jax.ShapeDtypeStruct((rows.n, d), F32), name="proj_out")(h, rows.mod, o_nsa, o_ssm, o_rwkv, w, w, w)


def _rope_tables(pos):
    half = ROT_DIM // 2
    inv = ROPE_THETA ** (-2.0 * jnp.arange(half, dtype=F32) / ROT_DIM)
    ang = pos.astype(F32)[:, None] * inv[None, :]
    cos, sin = jnp.cos(ang), jnp.sin(ang)
    n = pos.shape[0]
    ct = jnp.concatenate([cos, cos, jnp.ones((n, A_DH - ROT_DIM), F32)], axis=1)
    sa = jnp.concatenate([-sin, jnp.zeros((n, A_DH - half), F32)], axis=1)
    sb = jnp.concatenate([jnp.zeros((n, half), F32), sin, jnp.zeros((n, A_DH - ROT_DIM), F32)], axis=1)
    return ct, sa, sb


def _prep_body(q_ref, ks_ref, kw_ref, gt_ref, ct_ref, sa_ref, sb_ref, qg_ref, kg1_ref, kg2_ref,
               qn_ref, qr_ref, slc_ref, win_ref, go_ref):
    ct, sa, sb = ct_ref[...], sa_ref[...], sb_ref[...]
    half = ROT_DIM // 2

    def rope(x):
        return x * ct + pltpu.roll(x, A_DH - half, 1) * sa + pltpu.roll(x, half, 1) * sb

    for h in range(A_HEADS):
        sl = slice(h * A_DH, (h + 1) * A_DH)
        x = _rms(q_ref[:, sl], qg_ref[...])
        qn_ref[:, sl] = (x * ATT_SCALE).astype(BF)
        qr_ref[:, sl] = (rope(x) * ATT_SCALE).astype(BF)
    for src, dst, kg in ((ks_ref, slc_ref, kg1_ref), (kw_ref, win_ref, kg2_ref)):
        for g in range(A_KV):
            sl = slice(g * A_DH, (g + 1) * A_DH)
            dst[:, sl] = rope(_rms(src[:, sl], kg[...]))
        dst[:, A_KVW:] = src[:, A_KVW:]
    go_ref[...] = jax.nn.sigmoid(gt_ref[...])


def _nsa_prep(n, tm, z, tabs, tab_blocks, q_gain, k_gain):
    tspec = pl.BlockSpec((tm, LANE), lambda i, j: (i % tab_blocks, 0))
    vspec = _full((1, A_DH))
    return _call(
        _prep_body, (n // tm, 1),
        [_zspec(tm, A_WIDTH, ZQ), _zspec(tm, 2 * A_KVW, ZSLC), _zspec(tm, 2 * A_KVW, ZWIN), _zspec(tm, LANE, ZGATE),
         tspec, tspec, tspec, vspec, vspec, vspec],
        [pl.BlockSpec((tm, A_WIDTH), lambda i, j: (i, 0)), pl.BlockSpec((tm, A_WIDTH), lambda i, j: (i, 0)),
         pl.BlockSpec((tm, 2 * A_KVW), lambda i, j: (i, 0)), pl.BlockSpec((tm, 2 * A_KVW), lambda i, j: (i, 0)),
         pl.BlockSpec((tm, LANE), lambda i, j: (i, 0))],
        [jax.ShapeDtypeStruct((n, A_WIDTH), BF), jax.ShapeDtypeStruct((n, A_WIDTH), BF),
         jax.ShapeDtypeStruct((n, 2 * A_KVW), F32), jax.ShapeDtypeStruct((n, 2 * A_KVW), F32),
         jax.ShapeDtypeStruct((n, LANE), F32)],
        name="nsa_prep")(z, z, z, z, *tabs, q_gain.reshape(1, A_DH), k_gain[1:2], k_gain[2:3])


def _cmp_tail(acc, w2, kg, is_k):
    out = _dot(jax.nn.gelu(acc).astype(BF), w2)
    return jnp.where(is_k, _rms(out, kg), out)


def _compress_body(x_ref, pe_ref, w1_ref, w2_ref, kg_ref, o_ref):
    x = (x_ref[...] + pe_ref[...]).astype(BF)
    o_ref[...] = _cmp_tail(_dot(x, w1_ref[...]), w2_ref[...], kg_ref[...], pl.program_id(0) == 0)


def _compress(x, pe, w1, w2, kg):
    _, m, f = x.shape
    tm = min(m, 256)
    return _call(
        _compress_body, (2, m // tm),
        [pl.BlockSpec((None, tm, f), lambda s, i: (s, i, 0)), pl.BlockSpec((None, 1, f), lambda s, i: (s, 0, 0)),
         pl.BlockSpec((None, f, A_DH), lambda s, i: (s, 0, 0)), pl.BlockSpec((None, A_DH, A_DH), lambda s, i: (s, 0, 0)),
         _full((1, A_DH))],
        pl.BlockSpec((None, tm, A_DH), lambda s, i: (s, i, 0)),
        jax.ShapeDtypeStruct((2, m, A_DH), F32), name="compress")(x, pe.reshape(2, 1, f), w1, w2, kg)


PAST_NB = 2


SUB = 8
CMP_RPT = SUB // (2 * A_KV)
CMP_QUAD = 2 * CMP_RPT


def _cmp_past_body(pt_ref, *refs):
    npg = PAST_LEN // PAGE_SIZE
    pages = refs[:PAST_NB * npg]
    pe_ref, w1_ref, w2_ref, kg_ref, o_ref = refs[PAST_NB * npg:]
    kvr = 2 * A_KV
    bpp = PAGE_SIZE // CMP_BLOCK
    nblk = PAST_NB * (PAST_LEN // CMP_BLOCK)
    rows = nblk * SUB
    acc = jnp.zeros((rows, CMP_QUAD * A_DH), F32)
    for q in range(CMP_BLOCK // CMP_QUAD):
        halves = []
        for half in range(2):
            off = (q * 2 + half) * SUB
            tiles = jnp.stack([pg[c * CMP_BLOCK * kvr + off:c * CMP_BLOCK * kvr + off + SUB, :]
                               for pg in pages for c in range(bpp)], axis=0)
            halves.append((tiles + pe_ref[off:off + SUB, :][None]).reshape(rows, A_DH).astype(BF))
        acc = acc + _dot(jnp.concatenate(halves, axis=1), w1_ref[q])
    j = jnp.bitwise_and(lax.broadcasted_iota(I32, (rows, A_DH), 0), SUB - 1)
    is_v = jnp.bitwise_and(j, kvr - 1) >= A_KV
    blk = jnp.right_shift(j, int(math.log2(kvr))) * 2 + is_v.astype(I32)
    sel = jnp.zeros((rows, A_DH), F32)
    for b in range(CMP_QUAD):
        sel = jnp.where(blk == b, acc[:, b * A_DH:(b + 1) * A_DH], sel)
    hid = jax.nn.gelu(sel + pltpu.roll(sel, rows - kvr, 0)).astype(BF)
    out2 = _dot(hid, w2_ref[...])
    out = jnp.where(is_v, out2[:, A_DH:], _rms(out2[:, :A_DH], kg_ref[...]))
    o_ref[...] = out.reshape(nblk, SUB, A_DH)


def _compress_past(layer, cache, page_table, pe, w1, w2, kg):
    nb, npg = page_table.shape
    ncb = PAST_LEN // CMP_BLOCK
    kvr = 2 * A_KV
    nq = CMP_BLOCK // CMP_QUAD

    def page_spec(i, p):
        return pl.BlockSpec((None, None, PAGE_SIZE * kvr, A_DH), lambda b, pt: (layer, pt[(b * PAST_NB + i) * npg + p], 0, 0))

    pe_flat = jnp.broadcast_to(jnp.transpose(pe, (1, 0, 2))[:, :, None, :], (CMP_BLOCK, 2, A_KV, A_DH)).reshape(CMP_BLOCK * kvr, A_DH)
    w1q = jnp.transpose(w1.reshape(2, nq, 2, CMP_RPT, A_DH, A_DH), (1, 2, 4, 3, 0, 5)).reshape(nq, 2 * A_DH, CMP_QUAD * A_DH)
    w2c = jnp.concatenate([w2[0], w2[1]], axis=1)
    specs = [page_spec(i, p) for i in range(PAST_NB) for p in range(npg)]
    out = _call(
        _cmp_past_body, (nb // PAST_NB,),
        specs + [_full(pe_flat.shape), _full(w1q.shape), _full(w2c.shape), _full((1, A_DH))],
        pl.BlockSpec((PAST_NB * ncb, SUB, A_DH), lambda b, pt: (b, 0, 0)),
        jax.ShapeDtypeStruct((nb * ncb, SUB, A_DH), F32), nsp=1, name="compress_past")(
            page_table.reshape(-1), *([cache] * (PAST_NB * npg)), pe_flat, w1q, w2c, kg)
    return jnp.transpose(out[:, :kvr].reshape(nb, ncb, 2, A_KV, A_DH), (0, 2, 3, 1, 4))


def _select(score, nblk):
    j = lax.broadcasted_iota(I32, score.shape, 0)
    rank = jnp.zeros(score.shape, F32)
    for k in range(nblk):
        rk = score[k:k + 1, :]
        rank = rank + ((rk > score) | ((rk == score) & (k < j))).astype(F32)
    return rank


def _cmpattn_body(q_ref, ck_ref, cv_ref, o_ref, sel_ref):
    qi = pl.program_id(1)
    tq = q_ref.shape[0]
    nc = ck_ref.shape[1]
    ns = nc // 2
    base = qi * tq
    sel_rows = []
    for g in range(A_KV):
        q4 = jnp.concatenate([q_ref[:, (g * A_HPG + h) * A_DH:(g * A_HPG + h + 1) * A_DH] for h in range(A_HPG)], axis=0)
        ckg = ck_ref[g].astype(BF)
        cvg = cv_ref[g].astype(BF)
        pos = base + jnp.bitwise_and(lax.broadcasted_iota(I32, (A_HPG * tq, nc), 0), tq - 1)
        blk_end = (lax.broadcasted_iota(I32, (A_HPG * tq, nc), 1) + 1) * CMP_BLOCK - 1
        p = _masked_softmax(_dot_nt(q4, ckg), blk_end <= pos, -1)
        o = _dot(p.astype(BF), cvg)
        for h in range(A_HPG):
            o_ref[:, (g * A_HPG + h) * A_DH:(g * A_HPG + h + 1) * A_DH] = o[h * tq:(h + 1) * tq]
        ckp = jnp.concatenate([ck_ref[g, pl.ds(0, ns, stride=2), :], ck_ref[g, pl.ds(1, ns, stride=2), :]], axis=0).astype(BF)
        row = lax.broadcasted_iota(I32, (nc, A_HPG * tq), 0)
        blk = jnp.where(row < ns, 2 * row, 2 * (row - ns) + 1)
        post = base + jnp.bitwise_and(lax.broadcasted_iota(I32, (nc, A_HPG * tq), 1), tq - 1)
        pt = _masked_softmax(_dot_nt(ckp, q4), (blk + 1) * CMP_BLOCK - 1 <= post, 0)
        imp = pt[:, 0:tq]
        for h in range(1, A_HPG):
            imp = imp + pt[:, h * tq:(h + 1) * tq]
        imp = imp[:ns] + imp[ns:]
        j = lax.broadcasted_iota(I32, (ns, tq), 0)
        pos2 = base + lax.broadcasted_iota(I32, (ns, tq), 1)
        cur = jnp.right_shift(pos2, int(math.log2(SEL_BLOCK)))
        forced = (j == 0) | (j == cur) | (j == cur - 1)
        score = jnp.where(j * SEL_BLOCK <= pos2, imp + jnp.where(forced, FORCE_BONUS, 0.0), -1e9)
        sel_rows.append((_select(score, ns) < min(N_SEL, ns)).astype(F32))
    pad = jnp.zeros((LANE - A_KV * ns, tq), F32)
    sel_ref[...] = jnp.concatenate(sel_rows + [pad], axis=0).T


def _cmp_attn(nb, t, qn, ck, cv):
    tq = 256
    nc = ck.shape[2]
    cspec = pl.BlockSpec((None, A_KV, nc, A_DH), lambda b, i: (b, 0, 0, 0))
    return _call(
        _cmpattn_body, (nb, t // tq),
        [pl.BlockSpec((tq, A_WIDTH), lambda b, i: (b * (t // tq) + i, 0)), cspec, cspec],
        [pl.BlockSpec((tq, A_WIDTH), lambda b, i: (b * (t // tq) + i, 0)),
         pl.BlockSpec((tq, LANE), lambda b, i: (b * (t // tq) + i, 0))],
        [jax.ShapeDtypeStruct((nb * t, A_WIDTH), F32), jax.ShapeDtypeStruct((nb * t, LANE), F32)],
        name="cmp_attn")(qn, ck, cv)


SLC_SPAN = 512


def _slcwin_body(q_ref, sk_ref, wk_ref, sel_ref, os_ref, ow_ref):
    qi = pl.program_id(1)
    tq = q_ref.shape[0]
    t = sk_ref.shape[0]
    ns = t // SEL_BLOCK
    base = qi * tq
    wkeys = WINDOW + tq
    wstart = pl.multiple_of(jnp.maximum(base - WINDOW, 0), tq)
    selb = sel_ref[...].astype(BF)
    qpw = base + jnp.bitwise_and(lax.broadcasted_iota(I32, (A_HPG * tq, wkeys), 0), tq - 1)
    kpw = wstart + lax.broadcasted_iota(I32, (A_HPG * tq, wkeys), 1)
    wmask = (kpw <= qpw) & (kpw > qpw - WINDOW)

    def heads_of(g):
        return jnp.concatenate([q_ref[:, (g * A_HPG + h) * A_DH:(g * A_HPG + h + 1) * A_DH] for h in range(A_HPG)], axis=0)

    def put(ref, g, o):
        for h in range(A_HPG):
            ref[:, (g * A_HPG + h) * A_DH:(g * A_HPG + h + 1) * A_DH] = o[h * tq:(h + 1) * tq]

    for g in range(A_KV):
        ksl = slice(g * A_DH, (g + 1) * A_DH)
        vsl = slice(A_KVW + g * A_DH, A_KVW + (g + 1) * A_DH)
        pw = _masked_softmax(_dot_nt(heads_of(g), wk_ref[pl.ds(wstart, wkeys), ksl].astype(BF)), wmask, -1)
        put(ow_ref, g, _dot(pw.astype(BF), wk_ref[pl.ds(wstart, wkeys), vsl].astype(BF)))

    nspan = t // SLC_SPAN
    for c in range(nspan):
        @pl.when((base + tq - 1) // SLC_SPAN == c)
        def _(c=c):
            nk = (c + 1) * SLC_SPAN
            lrow = lax.broadcasted_iota(I32, (LANE, nk), 0)
            kblk = jnp.right_shift(lax.broadcasted_iota(I32, (LANE, nk), 1), int(math.log2(SEL_BLOCK)))
            qpos = base + jnp.bitwise_and(lax.broadcasted_iota(I32, (A_HPG * tq, nk), 0), tq - 1)
            causal = lax.broadcasted_iota(I32, (A_HPG * tq, nk), 1) <= qpos
            for g in range(A_KV):
                expand = (lrow == g * ns + kblk).astype(BF)
                picked = _dot(selb, expand)
                smask = (jnp.concatenate([picked] * A_HPG, axis=0) > 0.5) & causal
                p = _masked_softmax(_dot_nt(heads_of(g), sk_ref[0:nk, g * A_DH:(g + 1) * A_DH].astype(BF)), smask, -1)
                put(os_ref, g, _dot(p.astype(BF), sk_ref[0:nk, A_KVW + g * A_DH:A_KVW + (g + 1) * A_DH].astype(BF)))


def _slc_win_attn(nb, t, qr, slc_rows, win_rows, sel):
    tq = 128
    nq = t // tq
    rowspec = pl.BlockSpec((tq, A_WIDTH), lambda b, i: (b * nq + i, 0))
    kvspec = pl.BlockSpec((t, 2 * A_KVW), lambda b, i: (b, 0))
    return _call(
        _slcwin_body, (nb, nq),
        [rowspec, kvspec, kvspec, pl.BlockSpec((tq, LANE), lambda b, i: (b * nq + i, 0))],
        [rowspec, rowspec],
        [jax.ShapeDtypeStruct((nb * t, A_WIDTH), F32)] * 2, name="slc_win_attn")(qr, slc_rows, win_rows, sel)


def _combine_body(g_ref, oc_ref, os_ref, ow_ref, o_ref):
    gt = g_ref[...]
    for hd in range(A_HEADS):
        sl = slice(hd * A_DH, (hd + 1) * A_DH)
        acc = (gt[:, hd:hd + 1] * oc_ref[:, sl] + gt[:, A_HEADS + hd:A_HEADS + hd + 1] * os_ref[:, sl]
               + gt[:, 2 * A_HEADS + hd:2 * A_HEADS + hd + 1] * ow_ref[:, sl])
        o_ref[:, sl] = acc.astype(BF)


def _nsa_combine(n, tm, gates, o_cmp, o_slc, o_win):
    spec = pl.BlockSpec((tm, A_WIDTH), lambda i: (i, 0))
    return _call(_combine_body, (n // tm,), [pl.BlockSpec((tm, LANE), lambda i: (i, 0)), spec, spec, spec], spec,
                 jax.ShapeDtypeStruct((n, A_WIDTH), BF), name="nsa_combine")(gates, o_cmp, o_slc, o_win)


def _cmpattn_s_body(q_ref, cp_ref, cn_ref, pt_ref, o_ref, idx_ref, blk_ref):
    pos = PAST_LEN
    q = q_ref[...]
    ncp = cp_ref.shape[2]
    ns = ncp // 2
    nrow = ns + 8
    hrow = lax.broadcasted_iota(I32, (A_HEADS, A_DH), 0)
    o_all = jnp.zeros((A_HEADS, A_DH), F32)
    idx_ref[...] = jnp.zeros(idx_ref.shape, I32)
    blk_ref[...] = jnp.zeros(blk_ref.shape, I32)
    new_ok =(ncp + 1) * CMP_BLOCK - 1 <= pos
    for g in range(A_KV):
        ck = cp_ref[0, g].astype(BF)
        cv = cp_ref[1, g].astype(BF)
        ckn = cn_ref[0, g].astype(BF)
        cvn = cn_ref[1, g]
        vp = (lax.broadcasted_iota(I32, (A_HEADS, ncp), 1) + 1) * CMP_BLOCK - 1 <= pos
        vn = (lax.broadcasted_iota(I32, (A_HEADS, 8), 1) == 0) & new_ok
        sp = jnp.where(vp, _dot_nt(q, ck), -jnp.inf)
        sn = jnp.where(vn, _dot_nt(q, ckn), -jnp.inf)
        m = jnp.maximum(jnp.max(sp, axis=-1, keepdims=True), jnp.max(sn, axis=-1, keepdims=True))
        m = _finite_or_zero(m)
        ep = jnp.where(vp, jnp.exp(sp - m), 0.0)
        en = jnp.where(vn, jnp.exp(sn - m), 0.0)
        den = jnp.maximum(jnp.sum(ep, axis=-1, keepdims=True) + jnp.sum(en, axis=-1, keepdims=True), 1e-30)
        o = _dot((ep / den).astype(BF), cv) + (en / den)[:, 0:1] * cvn[0:1, :]
        o_all = jnp.where((hrow >= g * A_HPG) & (hrow < (g + 1) * A_HPG), o, o_all)
        cke = cp_ref[0, g, pl.ds(0, ns, stride=2), :].astype(BF)
        cko = cp_ref[0, g, pl.ds(1, ns, stride=2), :].astype(BF)
        rowe = lax.broadcasted_iota(I32, (ns, A_HEADS), 0)
        ve = (2 * rowe + 1) * CMP_BLOCK - 1 <= pos
        vo = (2 * rowe + 2) * CMP_BLOCK - 1 <= pos
        vnt = (lax.broadcasted_iota(I32, (8, A_HEADS), 0) == 0) & new_ok
        ste = jnp.where(ve, _dot_nt(cke, q), -jnp.inf)
        sto = jnp.where(vo, _dot_nt(cko, q), -jnp.inf)
        stn = jnp.where(vnt, _dot_nt(ckn, q), -jnp.inf)
        mt = jnp.maximum(jnp.maximum(jnp.max(ste, axis=0, keepdims=True), jnp.max(sto, axis=0, keepdims=True)),
                         jnp.max(stn, axis=0, keepdims=True))
        mt = _finite_or_zero(mt)
        ee = jnp.where(ve, jnp.exp(ste - mt), 0.0)
        eo = jnp.where(vo, jnp.exp(sto - mt), 0.0)
        et = jnp.where(vnt, jnp.exp(stn - mt), 0.0)
        dent = jnp.maximum(jnp.sum(ee, axis=0, keepdims=True) + jnp.sum(eo, axis=0, keepdims=True)
                           + jnp.sum(et, axis=0, keepdims=True), 1e-30)
        hlane = lax.broadcasted_iota(I32, (1, A_HEADS), 1)
        ing = (hlane >= g * A_HPG) & (hlane < (g + 1) * A_HPG)

        def imp_of(e):
            return jnp.sum(jnp.where(ing, e / dent, 0.0), axis=1, keepdims=True)

        imp = jnp.concatenate([imp_of(ee) + imp_of(eo), imp_of(et)], axis=0)
        imp = jnp.broadcast_to(imp, (nrow, LANE))
        j = lax.broadcasted_iota(I32, (nrow, LANE), 0)
        cur = pos // SEL_BLOCK
        nsel = ns + 1
        forced = (j == 0) | (j == cur) | (j == cur - 1)
        score = jnp.where(j * SEL_BLOCK <= pos, imp + jnp.where(forced, FORCE_BONUS, 0.0), -1e9)
        score = jnp.where(j < nsel, score, -3e9)
        rank = _select(score, nsel)
        slot = lax.broadcasted_iota(I32, (nrow, LANE), 1)
        hit = (rank == slot.astype(F32)) & (slot < min(N_SEL, nsel)) & (j < nsel)
        ids = jnp.sum(jnp.where(hit, j, 0), axis=0, keepdims=True)
        idx_ref[g:g + 1, :] = ids
        bpp = PAGE_SIZE // SEL_BLOCK
        pidx = jnp.minimum(ids, PAST_LEN // SEL_BLOCK - 1)
        page = jnp.right_shift(pidx, int(math.log2(bpp)))
        phys = jnp.zeros_like(ids)
        for p in range(pt_ref.shape[1]):
            phys = jnp.where(page == p, pt_ref[:, p:p + 1], phys)
        blk_ref[g:g + 1, :] = phys * bpp + jnp.bitwise_and(pidx, bpp - 1)
    o_ref[...] = o_all


def _cmp_attn_sample(q3, ckv_past, ckv_new, page_table):
    nb = q3.shape[0]
    ncp = ckv_past.shape[3]
    npg = page_table.shape[1]
    ids = pl.BlockSpec((None, 8, LANE), lambda b: (b, 0, 0))
    return _call(
        _cmpattn_s_body, (nb,),
        [pl.BlockSpec((None, A_HEADS, A_DH), lambda b: (b, 0, 0)),
         pl.BlockSpec((None, 2, A_KV, ncp, A_DH), lambda b: (b, 0, 0, 0, 0)),
         pl.BlockSpec((None, 2, A_KV, 8, A_DH), lambda b: (b, 0, 0, 0, 0)),
         pl.BlockSpec((None, 1, npg), lambda b: (b, 0, 0))],
        [pl.BlockSpec((None, A_HEADS, A_DH), lambda b: (b, 0, 0)), ids, ids],
        [jax.ShapeDtypeStruct((nb, A_HEADS, A_DH), F32), jax.ShapeDtypeStruct((nb, 8, LANE), I32),
         jax.ShapeDtypeStruct((nb, 8, LANE), I32)],
        name="cmp_attn_sample")(q3, ckv_past, ckv_new, page_table.reshape(nb, 1, npg))


KV_ROWS = 2 * A_KV


def _flat_attend(q, xb, kmask, sn, vn):
    s = jnp.where(kmask, _dot_nt(q, xb), -jnp.inf)
    m = _finite_or_zero(jnp.maximum(jnp.max(s, axis=-1, keepdims=True), sn))
    e = jnp.where(kmask, jnp.exp(s - m), 0.0)
    en = jnp.exp(sn - m)
    den = jnp.maximum(jnp.sum(e, axis=-1, keepdims=True) + en, 1e-30)
    return _dot(pltpu.roll(e / den, A_KV, 1).astype(BF), xb) + (en / den) * vn


def _slc_s_body(idx_ref, pt_ref, q_ref, *refs):
    blocks = refs[:A_KV * N_SEL]
    new_ref, o_ref = refs[A_KV * N_SEL:]
    b = pl.program_id(0)
    npast = PAST_LEN // SEL_BLOCK
    nfb = SEL_BLOCK * KV_ROWS
    nk = N_SEL * nfb
    q = q_ref[...]
    new = new_ref[...]
    col = lax.broadcasted_iota(I32, (A_HEADS, nk), 1)
    krow = jnp.bitwise_and(col, KV_ROWS - 1)
    slot = jnp.right_shift(col, int(math.log2(nfb)))
    hrow = lax.broadcasted_iota(I32, (A_HEADS, A_DH), 0)
    o_all = jnp.zeros((A_HEADS, A_DH), F32)
    for g in range(A_KV):
        xb = jnp.concatenate([blocks[g * N_SEL + n][...].astype(BF) for n in range(N_SEL)], axis=0)
        past = jnp.zeros((A_HEADS, nk), I32)
        nfresh = jnp.int32(0)
        for n in range(N_SEL):
            is_past = (idx_ref[(b * A_KV + g) * N_SEL + n] < npast).astype(I32)
            past = jnp.where(slot == n, is_past, past)
            nfresh = nfresh + (1 - is_past)
        fresh = jnp.full((A_HEADS, 1), nfresh, I32) > 0
        sn = jnp.where(fresh, jnp.sum(q.astype(F32) * new[g:g + 1, :], axis=-1, keepdims=True), -jnp.inf)
        o = _flat_attend(q, xb, (past > 0) & (krow == g), sn, new[A_KV + g:A_KV + g + 1, :])
        o_all = jnp.where((hrow >= g * A_HPG) & (hrow < (g + 1) * A_HPG), o, o_all)
    o_ref[...] = o_all


def _slc_attn_sample(layer, q3, cache, idx, blk, slc_new):
    nb = q3.shape[0]

    def blk_spec(g, n):
        return pl.BlockSpec((None, None, SEL_BLOCK * KV_ROWS, A_DH),
                            lambda b, idx_ref, blk_ref: (layer, blk_ref[(b * A_KV + g) * N_SEL + n], 0, 0))

    return _call(
        _slc_s_body, (nb,),
        [pl.BlockSpec((None, A_HEADS, A_DH), lambda b, i, p: (b, 0, 0))]
        + [blk_spec(g, n) for g in range(A_KV) for n in range(N_SEL)]
        + [pl.BlockSpec((None, KV_ROWS, A_DH), lambda b, i, p: (b, 0, 0))],
        pl.BlockSpec((None, A_HEADS, A_DH), lambda b, i, p: (b, 0, 0)),
        jax.ShapeDtypeStruct((nb, A_HEADS, A_DH), F32), nsp=2, name="slc_attn_sample")(
            idx, blk, q3, *([cache] * (A_KV * N_SEL)), slc_new)


WIN_SB = 4


def _win_s_body(q_ref, buf_ref, new_ref, *refs):
    o_ref, nb_ref = refs[-2:]
    pos = PAST_LEN
    nf = buf_ref.shape[1]
    wb = nf // KV_ROWS
    col = lax.broadcasted_iota(I32, (A_HEADS, nf), 1)
    hrow = lax.broadcasted_iota(I32, (A_HEADS, nf), 0)
    kpos = PAST_LEN - wb + jnp.right_shift(col, int(math.log2(KV_ROWS)))
    kmask = ((jnp.bitwise_and(col, KV_ROWS - 1) == jnp.right_shift(hrow, int(math.log2(A_HPG))))
             & (kpos <= pos) & (kpos > pos - WINDOW))
    h1 = lax.broadcasted_iota(I32, (A_HEADS, 1), 0)
    for i in range(WIN_SB):
        q = q_ref[i]
        new = new_ref[i]
        qf = q.astype(F32)
        sn = jnp.zeros((A_HEADS, 1), F32)
        vn = jnp.zeros((A_HEADS, A_DH), F32)
        for g in range(A_KV):
            ing = (h1 >= g * A_HPG) & (h1 < (g + 1) * A_HPG)
            sn = jnp.where(ing, jnp.sum(qf * new[g:g + 1, :], axis=-1, keepdims=True), sn)
            vn = jnp.where(ing, new[A_KV + g:A_KV + g + 1, :], vn)
        o_ref[i] = _flat_attend(q, buf_ref[i].astype(BF), kmask, sn, vn)
        nb_ref[i, pl.ds(0, nf - KV_ROWS), :] = buf_ref[i, pl.ds(KV_ROWS, nf - KV_ROWS), :]
        nb_ref[i, pl.ds(nf - KV_ROWS, KV_ROWS), :] = new


def _win_attn_sample(layer, q3, cache, win_new, acc):
    depth, nb, nf, _ = cache.shape
    ins = [pl.BlockSpec((WIN_SB, A_HEADS, A_DH), lambda b: (b, 0, 0)),
           pl.BlockSpec((None, WIN_SB, nf, A_DH), lambda b: (layer, b, 0, 0)),
           pl.BlockSpec((WIN_SB, KV_ROWS, A_DH), lambda b: (b, 0, 0))]
    args = [q3, cache, win_new]
    if acc is not None:
        ins.append(pl.BlockSpec(memory_space=pl.ANY))
        args.append(acc)
    return _call(
        _win_s_body, (nb // WIN_SB,), ins,
        [pl.BlockSpec((WIN_SB, A_HEADS, A_DH), lambda b: (b, 0, 0)),
         pl.BlockSpec((None, WIN_SB, nf, A_DH), lambda b: (layer, b, 0, 0))],
        [jax.ShapeDtypeStruct((nb, A_HEADS, A_DH), F32), jax.ShapeDtypeStruct(cache.shape, F32)],
        name="win_attn_sample", aliases=None if acc is None else {3: 1})(*args)


def _ssm_disc_body(are_ref, aim_ref, ldt_ref, bre_ref, bim_ref, lre_ref, lim_ref, bbre_ref, bbim_ref):
    a_re, a_im = are_ref[...], aim_ref[...]
    dt = jnp.exp(ldt_ref[...])
    mag = jnp.exp(a_re * dt)
    lam_re = mag * jnp.cos(a_im * dt)
    lam_im = mag * jnp.sin(a_im * dt)
    den = a_re * a_re + a_im * a_im
    co_re = ((lam_re - 1.0) * a_re + lam_im * a_im) / den
    co_im = (lam_im * a_re - (lam_re - 1.0) * a_im) / den
    lre_ref[...] = lam_re
    lim_ref[...] = lam_im
    bbre_ref[...] = co_re * bre_ref[...] - co_im * bim_ref[...]
    bbim_ref[...] = co_re * bim_ref[...] + co_im * bre_ref[...]


def _ssm_disc(a_re, a_im, log_dt, b_re, b_im):
    flat = lambda a: a.reshape(1, S_FLAT)
    bt = lambda b: jnp.transpose(b, (2, 0, 1)).reshape(S_CH, S_FLAT)
    ldt = jnp.broadcast_to(log_dt[:, None], (S_GROUPS, S_STATE))
    v = jax.ShapeDtypeStruct((1, S_FLAT), F32)
    m = jax.ShapeDtypeStruct((S_CH, S_FLAT), F32)
    return pl.pallas_call(_ssm_disc_body, out_shape=[v, v, m, m], name="ssm_disc")(
        flat(a_re), flat(a_im), flat(ldt), bt(b_re), bt(b_im))


def _ssm_tail(hr, hi, u, cre, cim, d, wg, bg):
    y = _dot(hr.astype(BF), cre) - _dot(hi.astype(BF), cim) + d * u
    y = jax.nn.gelu(y)
    return (y * jax.nn.sigmoid(_dot(y.astype(BF), wg) + bg)).astype(BF)


def _ssm_body(u_ref, lre_ref, lim_ref, bbre_ref, bbim_ref, cre_ref, cim_ref, d_ref, wg_ref, bg_ref,
              o_ref, st_ref, xre, xim, hre, him):
    @pl.when(pl.program_id(1) == 0)
    def _():
        st_ref[...] = jnp.zeros_like(st_ref)

    tt = u_ref.shape[0]
    u = u_ref[...]
    ub = u.astype(BF)
    xre[...] = _dot(ub, bbre_ref[...])
    xim[...] = _dot(ub, bbim_ref[...])
    lre, lim = lre_ref[...], lim_ref[...]

    def step(t, c):
        hr, hi = c
        nr = lre * hr - lim * hi + xre[pl.ds(t, 1), :]
        ni = lre * hi + lim * hr + xim[pl.ds(t, 1), :]
        hre[pl.ds(t, 1), :] = nr
        him[pl.ds(t, 1), :] = ni
        return nr, ni

    hr, hi = lax.fori_loop(0, tt, step, (st_ref[0:1, :], st_ref[1:2, :]), unroll=4)
    st_ref[0:1, :] = hr
    st_ref[1:2, :] = hi
    o_ref[...] = _ssm_tail(hre[...], him[...], u, cre_ref[...], cim_ref[...], d_ref[...], wg_ref[...], bg_ref[...])


def _ssm_prompt(nb, t, z, sp):
    tt = 256
    nt = t // tt
    cs = [_full(a.shape) for a in sp]
    return _call(
        _ssm_body, (nb, nt),
        [pl.BlockSpec((tt, S_WIDTH), lambda b, i: (b * nt + i, ZSSM // S_WIDTH))] + cs,
        [pl.BlockSpec((tt, S_WIDTH), lambda b, i: (b * nt + i, 0)), pl.BlockSpec((None, 2, S_FLAT), lambda b, i: (b, 0, 0))],
        [jax.ShapeDtypeStruct((nb * t, S_WIDTH), BF), jax.ShapeDtypeStruct((nb, 2, S_FLAT), F32)],
        scratch=[pltpu.VMEM((tt, S_FLAT), F32)] * 4, name="ssm_prompt")(z, *sp)


def _ssm_s_body(u_ref, h0r_ref, h0i_ref, lre_ref, lim_ref, bbre_ref, bbim_ref, cre_ref, cim_ref, d_ref, wg_ref, bg_ref,
                o_ref, hr_ref, hi_ref):
    u = u_ref[...]
    ub = u.astype(BF)
    lre, lim = lre_ref[...], lim_ref[...]
    h0r, h0i = h0r_ref[...], h0i_ref[...]
    hr = _dot(ub, bbre_ref[...]) + (lre * h0r - lim * h0i)
    hi = _dot(ub, bbim_ref[...]) + (lre * h0i + lim * h0r)
    hr_ref[...] = hr
    hi_ref[...] = hi
    o_ref[...] = _ssm_tail(hr, hi, u, cre_ref[...], cim_ref[...], d_ref[...], wg_ref[...], bg_ref[...])


def _ssm_sample(z, h0r, h0i, sp):
    n = z.shape[0]
    st = pl.BlockSpec((n, S_FLAT), lambda i: (0, 0))
    return _call(
        _ssm_s_body, (1,),
        [pl.BlockSpec((n, S_WIDTH), lambda i: (0, ZSSM // S_WIDTH)), st, st] + [_full(a.shape) for a in sp],
        [pl.BlockSpec((n, S_WIDTH), lambda i: (0, 0)), st, st],
        [jax.ShapeDtypeStruct((n, S_WIDTH), BF), jax.ShapeDtypeStruct((n, S_FLAT), F32), jax.ShapeDtypeStruct((n, S_FLAT), F32)],
        name="ssm_sample")(z, h0r, h0i, *sp)


def _rwkv_prep_body(tiles_per_seq, *refs):
    z_refs, refs = refs[:5], refs[5:]
    if tiles_per_seq is None:
        p_refs, refs = refs[:5], refs[5:]
    m_refs, refs = refs[:5], refs[5:]
    w0_ref, w2_ref, a0_ref, a2_ref, g2_ref, kkw_ref, kaw_ref, rk_ref, hs_ref = refs[:9]
    r_ref, w_ref, k_ref, kk_ref, ka_ref, v_ref, g_ref, bonus_ref = refs[9:17]
    carry_refs = refs[17:]

    def mix(i):
        z = z_refs[i][...]
        if tiles_per_seq is None:
            prev = p_refs[i][...]
        else:
            first = jnp.where(pl.program_id(0) % tiles_per_seq == 0, 0.0, carry_refs[i][...])
            prev = jnp.where(lax.broadcasted_iota(I32, z.shape, 0) == 0, first, pltpu.roll(z, 1, 0))
            carry_refs[i][...] = z[z.shape[0] - 1:, :]
        return z + (prev - z) * m_refs[i][...]

    r, k, v = mix(0), mix(1), mix(2)
    wa = mix(3)
    gl = mix(4)
    w = -jax.nn.softplus(-(w0_ref[...] + _dot(jnp.tanh(wa).astype(BF), w2_ref[...]))) - 0.5
    a = jax.nn.sigmoid(a0_ref[...] + _dot(wa.astype(BF), a2_ref[...]))
    hs = hs_ref[...]
    kk = k * kkw_ref[...]
    kk = kk / jnp.maximum(jnp.sqrt(_dot(kk * kk, hs, HI)), 1e-12)
    k = k * (1.0 + (a - 1.0) * kaw_ref[...])
    r_ref[...] = r
    w_ref[...] = jnp.exp(-jnp.exp(w))
    k_ref[...] = k
    kk_ref[...] = kk
    ka_ref[...] = kk * a
    v_ref[...] = v
    g_ref[...] = _dot(jax.nn.sigmoid(gl).astype(BF), g2_ref[...])
    bonus_ref[...] = _dot(r * k * rk_ref[...], hs, HI) * v


def _rwkv_prep(n, tm, z, prev, mu, rp, seq=None):
    def spec(rows, width, off):
        assert off % width == 0
        return pl.BlockSpec((rows, width), (lambda i: (i, off // width)) if rows == tm else (lambda i: (0, off // width)))

    w = R_WIDTH
    offs = [(w, 0), (w, w), (w, 2 * w), (LANE, 3 * w), (LANE, 3 * w + LANE)]
    out = pl.BlockSpec((tm, w), lambda i: (i, 0))
    given = prev is not None
    return _call(
        functools.partial(_rwkv_prep_body, None if given else seq // tm), (n // tm,),
        [spec(tm, wd, ZRW + o) for wd, o in offs] + ([spec(tm, wd, o) for wd, o in offs] if given else [])
        + [spec(1, wd, o) for wd, o in offs] + [_full(a.shape) for a in rp],
        [out] * 8, [jax.ShapeDtypeStruct((n, w), F32)] * 8,
        scratch=[] if given else [pltpu.VMEM((1, wd), F32) for wd, _ in offs], name="rwkv_prep")(
            *([z] * 5), *([prev] * 5 if given else []), *([mu] * 5), *rp)


def _rwkv_scan_body(r_ref, w_ref, k_ref, kk_ref, ka_ref, v_ref, o_ref, st_ref, rs, ws, ks, kks, kas, vs, os_, *bufs):
    @pl.when(pl.program_id(1) == 0)
    def _():
        st_ref[...] = jnp.zeros_like(st_ref)

    tt = r_ref.shape[0]
    n = R_DH
    lc = RWKV_CHUNK
    for h in range(R_HEADS):
        sl = slice(h * n, (h + 1) * n)
        for src, dst in ((r_ref, rs), (w_ref, ws), (k_ref, ks), (kk_ref, kks), (ka_ref, kas), (v_ref, vs)):
            dst[h] = src[:, sl]
    eye = (lax.broadcasted_iota(I32, (n, n), 0) == lax.broadcasted_iota(I32, (n, n), 1)).astype(F32)
    m0 = jnp.concatenate([eye, jnp.zeros((lc, n), F32)], axis=0)
    xrow = lax.broadcasted_iota(I32, (2 * lc, lc), 0)
    xcol = lax.broadcasted_iota(I32, (2 * lc, lc), 1)
    xsign = jnp.where(xrow < lc, -1.0, 1.0)
    xkeep = jnp.where(xrow < lc, xrow, xrow - lc) <= xcol
    heads = range(R_HEADS)

    def rows_part(ci, buf):
        ms_ref, cs_ref, gam_ref = buf
        t0 = pl.multiple_of(ci * lc, lc)
        gs = [jnp.ones((1, n), F32)] * R_HEADS
        for h in heads:
            ms_ref[h] = m0
        for s in range(lc):
            t = t0 + s
            for h in heads:
                w = ws[h, pl.ds(t, 1), :]
                c = jnp.sum(ms_ref[h] * kks[h, pl.ds(t, 1), :], axis=1, keepdims=True)
                cs_ref[h, :, s:s + 1] = c
                ms_ref[h] = ms_ref[h] * w - c * kas[h, pl.ds(t, 1), :]
                ms_ref[h, n + s:n + s + 1, :] = ks[h, pl.ds(t, 1), :]
                gs[h] = gs[h] * w
                gam_ref[h, s:s + 1, :] = gs[h]

    def state_part(ci, buf):
        ms_ref, cs_ref, gam_ref = buf
        sl16 = pl.ds(pl.multiple_of(ci * lc, lc), lc)
        part = []
        for h in heads:
            g = gam_ref[h]
            ginv = 1.0 / g
            rt = g * rs[h, sl16, :]
            v16 = vs[h, sl16, :]
            st0 = st_ref[h]
            stack = jnp.concatenate([st0, v16], axis=0)
            x = _dot_nt(jnp.concatenate([kas[h, sl16, :] * ginv, ks[h, sl16, :] * ginv], axis=0), rt, HI)
            x = jnp.where(xkeep, x * xsign, 0.0)
            cst = _dot_tn(cs_ref[h], stack, HI)
            part.append((x, cst, v16, _dot(rt, st0, HI)))
            st_ref[h] = _dot_tn(ms_ref[h], stack, HI)
        for h in heads:
            x, cst, v16, o0 = part[h]
            os_[h, sl16, :] = o0 + _dot_tn(x, jnp.concatenate([cst, v16], axis=0), HI)

    nch = tt // lc
    buf_a, buf_b = bufs[:3], bufs[3:]
    rows_part(0, buf_a)

    def pair(pi, carry):
        c0 = 2 * pi
        rows_part(c0 + 1, buf_b)
        state_part(c0, buf_a)
        rows_part(jnp.minimum(c0 + 2, nch - 1), buf_a)
        state_part(c0 + 1, buf_b)
        return carry

    lax.fori_loop(0, nch // 2, pair, 0)
    for h in range(R_HEADS):
        o_ref[:, h * n:(h + 1) * n] = os_[h]


def _rwkv_scan(nb, t, r, w, k, kk, ka, v):
    tt = 256
    nt = t // tt
    row = pl.BlockSpec((tt, R_WIDTH), lambda b, i: (b * nt + i, 0))
    return _call(
        _rwkv_scan_body, (nb, nt), [row] * 6,
        [row, pl.BlockSpec((None, R_HEADS, R_DH, R_DH), lambda b, i: (b, 0, 0, 0))],
        [jax.ShapeDtypeStruct((nb * t, R_WIDTH), F32), jax.ShapeDtypeStruct((nb, R_HEADS, R_DH, R_DH), F32)],
        scratch=[pltpu.VMEM((R_HEADS, tt, R_DH), F32)] * 7
        + [pltpu.VMEM((R_HEADS, R_DH + RWKV_CHUNK, R_DH), F32), pltpu.VMEM((R_HEADS, R_DH + RWKV_CHUNK, RWKV_CHUNK), F32),
           pltpu.VMEM((R_HEADS, RWKV_CHUNK, R_DH), F32)] * 2,
        name="rwkv_scan")(r, w, k, kk, ka, v)


RWKV_SB = 8
RWKV_HG = 4


def _rwkv_step_body(r_ref, w_ref, k_ref, kk_ref, ka_ref, vt_ref, s_ref, ot_ref, so_ref):
    n = R_DH
    lane = lax.broadcasted_iota(I32, (n, RWKV_SB), 1)
    samples = range(RWKV_SB)
    for h0 in range(0, R_HEADS, RWKV_HG):
        heads = range(h0, h0 + RWKV_HG)
        rem = {(h, s): jnp.sum(s_ref[s, h] * kk_ref[h, s:s + 1, :], axis=1, keepdims=True) for h in heads for s in samples}
        outs = {}
        for h in heads:
            for s in samples:
                st = (s_ref[s, h] * w_ref[h, s:s + 1, :] - rem[h, s] * ka_ref[h, s:s + 1, :]
                      + vt_ref[h * n:(h + 1) * n, s:s + 1] * k_ref[h, s:s + 1, :])
                so_ref[s, h] = st
                outs[h, s] = jnp.sum(st * r_ref[h, s:s + 1, :], axis=1, keepdims=True)
        for h in heads:
            ot = jnp.zeros((n, RWKV_SB), F32)
            for s in samples:
                ot = jnp.where(lane == s, outs[h, s], ot)
            ot_ref[h * n:(h + 1) * n, :] = ot


def _rwkv_step(r, w, k, kk, ka, v, state, layer):
    nb = r.shape[0]
    nblk = nb // RWKV_SB
    vt = jnp.transpose(v.reshape(nblk, RWKV_SB, R_WIDTH), (0, 2, 1))
    r, w, k, kk, ka = (jnp.transpose(a.reshape(nblk, RWKV_SB, R_HEADS, R_DH), (0, 2, 1, 3)) for a in (r, w, k, kk, ka))
    row = pl.BlockSpec((None, R_HEADS, RWKV_SB, R_DH), lambda i: (i, 0, 0, 0))
    col = pl.BlockSpec((None, R_WIDTH, RWKV_SB), lambda i: (i, 0, 0))
    sts = pl.BlockSpec((RWKV_SB, R_HEADS, R_DH, R_DH), lambda i: (i, 0, 0, 0))
    sti = pl.BlockSpec((None, RWKV_SB, R_HEADS, R_DH, R_DH), lambda i: (layer, i, 0, 0, 0))
    ot, so = _call(
        _rwkv_step_body, (nblk,), [row] * 5 + [col, sti], [col, sts],
        [jax.ShapeDtypeStruct((nblk, R_WIDTH, RWKV_SB), F32), jax.ShapeDtypeStruct(state.shape[1:], F32)],
        name="rwkv_step")(r, w, k, kk, ka, vt, state)
    return jnp.transpose(ot, (0, 2, 1)).reshape(nb, R_WIDTH), so


def _rwkv_post_body(o_ref, g_ref, bonus_ref, lng_ref, lnb_ref, ha_ref, out_ref):
    o = o_ref[...]
    ha = ha_ref[...]
    cen = o - _dot(o, ha, HI)
    var = _dot(cen * cen, ha, HI)
    y = cen * lax.rsqrt(var + GN_EPS) * lng_ref[...] + lnb_ref[...]
    out_ref[...] = ((y + bonus_ref[...]) * g_ref[...]).astype(BF)


def _rwkv_post(n, tm, o, g, bonus, ln_g, ln_b, havg):
    row = pl.BlockSpec((tm, R_WIDTH), lambda i: (i, 0))
    vec = _full((1, R_WIDTH))
    return _call(_rwkv_post_body, (n // tm,), [row, row, row, vec, vec, _full(havg.shape)], row,
                 jax.ShapeDtypeStruct((n, R_WIDTH), BF), name="rwkv_post")(o, g, bonus, ln_g, ln_b, havg)


def _block_diag(blocks):
    g, a, b = blocks.shape
    return jnp.einsum('gab,gh->gahb', blocks, jnp.eye(g, dtype=blocks.dtype)).reshape(g * a, g * b)


class _LayerParams(NamedTuple):
    norm_g: jax.Array
    ffn: tuple
    w_in: tuple
    w_out: jax.Array
    q_gain: jax.Array
    k_gain: jax.Array
    cmp: tuple
    ssm: tuple
    rwkv_mu: jax.Array
    rwkv: tuple
    rwkv_ln: tuple


def _layer_params(l, p):
    w_in_r = (p['w_in_bf16'], l)
    ffn = p['ffn_bf16'] + (l,)
    cmp = (p['nsa_cmp_pe'][l], p['nsa_cmp_w1'][l].astype(BF), p['nsa_cmp_w2'][l].astype(BF), p['nsa_k_gain'][l, 0:1])
    lre, lim, bbre, bbim = _ssm_disc(p['ssm_a_re'][l], p['ssm_a_im'][l], p['ssm_log_dt'][l], p['ssm_b_re'][l], p['ssm_b_im'][l])
    to_gcp = lambda m: jnp.transpose(m.reshape(S_CH, S_GROUPS, S_STATE), (1, 0, 2))
    ssm = (lre, lim, _block_diag(to_gcp(bbre)).astype(BF), _block_diag(to_gcp(bbim)).astype(BF),
           _block_diag(jnp.transpose(p['ssm_c_re'][l], (0, 2, 1))).astype(BF),
           _block_diag(jnp.transpose(p['ssm_c_im'][l], (0, 2, 1))).astype(BF),
           p['ssm_d'][l].reshape(1, S_WIDTH), p['ssm_w_glu'][l].astype(BF), p['ssm_b_glu'][l].reshape(1, S_WIDTH))
    zl = jnp.zeros((R_DECAY_LORA, R_WIDTH), F32)
    vec = lambda a: a.reshape(1, R_WIDTH)
    hsum = _block_diag(jnp.ones((R_HEADS, R_DH, R_DH), F32))
    rwkv = (vec(p['rwkv_w0'][l]), jnp.concatenate([p['rwkv_w2'][l], zl], axis=0).astype(BF),
            vec(p['rwkv_a0'][l]), jnp.concatenate([zl, p['rwkv_a2'][l]], axis=0).astype(BF), p['rwkv_g2'][l].astype(BF),
            vec(p['rwkv_k_k'][l]), vec(p['rwkv_k_a'][l]), vec(p['rwkv_r_k'][l]), hsum)
    return _LayerParams(p['norm_g'][l], ffn, w_in_r, p['w_out'][l].astype(BF), p['nsa_q_gain'][l], p['nsa_k_gain'][l], cmp,
                        ssm, p['rwkv_mu'][l].reshape(1, R_IN), rwkv,
                        (vec(p['rwkv_ln_g'][l]), vec(p['rwkv_ln_b'][l]), hsum / R_DH))


def _layer_prompt(rows, nb, t, x, lp, tabs):
    n = rows.n
    h = _ffn(rows, x, lp.norm_g[0:1], 0, *lp.ffn, 0)
    z = _projin(rows, h, lp.norm_g[1:2], 3, *lp.w_in)
    qn, qr, slc_rows, win_rows, gates = _nsa_prep(n, 256, z, tabs, t // 256, lp.q_gain, lp.k_gain)
    ncb = t // CMP_BLOCK
    cmp_rows = z[:, ZCMP:ZSLC]
    xcmp = jnp.transpose(cmp_rows.reshape(nb, ncb, CMP_BLOCK, 2, A_KV, A_DH), (3, 0, 4, 1, 2, 5))
    ckv = _compress(xcmp.reshape(2, nb * A_KV * ncb, CMP_BLOCK * A_DH), *lp.cmp).reshape(2, nb, A_KV, ncb, A_DH)
    o_cmp, sel = _cmp_attn(nb, t, qn, ckv[0], ckv[1])
    o_slc, o_win = _slc_win_attn(nb, t, qr, slc_rows, win_rows, sel)
    o_nsa = _nsa_combine(n, 256, gates, o_cmp, o_slc, o_win)
    o_ssm, ssm_st = _ssm_prompt(nb, t, z, lp.ssm)
    r, w, k, kk, ka, v, g, bonus = _rwkv_prep(n, 256, z, None, lp.rwkv_mu, lp.rwkv, seq=t)
    o_scan, wkv_t = _rwkv_scan(nb, t, r, w, k, kk, ka, v)
    o_rwkv = _rwkv_post(n, 256, o_scan, g, bonus, *lp.rwkv_ln)
    h = _projout(rows, h, o_nsa, o_ssm, o_rwkv, lp.w_out)
    y = _ffn(rows, h, lp.norm_g[2:3], 6, *lp.ffn, 1)
    keep = min(WINDOW, t)
    shape6 = lambda a: a.reshape(nb, -1, 2, A_KV, A_DH)
    state = (shape6(cmp_rows), shape6(slc_rows), shape6(win_rows)[:, t - keep:],
             jnp.stack([ssm_st[:, 0], ssm_st[:, 1]], axis=-1).reshape(nb, S_GROUPS, S_STATE, 2),
             z.reshape(nb, t, ZW)[:, -1, ZRW:ZGATE], jnp.swapaxes(wkv_t, -1, -2))
    return y, state


def _layer_sample(rows, layer, x, lp, tabs, cache_cmp, cache_slc, cache_win, win_acc, page_table, ssm0, shift0, wkv_all):
    n = rows.n
    h = _ffn(rows, x, lp.norm_g[0:1], 0, *lp.ffn, 0)
    z = _projin(rows, h, lp.norm_g[1:2], 3, *lp.w_in)
    qn, qr, slc_new, win_new, gates = _nsa_prep(n, n, z, tabs, 1, lp.q_gain, lp.k_gain)
    cmp_new = z[:, ZCMP:ZSLC]
    ckv_past = _compress_past(layer, cache_cmp, page_table, *lp.cmp)
    xnew = jnp.transpose(cmp_new.reshape(n, 2, A_KV, A_DH), (1, 0, 2, 3)).reshape(2, n * A_KV, A_DH)
    xnew = jnp.pad(xnew, ((0, 0), (0, 0), (0, (CMP_BLOCK - 1) * A_DH)))
    ckv_new = _compress(xnew, *lp.cmp).reshape(2, n, A_KV, 1, A_DH)
    ckv_new = jnp.pad(jnp.transpose(ckv_new, (1, 0, 2, 3, 4)), ((0, 0), (0, 0), (0, 0), (0, 7), (0, 0)))
    q3n = qn.reshape(n, A_HEADS, A_DH)
    q3r = qr.reshape(n, A_HEADS, A_DH)
    o_cmp, idx, blk = _cmp_attn_sample(q3n, ckv_past, ckv_new, page_table)
    ids_flat = lambda a: a[:, :A_KV, :N_SEL].reshape(-1)
    o_slc = _slc_attn_sample(layer, q3r, cache_slc, ids_flat(idx), ids_flat(blk), slc_new.reshape(n, KV_ROWS, A_DH))
    o_win, win_acc = _win_attn_sample(layer, q3r, cache_win, win_new.reshape(n, KV_ROWS, A_DH), win_acc)
    flat = lambda a: a.reshape(n, A_WIDTH)
    o_nsa = _nsa_combine(n, n, gates, flat(o_cmp), flat(o_slc), flat(o_win))
    o_ssm, hr, hi = _ssm_sample(z, ssm0[..., 0].reshape(n, S_FLAT), ssm0[..., 1].reshape(n, S_FLAT), lp.ssm)
    r, w, k, kk, ka, v, g, bonus = _rwkv_prep(n, n, z, shift0, lp.rwkv_mu, lp.rwkv)
    o_step, wkv = _rwkv_step(r, w, k, kk, ka, v, wkv_all, layer)
    o_rwkv = _rwkv_post(n, n, o_step, g, bonus, *lp.rwkv_ln)
    h = _projout(rows, h, o_nsa, o_ssm, o_rwkv, lp.w_out)
    y = _ffn(rows, h, lp.norm_g[2:3], 6, *lp.ffn, 1)
    shape6 = lambda a: a.reshape(n, 1, 2, A_KV, A_DH)
    state = (shape6(cmp_new), shape6(slc_new), None,
             jnp.stack([hr, hi], axis=-1).reshape(n, S_GROUPS, S_STATE, 2), z[:, ZRW:ZGATE], wkv)
    return y, state, win_acc


def kernel(x_prompt, x_sample, cache_nsa_cmp, cache_nsa_slc, cache_nsa_win, state_ssm, state_rwkv_shift, state_rwkv_wkv, page_table, c_prompt, c_sample, norm_g, w_ada, b_ada, ffn_w1, ffn_w3, ffn_w2, w_in, w_out, nsa_q_gain, nsa_k_gain, nsa_cmp_pe, nsa_cmp_w1, nsa_cmp_w2, ssm_a_re, ssm_a_im, ssm_log_dt, ssm_b_re, ssm_b_im, ssm_c_re, ssm_c_im, ssm_d, ssm_w_glu, ssm_b_glu, rwkv_mu, rwkv_w0, rwkv_w2, rwkv_a0, rwkv_a2, rwkv_g2, rwkv_k_k, rwkv_k_a, rwkv_r_k, rwkv_ln_g, rwkv_ln_b):
    p = dict(norm_g=norm_g, ffn_w1=ffn_w1, ffn_w3=ffn_w3, ffn_w2=ffn_w2, w_in=w_in, w_out=w_out, nsa_q_gain=nsa_q_gain,
             nsa_k_gain=nsa_k_gain, nsa_cmp_pe=nsa_cmp_pe, nsa_cmp_w1=nsa_cmp_w1, nsa_cmp_w2=nsa_cmp_w2, ssm_a_re=ssm_a_re,
             ssm_a_im=ssm_a_im, ssm_log_dt=ssm_log_dt, ssm_b_re=ssm_b_re, ssm_b_im=ssm_b_im, ssm_c_re=ssm_c_re,
             ssm_c_im=ssm_c_im, ssm_d=ssm_d, ssm_w_glu=ssm_w_glu, ssm_b_glu=ssm_b_glu, rwkv_mu=rwkv_mu, rwkv_w0=rwkv_w0,
             rwkv_w2=rwkv_w2, rwkv_a0=rwkv_a0, rwkv_a2=rwkv_a2, rwkv_g2=rwkv_g2, rwkv_k_k=rwkv_k_k, rwkv_k_a=rwkv_k_a,
             rwkv_r_k=rwkv_r_k, rwkv_ln_g=rwkv_ln_g, rwkv_ln_b=rwkv_ln_b)
    p['ffn_bf16'] = (ffn_w1.astype(BF), ffn_w3.astype(BF), ffn_w2.astype(BF))
    p['w_in_bf16'] = jnp.concatenate(
        [w_in[:, :, :OFF_GATE], w_in[:, :, OFF_SSM:], w_in[:, :, OFF_GATE:OFF_SSM],
         jnp.zeros(w_in.shape[:2] + (ZW - ZGATE - 3 * A_HEADS,), F32)], axis=2).astype(BF)
    depth = w_in.shape[0]
    nbp, t, d = x_prompt.shape
    nbs, ts, _ = x_sample.shape
    assert ts == 1 and nbp <= 8 and d == D_MODEL
    cp8 = jnp.pad(c_prompt, ((0, 8 - nbp), (0, 0)))
    mod_p, mod_s = _ada(cp8, c_sample, w_ada, b_ada)
    tabs_p = _rope_tables(jnp.arange(t))
    tabs_s = _rope_tables(jnp.full((nbs,), PAST_LEN))
    hp = x_prompt.reshape(nbp * t, d)
    hs = x_sample.reshape(nbs, d)
    st_p, st_s = [], []
    cmp_flat = cache_nsa_cmp.reshape(cache_nsa_cmp.shape[:2] + (PAGE_SIZE * KV_ROWS, A_DH))
    slc_flat = cache_nsa_slc.reshape(cache_nsa_slc.shape[0], -1, SEL_BLOCK * KV_ROWS, A_DH)
    win_flat = cache_nsa_win.reshape(cache_nsa_win.shape[:2] + (cache_nsa_win.shape[2] * KV_ROWS, A_DH))
    win_acc = None
    for l in range(depth):
        lp = _layer_params(l, p)
        rows_p = _Rows(nbp * t, t, mod_p[l].reshape(8, 1, 9 * d), False, 512)
        rows_s = _Rows(nbs, 1, mod_s[l], True, nbs)
        hp, sp = _layer_prompt(rows_p, nbp, t, hp, lp, tabs_p)
        hs, ss, win_acc = _layer_sample(rows_s, l, hs, lp, tabs_s, cmp_flat, slc_flat, win_flat, win_acc, page_table,
                                        state_ssm[l], state_rwkv_shift[l], state_rwkv_wkv)
        st_p.append(sp)
        st_s.append(ss)
    outs = [hp.reshape(nbp, t, d), hs.reshape(nbs, 1, d)]
    for i in range(6):
        outs.append(jnp.stack([s[i] for s in st_p]))
        outs.append(win_acc.reshape(cache_nsa_win.shape) if i == 2 else jnp.stack([s[i] for s in st_s]))
    return tuple(outs)
```

```python
import functools
import math
from typing import NamedTuple

import jax
import jax.numpy as jnp
from jax import lax
from jax.experimental import pallas as pl
from jax.experimental.pallas import tpu as pltpu

F32 = jnp.float32
BF = jnp.bfloat16
I32 = jnp.int32
HI = lax.Precision.HIGHEST

D_MODEL = 2048
PAST_LEN = 2048
PAGE_SIZE = 128
A_HEADS, A_KV, A_HPG, A_DH = 8, 2, 4, 128
A_WIDTH = A_HEADS * A_DH
A_KVW = A_KV * A_DH
CMP_BLOCK, SEL_BLOCK, N_SEL, WINDOW = 32, 64, 16, 512
ROT_DIM = A_DH // 4
ROPE_THETA = 500000.0
ATT_SCALE = A_DH ** -0.5
FORCE_BONUS = 1.0e4
S_GROUPS, S_CH, S_STATE = 32, 16, 64
S_WIDTH = S_GROUPS * S_CH
S_FLAT = S_GROUPS * S_STATE
R_HEADS, R_DH = 8, 64
R_WIDTH = R_HEADS * R_DH
R_DECAY_LORA, R_A_LORA, R_GATE_LORA = 64, 64, 128
R_IN = 3 * R_WIDTH + R_DECAY_LORA + R_A_LORA + R_GATE_LORA
GN_EPS = 64e-5
OFF_KV = A_WIDTH
OFF_GATE = OFF_KV + 6 * A_KVW
OFF_SSM = OFF_GATE + 3 * A_HEADS
OFF_RWKV = OFF_SSM + S_WIDTH

ZQ, ZCMP, ZSLC, ZWIN, ZSSM, ZRW = 0, 1024, 1536, 2048, 2560, 3072
ZGATE = ZRW + R_IN
ZW = ZGATE + 128
LANE = 128
RWKV_CHUNK = 16


def _dot(a, b, precision=None):
    return jnp.dot(a, b, preferred_element_type=F32, precision=precision)


def _dot_nt(a, b, precision=None):
    return lax.dot_general(a, b, (((1,), (1,)), ((), ())), preferred_element_type=F32, precision=precision)


def _dot_tn(a, b, precision=None):
    return lax.dot_general(a, b, (((0,), (0,)), ((), ())), preferred_element_type=F32, precision=precision)


def _call(body, grid, in_specs, out_specs, out_shape, scratch=(), nsp=0, name=None, aliases=None):
    gs = pltpu.PrefetchScalarGridSpec(num_scalar_prefetch=nsp, grid=grid, in_specs=in_specs, out_specs=out_specs,
                                      scratch_shapes=list(scratch))
    return pl.pallas_call(body, grid_spec=gs, out_shape=out_shape, name=name, input_output_aliases=aliases or {},
                          compiler_params=pltpu.CompilerParams(dimension_semantics=("arbitrary",) * len(grid)))


def _rms(x, g):
    return x * lax.rsqrt(jnp.mean(x * x, axis=-1, keepdims=True) + 1e-6) * g


def _finite_or_zero(m):
    return jnp.where(jnp.abs(m) < jnp.inf, m, 0.0)


def _masked_exp(s, mask, axis):
    s = jnp.where(mask, s, -jnp.inf)
    m = _finite_or_zero(jnp.max(s, axis=axis, keepdims=True))
    e = jnp.exp(s - m)
    return e, 1.0 / jnp.maximum(jnp.sum(e, axis=axis, keepdims=True), 1e-30)


def _masked_softmax(s, mask, axis):
    e, r = _masked_exp(s, mask, axis)
    return e * r


class _Rows(NamedTuple):
    n: int
    seq: int
    mod: jax.Array
    per_row: bool
    tm: int


def _mod_spec(rows, tm, col, width=D_MODEL, jdep=False):
    per = D_MODEL // width
    if rows.per_row:
        return pl.BlockSpec((tm, width), lambda i, j: (i, col * per + (j if jdep else 0)))
    tpb = rows.seq // tm
    return pl.BlockSpec((None, 1, width), lambda i, j: (i // tpb, 0, col * per + (j if jdep else 0)))


def _zspec(tm, width, off):
    assert off % width == 0
    return pl.BlockSpec((tm, width), lambda i, j: (i, off // width))


def _full(shape):
    nd = len(shape)
    return pl.BlockSpec(shape, lambda *a: (0,) * nd)


def _ada_body(cp_ref, cs_ref, w_ref, b_ref, op_ref, os_ref):
    w = w_ref[...].astype(BF)
    b = b_ref[...]
    for c_ref, o_ref in ((cp_ref, op_ref), (cs_ref, os_ref)):
        c = c_ref[...]
        o_ref[...] = _dot((c * jax.nn.sigmoid(c)).astype(BF), w) + b


def _ada(cp8, cs, w_ada, b_ada):
    nl, d, n = w_ada.shape
    tn = 1024
    ns = cs.shape[0]
    return _call(
        _ada_body, (nl, n // tn),
        [_full((8, d)), _full((ns, d)),
         pl.BlockSpec((None, d, tn), lambda l, j: (l, 0, j)),
         pl.BlockSpec((None, 1, tn), lambda l, j: (l, 0, j))],
        [pl.BlockSpec((None, 8, tn), lambda l, j: (l, 0, j)),
         pl.BlockSpec((None, ns, tn), lambda l, j: (l, 0, j))],
        [jax.ShapeDtypeStruct((nl, 8, n), F32), jax.ShapeDtypeStruct((nl, ns, n), F32)],
        name="ada")(cp8, cs, w_ada, b_ada.reshape(nl, 1, n))


def _norm_mod(x, g, scale, shift):
    return _rms(x, g) * (1.0 + scale) + shift


def _ffn_body(x_ref, g_ref, sh_ref, sc_ref, gt_ref, w1_ref, w3_ref, w2_ref, o_ref, xn_ref):
    f = pl.program_id(1)

    @pl.when(f == 0)
    def _():
        xn_ref[...] = _norm_mod(x_ref[...], g_ref[...], sc_ref[...], sh_ref[...]).astype(BF)
        o_ref[...] = jnp.zeros_like(o_ref)

    xn = xn_ref[...]
    h1 = _dot(xn, w1_ref[...])
    h3 = _dot(xn, w3_ref[...])
    o_ref[...] += _dot((h1 * jax.nn.sigmoid(h1) * h3).astype(BF), w2_ref[...])

    @pl.when(f == pl.num_programs(1) - 1)
    def _():
        o_ref[...] = x_ref[...] + 0.5 * gt_ref[...] * o_ref[...]


def _ffn(rows, x, g, col0, w1, w3, w2, layer, which):
    d, ff = w1.shape[2:]
    tm = rows.tm
    tf = 512 if tm <= 512 else 256
    return _call(
        _ffn_body, (rows.n // tm, ff // tf),
        [pl.BlockSpec((tm, d), lambda i, f: (i, 0)), _full((1, d)),
         _mod_spec(rows, tm, col0), _mod_spec(rows, tm, col0 + 1), _mod_spec(rows, tm, col0 + 2),
         pl.BlockSpec((None, None, d, tf), lambda i, f: (layer, which, 0, f)),
         pl.BlockSpec((None, None, d, tf), lambda i, f: (layer, which, 0, f)),
         pl.BlockSpec((None, None, tf, d), lambda i, f: (layer, which, f, 0))],
        pl.BlockSpec((tm, d), lambda i, f: (i, 0)),
        jax.ShapeDtypeStruct((rows.n, d), F32),
        scratch=[pltpu.VMEM((tm, d), BF)], name="ffn")(x, g, rows.mod, rows.mod, rows.mod, w1, w3, w2)


def _projin_body(x_ref, g_ref, sh_ref, sc_ref, w_ref, o_ref, xn_ref):
    @pl.when(pl.program_id(1) == 0)
    def _():
        xn_ref[...] = _norm_mod(x_ref[...], g_ref[...], sc_ref[...], sh_ref[...]).astype(BF)

    o_ref[...] = _dot(xn_ref[...], w_ref[...])


def _projin(rows, x, g, col0, w, layer):
    d, n = w.shape[1:]
    tm, tn = rows.tm, 1664
    return _call(
        _projin_body, (rows.n // tm, n // tn),
        [pl.BlockSpec((tm, d), lambda i, j: (i, 0)), _full((1, d)),
         _mod_spec(rows, tm, col0), _mod_spec(rows, tm, col0 + 1),
         pl.BlockSpec((None, d, tn), lambda i, j: (layer, 0, j))],
        pl.BlockSpec((tm, tn), lambda i, j: (i, j)),
        jax.ShapeDtypeStruct((rows.n, n), F32),
        scratch=[pltpu.VMEM((tm, d), BF)], name="proj_in")(x, g, rows.mod, rows.mod, w)


def _projout_body(h_ref, gt_ref, a_ref, b_ref, c_ref, wa_ref, wb_ref, wc_ref, o_ref):
    mix = _dot(a_ref[...], wa_ref[...]) + _dot(b_ref[...], wb_ref[...]) + _dot(c_ref[...], wc_ref[...])
    o_ref[...] = h_ref[...] + gt_ref[...] * mix


def _projout(rows, h, o_nsa, o_ssm, o_rwkv, w):
    d = h.shape[1]
    tm, tn = rows.tm, 512
    nblk = A_WIDTH // S_WIDTH
    return _call(
        _projout_body, (rows.n // tm, d // tn),
        [pl.BlockSpec((tm, tn), lambda i, j: (i, j)), _mod_spec(rows, tm, 5, tn, True),
         pl.BlockSpec((tm, A_WIDTH), lambda i, j: (i, 0)), pl.BlockSpec((tm, S_WIDTH), lambda i, j: (i, 0)),
         pl.BlockSpec((tm, R_WIDTH), lambda i, j: (i, 0)),
         pl.BlockSpec((A_WIDTH, tn), lambda i, j: (0, j)), pl.BlockSpec((S_WIDTH, tn), lambda i, j: (nblk, j)),
         pl.BlockSpec((R_WIDTH, tn), lambda i, j: (nblk + 1, j))],
        pl.BlockSpec((tm, tn), lambda i, j: (i, j)),
        jax.ShapeDtypeStruct((rows.n, d), F32), name="proj_out")(h, rows.mod, o_nsa, o_ssm, o_rwkv, w, w, w)


def _rope_tables(pos):
    half = ROT_DIM // 2
    inv = ROPE_THETA ** (-2.0 * jnp.arange(half, dtype=F32) / ROT_DIM)
    ang = pos.astype(F32)[:, None] * inv[None, :]
    cos, sin = jnp.cos(ang), jnp.sin(ang)
    n = pos.shape[0]
    ct = jnp.concatenate([cos, cos, jnp.ones((n, A_DH - ROT_DIM), F32)], axis=1)
    sa = jnp.concatenate([-sin, jnp.zeros((n, A_DH - half), F32)], axis=1)
    sb = jnp.concatenate([jnp.zeros((n, half), F32), sin, jnp.zeros((n, A_DH - ROT_DIM), F32)], axis=1)
    return ct, sa, sb


def _prep_body(q_ref, ks_ref, kw_ref, gt_ref, ct_ref, sa_ref, sb_ref, qg_ref, kg1_ref, kg2_ref,
               qn_ref, qr_ref, slc_ref, win_ref, go_ref):
    ct, sa, sb = ct_ref[...], sa_ref[...], sb_ref[...]
    half = ROT_DIM // 2

    def rope(x):
        return x * ct + pltpu.roll(x, A_DH - half, 1) * sa + pltpu.roll(x, half, 1) * sb

    for h in range(A_HEADS):
        sl = slice(h * A_DH, (h + 1) * A_DH)
        x = _rms(q_ref[:, sl], qg_ref[...])
        qn_ref[:, sl] = (x * ATT_SCALE).astype(BF)
        qr_ref[:, sl] = (rope(x) * ATT_SCALE).astype(BF)
    for src, dst, kg in ((ks_ref, slc_ref, kg1_ref), (kw_ref, win_ref, kg2_ref)):
        for g in range(A_KV):
            sl = slice(g * A_DH, (g + 1) * A_DH)
            dst[:, sl] = rope(_rms(src[:, sl], kg[...]))
        dst[:, A_KVW:] = src[:, A_KVW:]
    go_ref[...] = jax.nn.sigmoid(gt_ref[...])


def _nsa_prep(n, tm, z, tabs, tab_blocks, q_gain, k_gain):
    tspec = pl.BlockSpec((tm, LANE), lambda i, j: (i % tab_blocks, 0))
    vspec = _full((1, A_DH))
    return _call(
        _prep_body, (n // tm, 1),
        [_zspec(tm, A_WIDTH, ZQ), _zspec(tm, 2 * A_KVW, ZSLC), _zspec(tm, 2 * A_KVW, ZWIN), _zspec(tm, LANE, ZGATE),
         tspec, tspec, tspec, vspec, vspec, vspec],
        [pl.BlockSpec((tm, A_WIDTH), lambda i, j: (i, 0)), pl.BlockSpec((tm, A_WIDTH), lambda i, j: (i, 0)),
         pl.BlockSpec((tm, 2 * A_KVW), lambda i, j: (i, 0)), pl.BlockSpec((tm, 2 * A_KVW), lambda i, j: (i, 0)),
         pl.BlockSpec((tm, LANE), lambda i, j: (i, 0))],
        [jax.ShapeDtypeStruct((n, A_WIDTH), BF), jax.ShapeDtypeStruct((n, A_WIDTH), BF),
         jax.ShapeDtypeStruct((n, 2 * A_KVW), F32), jax.ShapeDtypeStruct((n, 2 * A_KVW), F32),
         jax.ShapeDtypeStruct((n, LANE), F32)],
        name="nsa_prep")(z, z, z, z, *tabs, q_gain.reshape(1, A_DH), k_gain[1:2], k_gain[2:3])


def _cmp_tail(acc, w2, kg, is_k):
    out = _dot(jax.nn.gelu(acc).astype(BF), w2)
    return jnp.where(is_k, _rms(out, kg), out)


def _compress_body(x_ref, pe_ref, w1_ref, w2_ref, kg_ref, o_ref):
    x = (x_ref[...] + pe_ref[...]).astype(BF)
    o_ref[...] = _cmp_tail(_dot(x, w1_ref[...]), w2_ref[...], kg_ref[...], pl.program_id(0) == 0)


def _compress(x, pe, w1, w2, kg):
    _, m, f = x.shape
    tm = min(m, 256)
    return _call(
        _compress_body, (2, m // tm),
        [pl.BlockSpec((None, tm, f), lambda s, i: (s, i, 0)), pl.BlockSpec((None, 1, f), lambda s, i: (s, 0, 0)),
         pl.BlockSpec((None, f, A_DH), lambda s, i: (s, 0, 0)), pl.BlockSpec((None, A_DH, A_DH), lambda s, i: (s, 0, 0)),
         _full((1, A_DH))],
        pl.BlockSpec((None, tm, A_DH), lambda s, i: (s, i, 0)),
        jax.ShapeDtypeStruct((2, m, A_DH), F32), name="compress")(x, pe.reshape(2, 1, f), w1, w2, kg)


PAST_NB = 2


SUB = 8
CMP_RPT = SUB // (2 * A_KV)
CMP_QUAD = 2 * CMP_RPT


def _cmp_past_body(pt_ref, *refs):
    npg = PAST_LEN // PAGE_SIZE
    pages = refs[:PAST_NB * npg]
    pe_ref, w1_ref, w2_ref, kg_ref, o_ref = refs[PAST_NB * npg:]
    kvr = 2 * A_KV
    bpp = PAGE_SIZE // CMP_BLOCK
    nblk = PAST_NB * (PAST_LEN // CMP_BLOCK)
    rows = nblk * SUB
    acc = jnp.zeros((rows, CMP_QUAD * A_DH), F32)
    for q in range(CMP_BLOCK // CMP_QUAD):
        halves = []
        for half in range(2):
            off = (q * 2 + half) * SUB
            tiles = jnp.stack([pg[c * CMP_BLOCK * kvr + off:c * CMP_BLOCK * kvr + off + SUB, :]
                               for pg in pages for c in range(bpp)], axis=0)
            halves.append((tiles + pe_ref[off:off + SUB, :][None]).reshape(rows, A_DH).astype(BF))
        acc = acc + _dot(jnp.concatenate(halves, axis=1), w1_ref[q])
    j = jnp.bitwise_and(lax.broadcasted_iota(I32, (rows, A_DH), 0), SUB - 1)
    is_v = jnp.bitwise_and(j, kvr - 1) >= A_KV
    blk = jnp.right_shift(j, int(math.log2(kvr))) * 2 + is_v.astype(I32)
    sel = jnp.zeros((rows, A_DH), F32)
    for b in range(CMP_QUAD):
        sel = jnp.where(blk == b, acc[:, b * A_DH:(b + 1) * A_DH], sel)
    hid = jax.nn.gelu(sel + pltpu.roll(sel, rows - kvr, 0)).astype(BF)
    out2 = _dot(hid, w2_ref[...])
    out = jnp.where(is_v, out2[:, A_DH:], _rms(out2[:, :A_DH], kg_ref[...]))
    o_ref[...] = out.reshape(nblk, SUB, A_DH)


def _compress_past(layer, cache, page_table, pe, w1, w2, kg):
    nb, npg = page_table.shape
    ncb = PAST_LEN // CMP_BLOCK
    kvr = 2 * A_KV
    nq = CMP_BLOCK // CMP_QUAD

    def page_spec(i, p):
        return pl.BlockSpec((None, None, PAGE_SIZE * kvr, A_DH), lambda b, pt: (layer, pt[(b * PAST_NB + i) * npg + p], 0, 0))

    pe_flat = jnp.broadcast_to(jnp.transpose(pe, (1, 0, 2))[:, :, None, :], (CMP_BLOCK, 2, A_KV, A_DH)).reshape(CMP_BLOCK * kvr, A_DH)
    w1q = jnp.transpose(w1.reshape(2, nq, 2, CMP_RPT, A_DH, A_DH), (1, 2, 4, 3, 0, 5)).reshape(nq, 2 * A_DH, CMP_QUAD * A_DH)
    w2c = jnp.concatenate([w2[0], w2[1]], axis=1)
    specs = [page_spec(i, p) for i in range(PAST_NB) for p in range(npg)]
    out = _call(
        _cmp_past_body, (nb // PAST_NB,),
        specs + [_full(pe_flat.shape), _full(w1q.shape), _full(w2c.shape), _full((1, A_DH))],
        pl.BlockSpec((PAST_NB * ncb, SUB, A_DH), lambda b, pt: (b, 0, 0)),
        jax.ShapeDtypeStruct((nb * ncb, SUB, A_DH), F32), nsp=1, name="compress_past")(
            page_table.reshape(-1), *([cache] * (PAST_NB * npg)), pe_flat, w1q, w2c, kg)
    return jnp.transpose(out[:, :kvr].reshape(nb, ncb, 2, A_KV, A_DH), (0, 2, 3, 1, 4))


def _select(score, nblk):
    j = lax.broadcasted_iota(I32, score.shape, 0)
    rank = jnp.zeros(score.shape, F32)
    for k in range(nblk):
        rk = score[k:k + 1, :]
        rank = rank + ((rk > score) | ((rk == score) & (k < j))).astype(F32)
    return rank


def _cmpattn_body(q_ref, ck_ref, cv_ref, o_ref, sel_ref):
    qi = pl.program_id(1)
    tq = q_ref.shape[0]
    nc = ck_ref.shape[1]
    ns = nc // 2
    base = qi * tq
    sel_rows = []
    for g in range(A_KV):
        q4 = jnp.concatenate([q_ref[:, (g * A_HPG + h) * A_DH:(g * A_HPG + h + 1) * A_DH] for h in range(A_HPG)], axis=0)
        ckg = ck_ref[g].astype(BF)
        cvg = cv_ref[g].astype(BF)
        pos = base + jnp.bitwise_and(lax.broadcasted_iota(I32, (A_HPG * tq, nc), 0), tq - 1)
        blk_end = (lax.broadcasted_iota(I32, (A_HPG * tq, nc), 1) + 1) * CMP_BLOCK - 1
        p = _masked_softmax(_dot_nt(q4, ckg), blk_end <= pos, -1)
        o = _dot(p.astype(BF), cvg)
        for h in range(A_HPG):
            o_ref[:, (g * A_HPG + h) * A_DH:(g * A_HPG + h + 1) * A_DH] = o[h * tq:(h + 1) * tq]
        ckp = jnp.concatenate([ck_ref[g, pl.ds(0, ns, stride=2), :], ck_ref[g, pl.ds(1, ns, stride=2), :]], axis=0).astype(BF)
        row = lax.broadcasted_iota(I32, (nc, A_HPG * tq), 0)
        blk = jnp.where(row < ns, 2 * row, 2 * (row - ns) + 1)
        post = base + jnp.bitwise_and(lax.broadcasted_iota(I32, (nc, A_HPG * tq), 1), tq - 1)
        pt = _masked_softmax(_dot_nt(ckp, q4), (blk + 1) * CMP_BLOCK - 1 <= post, 0)
        imp = pt[:, 0:tq]
        for h in range(1, A_HPG):
            imp = imp + pt[:, h * tq:(h + 1) * tq]
        imp = imp[:ns] + imp[ns:]
        j = lax.broadcasted_iota(I32, (ns, tq), 0)
        pos2 = base + lax.broadcasted_iota(I32, (ns, tq), 1)
        cur = jnp.right_shift(pos2, int(math.log2(SEL_BLOCK)))
        forced = (j == 0) | (j == cur) | (j == cur - 1)
        score = jnp.where(j * SEL_BLOCK <= pos2, imp + jnp.where(forced, FORCE_BONUS, 0.0), -1e9)
        sel_rows.append((_select(score, ns) < min(N_SEL, ns)).astype(F32))
    pad = jnp.zeros((LANE - A_KV * ns, tq), F32)
    sel_ref[...] = jnp.concatenate(sel_rows + [pad], axis=0).T


def _cmp_attn(nb, t, qn, ck, cv):
    tq = 256
    nc = ck.shape[2]
    cspec = pl.BlockSpec((None, A_KV, nc, A_DH), lambda b, i: (b, 0, 0, 0))
    return _call(
        _cmpattn_body, (nb, t // tq),
        [pl.BlockSpec((tq, A_WIDTH), lambda b, i: (b * (t // tq) + i, 0)), cspec, cspec],
        [pl.BlockSpec((tq, A_WIDTH), lambda b, i: (b * (t // tq) + i, 0)),
         pl.BlockSpec((tq, LANE), lambda b, i: (b * (t // tq) + i, 0))],
        [jax.ShapeDtypeStruct((nb * t, A_WIDTH), F32), jax.ShapeDtypeStruct((nb * t, LANE), F32)],
        name="cmp_attn")(qn, ck, cv)


SLC_SPAN = 512


def _slcwin_body(q_ref, sk_ref, wk_ref, sel_ref, gt_ref, oc_ref, o_ref, os_ref, ow_ref):
    qi = pl.program_id(1)
    tq = q_ref.shape[0]
    t = sk_ref.shape[0]
    ns = t // SEL_BLOCK
    base = qi * tq
    wkeys = WINDOW + tq
    wstart = pl.multiple_of(jnp.maximum(base - WINDOW, 0), tq)
    selb = sel_ref[...].astype(BF)
    qpw = base + jnp.bitwise_and(lax.broadcasted_iota(I32, (A_HPG * tq, wkeys), 0), tq - 1)
    kpw = wstart + lax.broadcasted_iota(I32, (A_HPG * tq, wkeys), 1)
    wmask = (kpw <= qpw) & (kpw > qpw - WINDOW)

    def heads_of(g):
        return jnp.concatenate([q_ref[:, (g * A_HPG + h) * A_DH:(g * A_HPG + h + 1) * A_DH] for h in range(A_HPG)], axis=0)

    def put(ref, g, o):
        for h in range(A_HPG):
            ref[:, (g * A_HPG + h) * A_DH:(g * A_HPG + h + 1) * A_DH] = o[h * tq:(h + 1) * tq]

    for g in range(A_KV):
        ksl = slice(g * A_DH, (g + 1) * A_DH)
        vsl = slice(A_KVW + g * A_DH, A_KVW + (g + 1) * A_DH)
        ew, rw = _masked_exp(_dot_nt(heads_of(g), wk_ref[pl.ds(wstart, wkeys), ksl].astype(BF)), wmask, -1)
        put(ow_ref, g, _dot(ew.astype(BF), wk_ref[pl.ds(wstart, wkeys), vsl].astype(BF)) * rw)

    nspan = t // SLC_SPAN
    for c in range(nspan):
        @pl.when((base + tq - 1) // SLC_SPAN == c)
        def _(c=c):
            nk = (c + 1) * SLC_SPAN
            lrow = lax.broadcasted_iota(I32, (LANE, nk), 0)
            kblk = jnp.right_shift(lax.broadcasted_iota(I32, (LANE, nk), 1), int(math.log2(SEL_BLOCK)))
            qpos = base + jnp.bitwise_and(lax.broadcasted_iota(I32, (A_HPG * tq, nk), 0), tq - 1)
            causal = lax.broadcasted_iota(I32, (A_HPG * tq, nk), 1) <= qpos
            for g in range(A_KV):
                expand = (lrow == g * ns + kblk).astype(BF)
                picked = _dot(selb, expand)
                smask = (jnp.concatenate([picked] * A_HPG, axis=0) > 0.5) & causal
                e, r = _masked_exp(_dot_nt(heads_of(g), sk_ref[0:nk, g * A_DH:(g + 1) * A_DH].astype(BF)), smask, -1)
                put(os_ref, g, _dot(e.astype(BF), sk_ref[0:nk, A_KVW + g * A_DH:A_KVW + (g + 1) * A_DH].astype(BF)) * r)

    _combine_body(gt_ref, oc_ref, os_ref, ow_ref, o_ref)


def _slc_win_attn(nb, t, qr, slc_rows, win_rows, sel, gates, o_cmp):
    tq = 128
    nq = t // tq
    rowspec = pl.BlockSpec((tq, A_WIDTH), lambda b, i: (b * nq + i, 0))
    lanespec = pl.BlockSpec((tq, LANE), lambda b, i: (b * nq + i, 0))
    kvspec = pl.BlockSpec((t, 2 * A_KVW), lambda b, i: (b, 0))
    return _call(
        _slcwin_body, (nb, nq), [rowspec, kvspec, kvspec, lanespec, lanespec, rowspec], rowspec,
        jax.ShapeDtypeStruct((nb * t, A_WIDTH), BF), scratch=[pltpu.VMEM((tq, A_WIDTH), F32)] * 2,
        name="slc_win_attn")(qr, slc_rows, win_rows, sel, gates, o_cmp)


def _combine_body(g_ref, oc_ref, os_ref, ow_ref, o_ref):
    gt = g_ref[...]
    for hd in range(A_HEADS):
        sl = slice(hd * A_DH, (hd + 1) * A_DH)
        acc = (gt[:, hd:hd + 1] * oc_ref[:, sl] + gt[:, A_HEADS + hd:A_HEADS + hd + 1] * os_ref[:, sl]
               + gt[:, 2 * A_HEADS + hd:2 * A_HEADS + hd + 1] * ow_ref[:, sl])
        o_ref[:, sl] = acc.astype(BF)


def _nsa_combine(n, tm, gates, o_cmp, o_slc, o_win):
    spec = pl.BlockSpec((tm, A_WIDTH), lambda i: (i, 0))
    return _call(_combine_body, (n // tm,), [pl.BlockSpec((tm, LANE), lambda i: (i, 0)), spec, spec, spec], spec,
                 jax.ShapeDtypeStruct((n, A_WIDTH), BF), name="nsa_combine")(gates, o_cmp, o_slc, o_win)


CMP_SB = 4


def _cmpattn_s_body(*refs):
    for i in range(CMP_SB):
        _cmpattn_s_one(*(r.at[i] for r in refs))


def _cmpattn_s_one(q_ref, cp_ref, cn_ref, pt_ref, o_ref, idx_ref, blk_ref):
    pos = PAST_LEN
    q = q_ref[...]
    ncp = cp_ref.shape[2]
    ns = ncp // 2
    nrow = ns + 8
    hrow = lax.broadcasted_iota(I32, (A_HEADS, A_DH), 0)
    o_all = jnp.zeros((A_HEADS, A_DH), F32)
    idx_ref[...] = jnp.zeros(idx_ref.shape, I32)
    blk_ref[...] = jnp.zeros(blk_ref.shape, I32)
    new_ok =(ncp + 1) * CMP_BLOCK - 1 <= pos
    for g in range(A_KV):
        ck = cp_ref[0, g].astype(BF)
        cv = cp_ref[1, g].astype(BF)
        ckn = cn_ref[0, g].astype(BF)
        cvn = cn_ref[1, g]
        vp = (lax.broadcasted_iota(I32, (A_HEADS, ncp), 1) + 1) * CMP_BLOCK - 1 <= pos
        vn = (lax.broadcasted_iota(I32, (A_HEADS, 8), 1) == 0) & new_ok
        sp = jnp.where(vp, _dot_nt(q, ck), -jnp.inf)
        sn = jnp.where(vn, _dot_nt(q, ckn), -jnp.inf)
        m = jnp.maximum(jnp.max(sp, axis=-1, keepdims=True), jnp.max(sn, axis=-1, keepdims=True))
        m = _finite_or_zero(m)
        ep = jnp.where(vp, jnp.exp(sp - m), 0.0)
        en = jnp.where(vn, jnp.exp(sn - m), 0.0)
        den = jnp.maximum(jnp.sum(ep, axis=-1, keepdims=True) + jnp.sum(en, axis=-1, keepdims=True), 1e-30)
        o = _dot((ep / den).astype(BF), cv) + (en / den)[:, 0:1] * cvn[0:1, :]
        o_all = jnp.where((hrow >= g * A_HPG) & (hrow < (g + 1) * A_HPG), o, o_all)
        cke = cp_ref[0, g, pl.ds(0, ns, stride=2), :].astype(BF)
        cko = cp_ref[0, g, pl.ds(1, ns, stride=2), :].astype(BF)
        rowe = lax.broadcasted_iota(I32, (ns, A_HEADS), 0)
        ve = (2 * rowe + 1) * CMP_BLOCK - 1 <= pos
        vo = (2 * rowe + 2) * CMP_BLOCK - 1 <= pos
        vnt = (lax.broadcasted_iota(I32, (8, A_HEADS), 0) == 0) & new_ok
        ste = jnp.where(ve, _dot_nt(cke, q), -jnp.inf)
        sto = jnp.where(vo, _dot_nt(cko, q), -jnp.inf)
        stn = jnp.where(vnt, _dot_nt(ckn, q), -jnp.inf)
        mt = jnp.maximum(jnp.maximum(jnp.max(ste, axis=0, keepdims=True), jnp.max(sto, axis=0, keepdims=True)),
                         jnp.max(stn, axis=0, keepdims=True))
        mt = _finite_or_zero(mt)
        ee = jnp.where(ve, jnp.exp(ste - mt), 0.0)
        eo = jnp.where(vo, jnp.exp(sto - mt), 0.0)
        et = jnp.where(vnt, jnp.exp(stn - mt), 0.0)
        dent = jnp.maximum(jnp.sum(ee, axis=0, keepdims=True) + jnp.sum(eo, axis=0, keepdims=True)
                           + jnp.sum(et, axis=0, keepdims=True), 1e-30)
        hlane = lax.broadcasted_iota(I32, (1, A_HEADS), 1)
        ing = (hlane >= g * A_HPG) & (hlane < (g + 1) * A_HPG)

        def imp_of(e):
            return jnp.sum(jnp.where(ing, e / dent, 0.0), axis=1, keepdims=True)

        imp = jnp.concatenate([imp_of(ee) + imp_of(eo), imp_of(et)], axis=0)
        imp = jnp.broadcast_to(imp, (nrow, LANE))
        j = lax.broadcasted_iota(I32, (nrow, LANE), 0)
        cur = pos // SEL_BLOCK
        nsel = ns + 1
        forced = (j == 0) | (j == cur) | (j == cur - 1)
        score = jnp.where(j * SEL_BLOCK <= pos, imp + jnp.where(forced, FORCE_BONUS, 0.0), -1e9)
        score = jnp.where(j < nsel, score, -3e9)
        rank = _select(score, nsel)
        slot = lax.broadcasted_iota(I32, (nrow, LANE), 1)
        hit = (rank == slot.astype(F32)) & (slot < min(N_SEL, nsel)) & (j < nsel)
        ids = jnp.sum(jnp.where(hit, j, 0), axis=0, keepdims=True)
        idx_ref[g:g + 1, :] = ids
        bpp = PAGE_SIZE // SEL_BLOCK
        pidx = jnp.minimum(ids, PAST_LEN // SEL_BLOCK - 1)
        page = jnp.right_shift(pidx, int(math.log2(bpp)))
        phys = jnp.zeros_like(ids)
        for p in range(pt_ref.shape[1]):
            phys = jnp.where(page == p, pt_ref[:, p:p + 1], phys)
        blk_ref[g:g + 1, :] = phys * bpp + jnp.bitwise_and(pidx, bpp - 1)
    o_ref[...] = o_all


def _cmp_attn_sample(q3, ckv_past, ckv_new, page_table):
    nb = q3.shape[0]
    ncp = ckv_past.shape[3]
    npg = page_table.shape[1]
    ids = pl.BlockSpec((CMP_SB, 8, LANE), lambda b: (b, 0, 0))
    return _call(
        _cmpattn_s_body, (nb // CMP_SB,),
        [pl.BlockSpec((CMP_SB, A_HEADS, A_DH), lambda b: (b, 0, 0)),
         pl.BlockSpec((CMP_SB, 2, A_KV, ncp, A_DH), lambda b: (b, 0, 0, 0, 0)),
         pl.BlockSpec((CMP_SB, 2, A_KV, 8, A_DH), lambda b: (b, 0, 0, 0, 0)),
         pl.BlockSpec((CMP_SB, 1, npg), lambda b: (b, 0, 0))],
        [pl.BlockSpec((CMP_SB, A_HEADS, A_DH), lambda b: (b, 0, 0)), ids, ids],
        [jax.ShapeDtypeStruct((nb, A_HEADS, A_DH), F32), jax.ShapeDtypeStruct((nb, 8, LANE), I32),
         jax.ShapeDtypeStruct((nb, 8, LANE), I32)],
        name="cmp_attn_sample")(q3, ckv_past, ckv_new, page_table.reshape(nb, 1, npg))


KV_ROWS = 2 * A_KV


def _flat_attend(q, xb, kmask, sn, vn):
    s = jnp.where(kmask, _dot_nt(q, xb), -jnp.inf)
    m = _finite_or_zero(jnp.maximum(jnp.max(s, axis=-1, keepdims=True), sn))
    e = jnp.where(kmask, jnp.exp(s - m), 0.0)
    en = jnp.exp(sn - m)
    den = jnp.maximum(jnp.sum(e, axis=-1, keepdims=True) + en, 1e-30)
    return _dot(pltpu.roll(e / den, A_KV, 1).astype(BF), xb) + (en / den) * vn


def _slc_s_body(idx_ref, pt_ref, q_ref, *refs):
    blocks = refs[:A_KV * N_SEL]
    new_ref, o_ref = refs[A_KV * N_SEL:]
    b = pl.program_id(0)
    npast = PAST_LEN // SEL_BLOCK
    nfb = SEL_BLOCK * KV_ROWS
    nk = N_SEL * nfb
    q = q_ref[...]
    new = new_ref[...]
    col = lax.broadcasted_iota(I32, (A_HEADS, nk), 1)
    krow = jnp.bitwise_and(col, KV_ROWS - 1)
    slot = jnp.right_shift(col, int(math.log2(nfb)))
    hrow = lax.broadcasted_iota(I32, (A_HEADS, A_DH), 0)
    o_all = jnp.zeros((A_HEADS, A_DH), F32)
    for g in range(A_KV):
        xb = jnp.concatenate([blocks[g * N_SEL + n][...].astype(BF) for n in range(N_SEL)], axis=0)
        past = jnp.zeros((A_HEADS, nk), I32)
        nfresh = jnp.int32(0)
        for n in range(N_SEL):
            is_past = (idx_ref[(b * A_KV + g) * N_SEL + n] < npast).astype(I32)
            past = jnp.where(slot == n, is_past, past)
            nfresh = nfresh + (1 - is_past)
        fresh = jnp.full((A_HEADS, 1), nfresh, I32) > 0
        sn = jnp.where(fresh, jnp.sum(q.astype(F32) * new[g:g + 1, :], axis=-1, keepdims=True), -jnp.inf)
        o = _flat_attend(q, xb, (past > 0) & (krow == g), sn, new[A_KV + g:A_KV + g + 1, :])
        o_all = jnp.where((hrow >= g * A_HPG) & (hrow < (g + 1) * A_HPG), o, o_all)
    o_ref[...] = o_all


def _slc_attn_sample(layer, q3, cache, idx, blk, slc_new):
    nb = q3.shape[0]

    def blk_spec(g, n):
        return pl.BlockSpec((None, None, SEL_BLOCK * KV_ROWS, A_DH),
                            lambda b, idx_ref, blk_ref: (layer, blk_ref[(b * A_KV + g) * N_SEL + n], 0, 0))

    return _call(
        _slc_s_body, (nb,),
        [pl.BlockSpec((None, A_HEADS, A_DH), lambda b, i, p: (b, 0, 0))]
        + [blk_spec(g, n) for g in range(A_KV) for n in range(N_SEL)]
        + [pl.BlockSpec((None, KV_ROWS, A_DH), lambda b, i, p: (b, 0, 0))],
        pl.BlockSpec((None, A_HEADS, A_DH), lambda b, i, p: (b, 0, 0)),
        jax.ShapeDtypeStruct((nb, A_HEADS, A_DH), F32), nsp=2, name="slc_attn_sample")(
            idx, blk, q3, *([cache] * (A_KV * N_SEL)), slc_new)


WIN_SB = 4


def _win_s_body(q_ref, buf_ref, new_ref, *refs):
    o_ref, nb_ref = refs[-2:]
    pos = PAST_LEN
    nf = buf_ref.shape[1]
    wb = nf // KV_ROWS
    col = lax.broadcasted_iota(I32, (A_HEADS, nf), 1)
    hrow = lax.broadcasted_iota(I32, (A_HEADS, nf), 0)
    kpos = PAST_LEN - wb + jnp.right_shift(col, int(math.log2(KV_ROWS)))
    kmask = ((jnp.bitwise_and(col, KV_ROWS - 1) == jnp.right_shift(hrow, int(math.log2(A_HPG))))
             & (kpos <= pos) & (kpos > pos - WINDOW))
    h1 = lax.broadcasted_iota(I32, (A_HEADS, 1), 0)
    for i in range(WIN_SB):
        q = q_ref[i]
        new = new_ref[i]
        qf = q.astype(F32)
        sn = jnp.zeros((A_HEADS, 1), F32)
        vn = jnp.zeros((A_HEADS, A_DH), F32)
        for g in range(A_KV):
            ing = (h1 >= g * A_HPG) & (h1 < (g + 1) * A_HPG)
            sn = jnp.where(ing, jnp.sum(qf * new[g:g + 1, :], axis=-1, keepdims=True), sn)
            vn = jnp.where(ing, new[A_KV + g:A_KV + g + 1, :], vn)
        o_ref[i] = _flat_attend(q, buf_ref[i].astype(BF), kmask, sn, vn)
        nb_ref[i, pl.ds(0, nf - KV_ROWS), :] = buf_ref[i, pl.ds(KV_ROWS, nf - KV_ROWS), :]
        nb_ref[i, pl.ds(nf - KV_ROWS, KV_ROWS), :] = new


def _win_attn_sample(layer, q3, cache, win_new, acc):
    depth, nb, nf, _ = cache.shape
    ins = [pl.BlockSpec((WIN_SB, A_HEADS, A_DH), lambda b: (b, 0, 0)),
           pl.BlockSpec((None, WIN_SB, nf, A_DH), lambda b: (layer, b, 0, 0)),
           pl.BlockSpec((WIN_SB, KV_ROWS, A_DH), lambda b: (b, 0, 0))]
    args = [q3, cache, win_new]
    if acc is not None:
        ins.append(pl.BlockSpec(memory_space=pl.ANY))
        args.append(acc)
    return _call(
        _win_s_body, (nb // WIN_SB,), ins,
        [pl.BlockSpec((WIN_SB, A_HEADS, A_DH), lambda b: (b, 0, 0)),
         pl.BlockSpec((None, WIN_SB, nf, A_DH), lambda b: (layer, b, 0, 0))],
        [jax.ShapeDtypeStruct((nb, A_HEADS, A_DH), F32), jax.ShapeDtypeStruct(cache.shape, F32)],
        name="win_attn_sample", aliases=None if acc is None else {3: 1})(*args)


def _ssm_disc_body(are_ref, aim_ref, ldt_ref, bre_ref, bim_ref, lre_ref, lim_ref, bbre_ref, bbim_ref):
    a_re, a_im = are_ref[...], aim_ref[...]
    dt = jnp.exp(ldt_ref[...])
    mag = jnp.exp(a_re * dt)
    lam_re = mag * jnp.cos(a_im * dt)
    lam_im = mag * jnp.sin(a_im * dt)
    den = a_re * a_re + a_im * a_im
    co_re = ((lam_re - 1.0) * a_re + lam_im * a_im) / den
    co_im = (lam_im * a_re - (lam_re - 1.0) * a_im) / den
    lre_ref[...] = lam_re
    lim_ref[...] = lam_im
    bbre_ref[...] = co_re * bre_ref[...] - co_im * bim_ref[...]
    bbim_ref[...] = co_re * bim_ref[...] + co_im * bre_ref[...]


def _ssm_disc(a_re, a_im, log_dt, b_re, b_im):
    flat = lambda a: a.reshape(1, S_FLAT)
    bt = lambda b: jnp.transpose(b, (2, 0, 1)).reshape(S_CH, S_FLAT)
    ldt = jnp.broadcast_to(log_dt[:, None], (S_GROUPS, S_STATE))
    v = jax.ShapeDtypeStruct((1, S_FLAT), F32)
    m = jax.ShapeDtypeStruct((S_CH, S_FLAT), F32)
    return pl.pallas_call(_ssm_disc_body, out_shape=[v, v, m, m], name="ssm_disc")(
        flat(a_re), flat(a_im), flat(ldt), bt(b_re), bt(b_im))


def _ssm_tail(hr, hi, u, cre, cim, d, wg, bg):
    y = _dot(hr.astype(BF), cre) - _dot(hi.astype(BF), cim) + d * u
    y = jax.nn.gelu(y)
    return (y * jax.nn.sigmoid(_dot(y.astype(BF), wg) + bg)).astype(BF)


def _ssm_body(u_ref, lre_ref, lim_ref, bbre_ref, bbim_ref, cre_ref, cim_ref, d_ref, wg_ref, bg_ref,
              o_ref, st_ref, xre, xim, hre, him):
    @pl.when(pl.program_id(1) == 0)
    def _():
        st_ref[...] = jnp.zeros_like(st_ref)

    tt = u_ref.shape[0]
    u = u_ref[...]
    ub = u.astype(BF)
    xre[...] = _dot(ub, bbre_ref[...])
    xim[...] = _dot(ub, bbim_ref[...])
    lre, lim = lre_ref[...], lim_ref[...]

    def step(t, c):
        hr, hi = c
        nr = lre * hr - lim * hi + xre[pl.ds(t, 1), :]
        ni = lre * hi + lim * hr + xim[pl.ds(t, 1), :]
        hre[pl.ds(t, 1), :] = nr
        him[pl.ds(t, 1), :] = ni
        return nr, ni

    hr, hi = lax.fori_loop(0, tt, step, (st_ref[0:1, :], st_ref[1:2, :]), unroll=4)
    st_ref[0:1, :] = hr
    st_ref[1:2, :] = hi
    o_ref[...] = _ssm_tail(hre[...], him[...], u, cre_ref[...], cim_ref[...], d_ref[...], wg_ref[...], bg_ref[...])


def _ssm_prompt(nb, t, z, sp):
    tt = 256
    nt = t // tt
    cs = [_full(a.shape) for a in sp]
    return _call(
        _ssm_body, (nb, nt),
        [pl.BlockSpec((tt, S_WIDTH), lambda b, i: (b * nt + i, ZSSM // S_WIDTH))] + cs,
        [pl.BlockSpec((tt, S_WIDTH), lambda b, i: (b * nt + i, 0)), pl.BlockSpec((None, 2, S_FLAT), lambda b, i: (b, 0, 0))],
        [jax.ShapeDtypeStruct((nb * t, S_WIDTH), BF), jax.ShapeDtypeStruct((nb, 2, S_FLAT), F32)],
        scratch=[pltpu.VMEM((tt, S_FLAT), F32)] * 4, name="ssm_prompt")(z, *sp)


def _ssm_s_body(u_ref, h0r_ref, h0i_ref, lre_ref, lim_ref, bbre_ref, bbim_ref, cre_ref, cim_ref, d_ref, wg_ref, bg_ref,
                o_ref, hr_ref, hi_ref):
    u = u_ref[...]
    ub = u.astype(BF)
    lre, lim = lre_ref[...], lim_ref[...]
    h0r, h0i = h0r_ref[...], h0i_ref[...]
    hr = _dot(ub, bbre_ref[...]) + (lre * h0r - lim * h0i)
    hi = _dot(ub, bbim_ref[...]) + (lre * h0i + lim * h0r)
    hr_ref[...] = hr
    hi_ref[...] = hi
    o_ref[...] = _ssm_tail(hr, hi, u, cre_ref[...], cim_ref[...], d_ref[...], wg_ref[...], bg_ref[...])


def _ssm_sample(z, h0r, h0i, sp):
    n = z.shape[0]
    st = pl.BlockSpec((n, S_FLAT), lambda i: (0, 0))
    return _call(
        _ssm_s_body, (1,),
        [pl.BlockSpec((n, S_WIDTH), lambda i: (0, ZSSM // S_WIDTH)), st, st] + [_full(a.shape) for a in sp],
        [pl.BlockSpec((n, S_WIDTH), lambda i: (0, 0)), st, st],
        [jax.ShapeDtypeStruct((n, S_WIDTH), BF), jax.ShapeDtypeStruct((n, S_FLAT), F32), jax.ShapeDtypeStruct((n, S_FLAT), F32)],
        name="ssm_sample")(z, h0r, h0i, *sp)


def _rwkv_prep_body(tiles_per_seq, *refs):
    z_refs, refs = refs[:5], refs[5:]
    if tiles_per_seq is None:
        p_refs, refs = refs[:5], refs[5:]
    m_refs, refs = refs[:5], refs[5:]
    w0_ref, w2_ref, a0_ref, a2_ref, g2_ref, kkw_ref, kaw_ref, rk_ref, hs_ref = refs[:9]
    r_ref, w_ref, k_ref, kk_ref, ka_ref, v_ref, g_ref, bonus_ref = refs[9:17]
    carry_refs = refs[17:]

    def mix(i):
        z = z_refs[i][...]
        if tiles_per_seq is None:
            prev = p_refs[i][...]
        else:
            first = jnp.where(pl.program_id(0) % tiles_per_seq == 0, 0.0, carry_refs[i][...])
            prev = jnp.where(lax.broadcasted_iota(I32, z.shape, 0) == 0, first, pltpu.roll(z, 1, 0))
            carry_refs[i][...] = z[z.shape[0] - 1:, :]
        return z + (prev - z) * m_refs[i][...]

    r, k, v = mix(0), mix(1), mix(2)
    wa = mix(3)
    gl = mix(4)
    w = -jax.nn.softplus(-(w0_ref[...] + _dot(jnp.tanh(wa).astype(BF), w2_ref[...]))) - 0.5
    a = jax.nn.sigmoid(a0_ref[...] + _dot(wa.astype(BF), a2_ref[...]))
    hs = hs_ref[...]
    kk = k * kkw_ref[...]
    kk = kk / jnp.maximum(jnp.sqrt(_dot(kk * kk, hs, HI)), 1e-12)
    k = k * (1.0 + (a - 1.0) * kaw_ref[...])
    r_ref[...] = r
    w_ref[...] = jnp.exp(-jnp.exp(w))
    k_ref[...] = k
    kk_ref[...] = kk
    ka_ref[...] = kk * a
    v_ref[...] = v
    g_ref[...] = _dot(jax.nn.sigmoid(gl).astype(BF), g2_ref[...])
    bonus_ref[...] = _dot(r * k * rk_ref[...], hs, HI) * v


def _rwkv_prep(n, tm, z, prev, mu, rp, seq=None):
    def spec(rows, width, off):
        assert off % width == 0
        return pl.BlockSpec((rows, width), (lambda i: (i, off // width)) if rows == tm else (lambda i: (0, off // width)))

    w = R_WIDTH
    offs = [(w, 0), (w, w), (w, 2 * w), (LANE, 3 * w), (LANE, 3 * w + LANE)]
    out = pl.BlockSpec((tm, w), lambda i: (i, 0))
    given = prev is not None
    return _call(
        functools.partial(_rwkv_prep_body, None if given else seq // tm), (n // tm,),
        [spec(tm, wd, ZRW + o) for wd, o in offs] + ([spec(tm, wd, o) for wd, o in offs] if given else [])
        + [spec(1, wd, o) for wd, o in offs] + [_full(a.shape) for a in rp],
        [out] * 8, [jax.ShapeDtypeStruct((n, w), F32)] * 8,
        scratch=[] if given else [pltpu.VMEM((1, wd), F32) for wd, _ in offs], name="rwkv_prep")(
            *([z] * 5), *([prev] * 5 if given else []), *([mu] * 5), *rp)


def _rwkv_scan_body(r_ref, w_ref, k_ref, kk_ref, ka_ref, v_ref, g_ref, bonus_ref, lng_ref, lnb_ref, ha_ref,
                    o_ref, st_ref, rs, ws, ks, kks, kas, vs, os_, obuf, *bufs):
    @pl.when(pl.program_id(1) == 0)
    def _():
        st_ref[...] = jnp.zeros_like(st_ref)

    tt = r_ref.shape[0]
    n = R_DH
    lc = RWKV_CHUNK
    for h in range(R_HEADS):
        sl = slice(h * n, (h + 1) * n)
        for src, dst in ((r_ref, rs), (w_ref, ws), (k_ref, ks), (kk_ref, kks), (ka_ref, kas), (v_ref, vs)):
            dst[h] = src[:, sl]
    eye = (lax.broadcasted_iota(I32, (n, n), 0) == lax.broadcasted_iota(I32, (n, n), 1)).astype(F32)
    m0 = jnp.concatenate([eye, jnp.zeros((lc, n), F32)], axis=0)
    xrow = lax.broadcasted_iota(I32, (2 * lc, lc), 0)
    xcol = lax.broadcasted_iota(I32, (2 * lc, lc), 1)
    xsign = jnp.where(xrow < lc, -1.0, 1.0)
    xkeep = jnp.where(xrow < lc, xrow, xrow - lc) <= xcol
    heads = range(R_HEADS)

    def rows_part(ci, buf):
        ms_ref, cs_ref, gam_ref = buf
        t0 = pl.multiple_of(ci * lc, lc)
        gs = [jnp.ones((1, n), F32)] * R_HEADS
        for h in heads:
            ms_ref[h] = m0
        for s in range(lc):
            t = t0 + s
            for h in heads:
                w = ws[h, pl.ds(t, 1), :]
                c = jnp.sum(ms_ref[h] * kks[h, pl.ds(t, 1), :], axis=1, keepdims=True)
                cs_ref[h, :, s:s + 1] = c
                ms_ref[h] = ms_ref[h] * w - c * kas[h, pl.ds(t, 1), :]
                ms_ref[h, n + s:n + s + 1, :] = ks[h, pl.ds(t, 1), :]
                gs[h] = gs[h] * w
                gam_ref[h, s:s + 1, :] = gs[h]

    def state_part(ci, buf):
        ms_ref, cs_ref, gam_ref = buf
        sl16 = pl.ds(pl.multiple_of(ci * lc, lc), lc)
        part = []
        for h in heads:
            g = gam_ref[h]
            ginv = 1.0 / g
            rt = g * rs[h, sl16, :]
            v16 = vs[h, sl16, :]
            st0 = st_ref[h]
            stack = jnp.concatenate([st0, v16], axis=0)
            x = _dot_nt(jnp.concatenate([kas[h, sl16, :] * ginv, ks[h, sl16, :] * ginv], axis=0), rt, HI)
            x = jnp.where(xkeep, x * xsign, 0.0)
            cst = _dot_tn(cs_ref[h], stack, HI)
            part.append((x, cst, v16, _dot(rt, st0, HI)))
            st_ref[h] = _dot_tn(ms_ref[h], stack, HI)
        for h in heads:
            x, cst, v16, o0 = part[h]
            os_[h, sl16, :] = o0 + _dot_tn(x, jnp.concatenate([cst, v16], axis=0), HI)

    nch = tt // lc
    buf_a, buf_b = bufs[:3], bufs[3:]
    rows_part(0, buf_a)

    def pair(pi, carry):
        c0 = 2 * pi
        rows_part(c0 + 1, buf_b)
        state_part(c0, buf_a)
        rows_part(jnp.minimum(c0 + 2, nch - 1), buf_a)
        state_part(c0 + 1, buf_b)
        return carry

    lax.fori_loop(0, nch // 2, pair, 0)
    for h in range(R_HEADS):
        obuf[:, h * n:(h + 1) * n] = os_[h]
    _rwkv_post_body(obuf, g_ref, bonus_ref, lng_ref, lnb_ref, ha_ref, o_ref)


def _rwkv_scan(nb, t, r, w, k, kk, ka, v, g, bonus, ln_g, ln_b, havg):
    tt = 256
    nt = t // tt
    row = pl.BlockSpec((tt, R_WIDTH), lambda b, i: (b * nt + i, 0))
    vec = _full((1, R_WIDTH))
    return _call(
        _rwkv_scan_body, (nb, nt), [row] * 8 + [vec, vec, _full(havg.shape)],
        [row, pl.BlockSpec((None, R_HEADS, R_DH, R_DH), lambda b, i: (b, 0, 0, 0))],
        [jax.ShapeDtypeStruct((nb * t, R_WIDTH), BF), jax.ShapeDtypeStruct((nb, R_HEADS, R_DH, R_DH), F32)],
        scratch=[pltpu.VMEM((R_HEADS, tt, R_DH), F32)] * 7 + [pltpu.VMEM((tt, R_WIDTH), F32)]
        + [pltpu.VMEM((R_HEADS, R_DH + RWKV_CHUNK, R_DH), F32), pltpu.VMEM((R_HEADS, R_DH + RWKV_CHUNK, RWKV_CHUNK), F32),
           pltpu.VMEM((R_HEADS, RWKV_CHUNK, R_DH), F32)] * 2,
        name="rwkv_scan")(r, w, k, kk, ka, v, g, bonus, ln_g, ln_b, havg)


RWKV_SB = 8
RWKV_HG = 4


def _rwkv_step_body(r_ref, w_ref, k_ref, kk_ref, ka_ref, vt_ref, s_ref, ot_ref, so_ref):
    n = R_DH
    lane = lax.broadcasted_iota(I32, (n, RWKV_SB), 1)
    samples = range(RWKV_SB)
    for h0 in range(0, R_HEADS, RWKV_HG):
        heads = range(h0, h0 + RWKV_HG)
        rem = {(h, s): jnp.sum(s_ref[s, h] * kk_ref[h, s:s + 1, :], axis=1, keepdims=True) for h in heads for s in samples}
        outs = {}
        for h in heads:
            for s in samples:
                st = (s_ref[s, h] * w_ref[h, s:s + 1, :] - rem[h, s] * ka_ref[h, s:s + 1, :]
                      + vt_ref[h * n:(h + 1) * n, s:s + 1] * k_ref[h, s:s + 1, :])
                so_ref[s, h] = st
                outs[h, s] = jnp.sum(st * r_ref[h, s:s + 1, :], axis=1, keepdims=True)
        for h in heads:
            ot = jnp.zeros((n, RWKV_SB), F32)
            for s in samples:
                ot = jnp.where(lane == s, outs[h, s], ot)
            ot_ref[h * n:(h + 1) * n, :] = ot


def _rwkv_step(r, w, k, kk, ka, v, state, layer):
    nb = r.shape[0]
    nblk = nb // RWKV_SB
    vt = jnp.transpose(v.reshape(nblk, RWKV_SB, R_WIDTH), (0, 2, 1))
    r, w, k, kk, ka = (jnp.transpose(a.reshape(nblk, RWKV_SB, R_HEADS, R_DH), (0, 2, 1, 3)) for a in (r, w, k, kk, ka))
    row = pl.BlockSpec((None, R_HEADS, RWKV_SB, R_DH), lambda i: (i, 0, 0, 0))
    col = pl.BlockSpec((None, R_WIDTH, RWKV_SB), lambda i: (i, 0, 0))
    sts = pl.BlockSpec((RWKV_SB, R_HEADS, R_DH, R_DH), lambda i: (i, 0, 0, 0))
    sti = pl.BlockSpec((None, RWKV_SB, R_HEADS, R_DH, R_DH), lambda i: (layer, i, 0, 0, 0))
    ot, so = _call(
        _rwkv_step_body, (nblk,), [row] * 5 + [col, sti], [col, sts],
        [jax.ShapeDtypeStruct((nblk, R_WIDTH, RWKV_SB), F32), jax.ShapeDtypeStruct(state.shape[1:], F32)],
        name="rwkv_step")(r, w, k, kk, ka, vt, state)
    return jnp.transpose(ot, (0, 2, 1)).reshape(nb, R_WIDTH), so


def _rwkv_post_body(o_ref, g_ref, bonus_ref, lng_ref, lnb_ref, ha_ref, out_ref):
    o = o_ref[...]
    ha = ha_ref[...]
    cen = o - _dot(o, ha, HI)
    var = _dot(cen * cen, ha, HI)
    y = cen * lax.rsqrt(var + GN_EPS) * lng_ref[...] + lnb_ref[...]
    out_ref[...] = ((y + bonus_ref[...]) * g_ref[...]).astype(BF)


def _rwkv_post(n, tm, o, g, bonus, ln_g, ln_b, havg):
    row = pl.BlockSpec((tm, R_WIDTH), lambda i: (i, 0))
    vec = _full((1, R_WIDTH))
    return _call(_rwkv_post_body, (n // tm,), [row, row, row, vec, vec, _full(havg.shape)], row,
                 jax.ShapeDtypeStruct((n, R_WIDTH), BF), name="rwkv_post")(o, g, bonus, ln_g, ln_b, havg)


def _block_diag(blocks):
    g, a, b = blocks.shape
    return jnp.einsum('gab,gh->gahb', blocks, jnp.eye(g, dtype=blocks.dtype)).reshape(g * a, g * b)


class _LayerParams(NamedTuple):
    norm_g: jax.Array
    ffn: tuple
    w_in: tuple
    w_out: jax.Array
    q_gain: jax.Array
    k_gain: jax.Array
    cmp: tuple
    ssm: tuple
    rwkv_mu: jax.Array
    rwkv: tuple
    rwkv_ln: tuple


def _layer_params(l, p):
    w_in_r = (p['w_in_bf16'], l)
    ffn = p['ffn_bf16'] + (l,)
    cmp = (p['nsa_cmp_pe'][l], p['nsa_cmp_w1'][l].astype(BF), p['nsa_cmp_w2'][l].astype(BF), p['nsa_k_gain'][l, 0:1])
    lre, lim, bbre, bbim = _ssm_disc(p['ssm_a_re'][l], p['ssm_a_im'][l], p['ssm_log_dt'][l], p['ssm_b_re'][l], p['ssm_b_im'][l])
    to_gcp = lambda m: jnp.transpose(m.reshape(S_CH, S_GROUPS, S_STATE), (1, 0, 2))
    ssm = (lre, lim, _block_diag(to_gcp(bbre)).astype(BF), _block_diag(to_gcp(bbim)).astype(BF),
           _block_diag(jnp.transpose(p['ssm_c_re'][l], (0, 2, 1))).astype(BF),
           _block_diag(jnp.transpose(p['ssm_c_im'][l], (0, 2, 1))).astype(BF),
           p['ssm_d'][l].reshape(1, S_WIDTH), p['ssm_w_glu'][l].astype(BF), p['ssm_b_glu'][l].reshape(1, S_WIDTH))
    zl = jnp.zeros((R_DECAY_LORA, R_WIDTH), F32)
    vec = lambda a: a.reshape(1, R_WIDTH)
    hsum = _block_diag(jnp.ones((R_HEADS, R_DH, R_DH), F32))
    rwkv = (vec(p['rwkv_w0'][l]), jnp.concatenate([p['rwkv_w2'][l], zl], axis=0).astype(BF),
            vec(p['rwkv_a0'][l]), jnp.concatenate([zl, p['rwkv_a2'][l]], axis=0).astype(BF), p['rwkv_g2'][l].astype(BF),
            vec(p['rwkv_k_k'][l]), vec(p['rwkv_k_a'][l]), vec(p['rwkv_r_k'][l]), hsum)
    return _LayerParams(p['norm_g'][l], ffn, w_in_r, p['w_out'][l].astype(BF), p['nsa_q_gain'][l], p['nsa_k_gain'][l], cmp,
                        ssm, p['rwkv_mu'][l].reshape(1, R_IN), rwkv,
                        (vec(p['rwkv_ln_g'][l]), vec(p['rwkv_ln_b'][l]), hsum / R_DH))


def _layer_prompt(rows, nb, t, x, lp, tabs):
    n = rows.n
    h = _ffn(rows, x, lp.norm_g[0:1], 0, *lp.ffn, 0)
    z = _projin(rows, h, lp.norm_g[1:2], 3, *lp.w_in)
    qn, qr, slc_rows, win_rows, gates = _nsa_prep(n, 256, z, tabs, t // 256, lp.q_gain, lp.k_gain)
    ncb = t // CMP_BLOCK
    cmp_rows = z[:, ZCMP:ZSLC]
    xcmp = jnp.transpose(cmp_rows.reshape(nb, ncb, CMP_BLOCK, 2, A_KV, A_DH), (3, 0, 4, 1, 2, 5))
    ckv = _compress(xcmp.reshape(2, nb * A_KV * ncb, CMP_BLOCK * A_DH), *lp.cmp).reshape(2, nb, A_KV, ncb, A_DH)
    o_cmp, sel = _cmp_attn(nb, t, qn, ckv[0], ckv[1])
    o_nsa = _slc_win_attn(nb, t, qr, slc_rows, win_rows, sel, gates, o_cmp)
    o_ssm, ssm_st = _ssm_prompt(nb, t, z, lp.ssm)
    r, w, k, kk, ka, v, g, bonus = _rwkv_prep(n, 256, z, None, lp.rwkv_mu, lp.rwkv, seq=t)
    o_rwkv, wkv_t = _rwkv_scan(nb, t, r, w, k, kk, ka, v, g, bonus, *lp.rwkv_ln)
    h = _projout(rows, h, o_nsa, o_ssm, o_rwkv, lp.w_out)
    y = _ffn(rows, h, lp.norm_g[2:3], 6, *lp.ffn, 1)
    keep = min(WINDOW, t)
    shape6 = lambda a: a.reshape(nb, -1, 2, A_KV, A_DH)
    state = (shape6(cmp_rows), shape6(slc_rows), shape6(win_rows)[:, t - keep:],
             jnp.stack([ssm_st[:, 0], ssm_st[:, 1]], axis=-1).reshape(nb, S_GROUPS, S_STATE, 2),
             z.reshape(nb, t, ZW)[:, -1, ZRW:ZGATE], jnp.swapaxes(wkv_t, -1, -2))
    return y, state


def _layer_sample(rows, layer, x, lp, tabs, cache_cmp, cache_slc, cache_win, win_acc, page_table, ssm0, shift0, wkv_all):
    n = rows.n
    h = _ffn(rows, x, lp.norm_g[0:1], 0, *lp.ffn, 0)
    z = _projin(rows, h, lp.norm_g[1:2], 3, *lp.w_in)
    qn, qr, slc_new, win_new, gates = _nsa_prep(n, n, z, tabs, 1, lp.q_gain, lp.k_gain)
    cmp_new = z[:, ZCMP:ZSLC]
    ckv_past = _compress_past(layer, cache_cmp, page_table, *lp.cmp)
    xnew = jnp.transpose(cmp_new.reshape(n, 2, A_KV, A_DH), (1, 0, 2, 3)).reshape(2, n * A_KV, A_DH)
    xnew = jnp.pad(xnew, ((0, 0), (0, 0), (0, (CMP_BLOCK - 1) * A_DH)))
    ckv_new = _compress(xnew, *lp.cmp).reshape(2, n, A_KV, 1, A_DH)
    ckv_new = jnp.pad(jnp.transpose(ckv_new, (1, 0, 2, 3, 4)), ((0, 0), (0, 0), (0, 0), (0, 7), (0, 0)))
    q3n = qn.reshape(n, A_HEADS, A_DH)
    q3r = qr.reshape(n, A_HEADS, A_DH)
    o_cmp, idx, blk = _cmp_attn_sample(q3n, ckv_past, ckv_new, page_table)
    ids_flat = lambda a: a[:, :A_KV, :N_SEL].reshape(-1)
    o_slc = _slc_attn_sample(layer, q3r, cache_slc, ids_flat(idx), ids_flat(blk), slc_new.reshape(n, KV_ROWS, A_DH))
    o_win, win_acc = _win_attn_sample(layer, q3r, cache_win, win_new.reshape(n, KV_ROWS, A_DH), win_acc)
    flat = lambda a: a.reshape(n, A_WIDTH)
    o_nsa = _nsa_combine(n, n, gates, flat(o_cmp), flat(o_slc), flat(o_win))
    o_ssm, hr, hi = _ssm_sample(z, ssm0[..., 0].reshape(n, S_FLAT), ssm0[..., 1].reshape(n, S_FLAT), lp.ssm)
    r, w, k, kk, ka, v, g, bonus = _rwkv_prep(n, n, z, shift0, lp.rwkv_mu, lp.rwkv)
    o_step, wkv = _rwkv_step(r, w, k, kk, ka, v, wkv_all, layer)
    o_rwkv = _rwkv_post(n, n, o_step, g, bonus, *lp.rwkv_ln)
    h = _projout(rows, h, o_nsa, o_ssm, o_rwkv, lp.w_out)
    y = _ffn(rows, h, lp.norm_g[2:3], 6, *lp.ffn, 1)
    shape6 = lambda a: a.reshape(n, 1, 2, A_KV, A_DH)
    state = (shape6(cmp_new), shape6(slc_new), None,
             jnp.stack([hr, hi], axis=-1).reshape(n, S_GROUPS, S_STATE, 2), z[:, ZRW:ZGATE], wkv)
    return y, state, win_acc


def kernel(x_prompt, x_sample, cache_nsa_cmp, cache_nsa_slc, cache_nsa_win, state_ssm, state_rwkv_shift, state_rwkv_wkv, page_table, c_prompt, c_sample, norm_g, w_ada, b_ada, ffn_w1, ffn_w3, ffn_w2, w_in, w_out, nsa_q_gain, nsa_k_gain, nsa_cmp_pe, nsa_cmp_w1, nsa_cmp_w2, ssm_a_re, ssm_a_im, ssm_log_dt, ssm_b_re, ssm_b_im, ssm_c_re, ssm_c_im, ssm_d, ssm_w_glu, ssm_b_glu, rwkv_mu, rwkv_w0, rwkv_w2, rwkv_a0, rwkv_a2, rwkv_g2, rwkv_k_k, rwkv_k_a, rwkv_r_k, rwkv_ln_g, rwkv_ln_b):
    p = dict(norm_g=norm_g, ffn_w1=ffn_w1, ffn_w3=ffn_w3, ffn_w2=ffn_w2, w_in=w_in, w_out=w_out, nsa_q_gain=nsa_q_gain,
             nsa_k_gain=nsa_k_gain, nsa_cmp_pe=nsa_cmp_pe, nsa_cmp_w1=nsa_cmp_w1, nsa_cmp_w2=nsa_cmp_w2, ssm_a_re=ssm_a_re,
             ssm_a_im=ssm_a_im, ssm_log_dt=ssm_log_dt, ssm_b_re=ssm_b_re, ssm_b_im=ssm_b_im, ssm_c_re=ssm_c_re,
             ssm_c_im=ssm_c_im, ssm_d=ssm_d, ssm_w_glu=ssm_w_glu, ssm_b_glu=ssm_b_glu, rwkv_mu=rwkv_mu, rwkv_w0=rwkv_w0,
             rwkv_w2=rwkv_w2, rwkv_a0=rwkv_a0, rwkv_a2=rwkv_a2, rwkv_g2=rwkv_g2, rwkv_k_k=rwkv_k_k, rwkv_k_a=rwkv_k_a,
             rwkv_r_k=rwkv_r_k, rwkv_ln_g=rwkv_ln_g, rwkv_ln_b=rwkv_ln_b)
    p['ffn_bf16'] = (ffn_w1.astype(BF), ffn_w3.astype(BF), ffn_w2.astype(BF))
    p['w_in_bf16'] = jnp.concatenate(
        [w_in[:, :, :OFF_GATE], w_in[:, :, OFF_SSM:], w_in[:, :, OFF_GATE:OFF_SSM],
         jnp.zeros(w_in.shape[:2] + (ZW - ZGATE - 3 * A_HEADS,), F32)], axis=2).astype(BF)
    depth = w_in.shape[0]
    nbp, t, d = x_prompt.shape
    nbs, ts, _ = x_sample.shape
    assert ts == 1 and nbp <= 8 and d == D_MODEL
    cp8 = jnp.pad(c_prompt, ((0, 8 - nbp), (0, 0)))
    mod_p, mod_s = _ada(cp8, c_sample, w_ada, b_ada)
    tabs_p = _rope_tables(jnp.arange(t))
    tabs_s = _rope_tables(jnp.full((nbs,), PAST_LEN))
    hp = x_prompt.reshape(nbp * t, d)
    hs = x_sample.reshape(nbs, d)
    st_p, st_s = [], []
    cmp_flat = cache_nsa_cmp.reshape(cache_nsa_cmp.shape[:2] + (PAGE_SIZE * KV_ROWS, A_DH))
    slc_flat = cache_nsa_slc.reshape(cache_nsa_slc.shape[0], -1, SEL_BLOCK * KV_ROWS, A_DH)
    win_flat = cache_nsa_win.reshape(cache_nsa_win.shape[:2] + (cache_nsa_win.shape[2] * KV_ROWS, A_DH))
    win_acc = None
    for l in range(depth):
        lp = _layer_params(l, p)
        rows_p = _Rows(nbp * t, t, mod_p[l].reshape(8, 1, 9 * d), False, 512)
        rows_s = _Rows(nbs, 1, mod_s[l], True, nbs)
        hp, sp = _layer_prompt(rows_p, nbp, t, hp, lp, tabs_p)
        hs, ss, win_acc = _layer_sample(rows_s, l, hs, lp, tabs_s, cmp_flat, slc_flat, win_flat, win_acc, page_table,
                                        state_ssm[l], state_rwkv_shift[l], state_rwkv_wkv)
        st_p.append(sp)
        st_s.append(ss)
    outs = [hp.reshape(nbp, t, d), hs.reshape(nbs, 1, d)]
    for i in range(6):
        outs.append(jnp.stack([s[i] for s in st_p]))
        outs.append(win_acc.reshape(cache_nsa_win.shape) if i == 2 else jnp.stack([s[i] for s in st_s]))
    return tuple(outs)
```

```python
import functools
import math
from typing import NamedTuple

import jax
import jax.numpy as jnp
from jax import lax
from jax.experimental import pallas as pl
from jax.experimental.pallas import tpu as pltpu

F32 = jnp.float32
BF = jnp.bfloat16
I32 = jnp.int32
HI = lax.Precision.HIGHEST

D_MODEL = 2048
PAST_LEN = 2048
PAGE_SIZE = 128
A_HEADS, A_KV, A_HPG, A_DH = 8, 2, 4, 128
A_WIDTH = A_HEADS * A_DH
A_KVW = A_KV * A_DH
CMP_BLOCK, SEL_BLOCK, N_SEL, WINDOW = 32, 64, 16, 512
ROT_DIM = A_DH // 4
ROPE_THETA = 500000.0
ATT_SCALE = A_DH ** -0.5
FORCE_BONUS = 1.0e4
S_GROUPS, S_CH, S_STATE = 32, 16, 64
S_WIDTH = S_GROUPS * S_CH
S_FLAT = S_GROUPS * S_STATE
R_HEADS, R_DH = 8, 64
R_WIDTH = R_HEADS * R_DH
R_DECAY_LORA, R_A_LORA, R_GATE_LORA = 64, 64, 128
R_IN = 3 * R_WIDTH + R_DECAY_LORA + R_A_LORA + R_GATE_LORA
GN_EPS = 64e-5
OFF_KV = A_WIDTH
OFF_GATE = OFF_KV + 6 * A_KVW
OFF_SSM = OFF_GATE + 3 * A_HEADS
OFF_RWKV = OFF_SSM + S_WIDTH

ZQ, ZCMP, ZSLC, ZWIN, ZSSM, ZRW = 0, 1024, 1536, 2048, 2560, 3072
ZGATE = ZRW + R_IN
ZW = ZGATE + 128
LANE = 128
RWKV_CHUNK = 16


def _dot(a, b, precision=None):
    return jnp.dot(a, b, preferred_element_type=F32, precision=precision)


def _dot_nt(a, b, precision=None):
    return lax.dot_general(a, b, (((1,), (1,)), ((), ())), preferred_element_type=F32, precision=precision)


def _dot_tn(a, b, precision=None):
    return lax.dot_general(a, b, (((0,), (0,)), ((), ())), preferred_element_type=F32, precision=precision)


def _call(body, grid, in_specs, out_specs, out_shape, scratch=(), nsp=0, name=None, aliases=None):
    gs = pltpu.PrefetchScalarGridSpec(num_scalar_prefetch=nsp, grid=grid, in_specs=in_specs, out_specs=out_specs,
                                      scratch_shapes=list(scratch))
    return pl.pallas_call(body, grid_spec=gs, out_shape=out_shape, name=name, input_output_aliases=aliases or {},
                          compiler_params=pltpu.CompilerParams(dimension_semantics=("arbitrary",) * len(grid)))


def _rms(x, g):
    return x * lax.rsqrt(jnp.mean(x * x, axis=-1, keepdims=True) + 1e-6) * g


def _finite_or_zero(m):
    return jnp.where(jnp.abs(m) < jnp.inf, m, 0.0)


def _masked_exp(s, mask, axis):
    s = jnp.where(mask, s, -jnp.inf)
    m = _finite_or_zero(jnp.max(s, axis=axis, keepdims=True))
    e = jnp.exp(s - m)
    return e, 1.0 / jnp.maximum(jnp.sum(e, axis=axis, keepdims=True), 1e-30)


def _masked_softmax(s, mask, axis):
    e, r = _masked_exp(s, mask, axis)
    return e * r


class _Rows(NamedTuple):
    n: int
    seq: int
    mod: jax.Array
    per_row: bool
    tm: int


def _mod_spec(rows, tm, col, width=D_MODEL, jdep=False):
    per = D_MODEL // width
    if rows.per_row:
        return pl.BlockSpec((tm, width), lambda i, j: (i, col * per + (j if jdep else 0)))
    tpb = rows.seq // tm
    return pl.BlockSpec((None, 1, width), lambda i, j: (i // tpb, 0, col * per + (j if jdep else 0)))


def _zspec(tm, width, off):
    assert off % width == 0
    return pl.BlockSpec((tm, width), lambda i, j: (i, off // width))


def _full(shape):
    nd = len(shape)
    return pl.BlockSpec(shape, lambda *a: (0,) * nd)


def _ada_body(cp_ref, cs_ref, w_ref, b_ref, op_ref, os_ref):
    w = w_ref[...].astype(BF)
    b = b_ref[...]
    for c_ref, o_ref in ((cp_ref, op_ref), (cs_ref, os_ref)):
        c = c_ref[...]
        o_ref[...] = _dot((c * jax.nn.sigmoid(c)).astype(BF), w) + b


def _ada(cp8, cs, w_ada, b_ada):
    nl, d, n = w_ada.shape
    tn = 1024
    ns = cs.shape[0]
    return _call(
        _ada_body, (nl, n // tn),
        [_full((8, d)), _full((ns, d)),
         pl.BlockSpec((None, d, tn), lambda l, j: (l, 0, j)),
         pl.BlockSpec((None, 1, tn), lambda l, j: (l, 0, j))],
        [pl.BlockSpec((None, 8, tn), lambda l, j: (l, 0, j)),
         pl.BlockSpec((None, ns, tn), lambda l, j: (l, 0, j))],
        [jax.ShapeDtypeStruct((nl, 8, n), F32), jax.ShapeDtypeStruct((nl, ns, n), F32)],
        name="ada")(cp8, cs, w_ada, b_ada.reshape(nl, 1, n))


def _norm_mod(x, g, scale, shift):
    return _rms(x, g) * (1.0 + scale) + shift


def _ffn_body(x_ref, g_ref, sh_ref, sc_ref, gt_ref, w1_ref, w3_ref, w2_ref, o_ref, xn_ref):
    f = pl.program_id(1)

    @pl.when(f == 0)
    def _():
        xn_ref[...] = _norm_mod(x_ref[...], g_ref[...], sc_ref[...], sh_ref[...]).astype(BF)
        o_ref[...] = jnp.zeros_like(o_ref)

    xn = xn_ref[...]
    h1 = _dot(xn, w1_ref[...])
    h3 = _dot(xn, w3_ref[...])
    o_ref[...] += _dot((h1 * jax.nn.sigmoid(h1) * h3).astype(BF), w2_ref[...])

    @pl.when(f == pl.num_programs(1) - 1)
    def _():
        o_ref[...] = x_ref[...] + 0.5 * gt_ref[...] * o_ref[...]


def _ffn(rows, x, g, col0, w1, w3, w2, layer, which):
    d, ff = w1.shape[2:]
    tm = rows.tm
    tf = 512 if tm <= 512 else 256
    return _call(
        _ffn_body, (rows.n // tm, ff // tf),
        [pl.BlockSpec((tm, d), lambda i, f: (i, 0)), _full((1, d)),
         _mod_spec(rows, tm, col0), _mod_spec(rows, tm, col0 + 1), _mod_spec(rows, tm, col0 + 2),
         pl.BlockSpec((None, None, d, tf), lambda i, f: (layer, which, 0, f)),
         pl.BlockSpec((None, None, d, tf), lambda i, f: (layer, which, 0, f)),
         pl.BlockSpec((None, None, tf, d), lambda i, f: (layer, which, f, 0))],
        pl.BlockSpec((tm, d), lambda i, f: (i, 0)),
        jax.ShapeDtypeStruct((rows.n, d), F32),
        scratch=[pltpu.VMEM((tm, d), BF)], name="ffn")(x, g, rows.mod, rows.mod, rows.mod, w1, w3, w2)


def _projin_body(x_ref, g_ref, sh_ref, sc_ref, w_ref, o_ref, xn_ref):
    @pl.when(pl.program_id(1) == 0)
    def _():
        xn_ref[...] = _norm_mod(x_ref[...], g_ref[...], sc_ref[...], sh_ref[...]).astype(BF)

    o_ref[...] = _dot(xn_ref[...], w_ref[...])


def _projin(rows, x, g, col0, w, layer):
    d, n = w.shape[1:]
    tm, tn = rows.tm, 1664
    return _call(
        _projin_body, (rows.n // tm, n // tn),
        [pl.BlockSpec((tm, d), lambda i, j: (i, 0)), _full((1, d)),
         _mod_spec(rows, tm, col0), _mod_spec(rows, tm, col0 + 1),
         pl.BlockSpec((None, d, tn), lambda i, j: (layer, 0, j))],
        pl.BlockSpec((tm, tn), lambda i, j: (i, j)),
        jax.ShapeDtypeStruct((rows.n, n), F32),
        scratch=[pltpu.VMEM((tm, d), BF)], name="proj_in")(x, g, rows.mod, rows.mod, w)


def _projout_body(h_ref, gt_ref, a_ref, b_ref, c_ref, wa_ref, wb_ref, wc_ref, o_ref):
    mix = _dot(a_ref[...], wa_ref[...]) + _dot(b_ref[...], wb_ref[...]) + _dot(c_ref[...], wc_ref[...])
    o_ref[...] = h_ref[...] + gt_ref[...] * mix


def _projout(rows, h, o_nsa, o_ssm, o_rwkv, w):
    d = h.shape[1]
    tm, tn = rows.tm, d
    nblk = A_WIDTH // S_WIDTH
    return _call(
        _projout_body, (rows.n // tm, d // tn),
        [pl.BlockSpec((tm, tn), lambda i, j: (i, j)), _mod_spec(rows, tm, 5, tn, True),
         pl.BlockSpec((tm, A_WIDTH), lambda i, j: (i, 0)), pl.BlockSpec((tm, S_WIDTH), lambda i, j: (i, 0)),
         pl.BlockSpec((tm, R_WIDTH), lambda i, j: (i, 0)),
         pl.BlockSpec((A_WIDTH, tn), lambda i, j: (0, j)), pl.BlockSpec((S_WIDTH, tn), lambda i, j: (nblk, j)),
         pl.BlockSpec((R_WIDTH, tn), lambda i, j: (nblk + 1, j))],
        pl.BlockSpec((tm, tn), lambda i, j: (i, j)),
        jax.ShapeDtypeStruct((rows.n, d), F32), name="proj_out")(h, rows.mod, o_nsa, o_ssm, o_rwkv, w, w, w)


def _rope_tables(pos):
    half = ROT_DIM // 2
    inv = ROPE_THETA ** (-2.0 * jnp.arange(half, dtype=F32) / ROT_DIM)
    ang = pos.astype(F32)[:, None] * inv[None, :]
    cos, sin = jnp.cos(ang), jnp.sin(ang)
    n = pos.shape[0]
    ct = jnp.concatenate([cos, cos, jnp.ones((n, A_DH - ROT_DIM), F32)], axis=1)
    sa = jnp.concatenate([-sin, jnp.zeros((n, A_DH - half), F32)], axis=1)
    sb = jnp.concatenate([jnp.zeros((n, half), F32), sin, jnp.zeros((n, A_DH - ROT_DIM), F32)], axis=1)
    return ct, sa, sb


def _prep_body(q_ref, ks_ref, kw_ref, gt_ref, ct_ref, sa_ref, sb_ref, qg_ref, kg1_ref, kg2_ref,
               qn_ref, qr_ref, slc_ref, win_ref, go_ref):
    ct, sa, sb = ct_ref[...], sa_ref[...], sb_ref[...]
    half = ROT_DIM // 2

    def rope(x):
        return x * ct + pltpu.roll(x, A_DH - half, 1) * sa + pltpu.roll(x, half, 1) * sb

    for h in range(A_HEADS):
        sl = slice(h * A_DH, (h + 1) * A_DH)
        x = _rms(q_ref[:, sl], qg_ref[...])
        qn_ref[:, sl] = (x * ATT_SCALE).astype(BF)
        qr_ref[:, sl] = (rope(x) * ATT_SCALE).astype(BF)
    for src, dst, kg in ((ks_ref, slc_ref, kg1_ref), (kw_ref, win_ref, kg2_ref)):
        for g in range(A_KV):
            sl = slice(g * A_DH, (g + 1) * A_DH)
            dst[:, sl] = rope(_rms(src[:, sl], kg[...]))
        dst[:, A_KVW:] = src[:, A_KVW:]
    go_ref[...] = jax.nn.sigmoid(gt_ref[...])


def _nsa_prep(n, tm, z, tabs, tab_blocks, q_gain, k_gain):
    tspec = pl.BlockSpec((tm, LANE), lambda i, j: (i % tab_blocks, 0))
    vspec = _full((1, A_DH))
    return _call(
        _prep_body, (n // tm, 1),
        [_zspec(tm, A_WIDTH, ZQ), _zspec(tm, 2 * A_KVW, ZSLC), _zspec(tm, 2 * A_KVW, ZWIN), _zspec(tm, LANE, ZGATE),
         tspec, tspec, tspec, vspec, vspec, vspec],
        [pl.BlockSpec((tm, A_WIDTH), lambda i, j: (i, 0)), pl.BlockSpec((tm, A_WIDTH), lambda i, j: (i, 0)),
         pl.BlockSpec((tm, 2 * A_KVW), lambda i, j: (i, 0)), pl.BlockSpec((tm, 2 * A_KVW), lambda i, j: (i, 0)),
         pl.BlockSpec((tm, LANE), lambda i, j: (i, 0))],
        [jax.ShapeDtypeStruct((n, A_WIDTH), BF), jax.ShapeDtypeStruct((n, A_WIDTH), BF),
         jax.ShapeDtypeStruct((n, 2 * A_KVW), F32), jax.ShapeDtypeStruct((n, 2 * A_KVW), F32),
         jax.ShapeDtypeStruct((n, LANE), F32)],
        name="nsa_prep")(z, z, z, z, *tabs, q_gain.reshape(1, A_DH), k_gain[1:2], k_gain[2:3])


def _cmp_tail(acc, w2, kg, is_k):
    out = _dot(jax.nn.gelu(acc).astype(BF), w2)
    return jnp.where(is_k, _rms(out, kg), out)


def _compress_body(x_ref, pe_ref, w1_ref, w2_ref, kg_ref, o_ref):
    x = (x_ref[...] + pe_ref[...]).astype(BF)
    o_ref[...] = _cmp_tail(_dot(x, w1_ref[...]), w2_ref[...], kg_ref[...], pl.program_id(0) == 0)


def _compress(x, pe, w1, w2, kg):
    _, m, f = x.shape
    tm = min(m, 256)
    return _call(
        _compress_body, (2, m // tm),
        [pl.BlockSpec((None, tm, f), lambda s, i: (s, i, 0)), pl.BlockSpec((None, 1, f), lambda s, i: (s, 0, 0)),
         pl.BlockSpec((None, f, A_DH), lambda s, i: (s, 0, 0)), pl.BlockSpec((None, A_DH, A_DH), lambda s, i: (s, 0, 0)),
         _full((1, A_DH))],
        pl.BlockSpec((None, tm, A_DH), lambda s, i: (s, i, 0)),
        jax.ShapeDtypeStruct((2, m, A_DH), F32), name="compress")(x, pe.reshape(2, 1, f), w1, w2, kg)


PAST_NB = 2


SUB = 8
CMP_RPT = SUB // (2 * A_KV)
CMP_QUAD = 2 * CMP_RPT


def _cmp_past_body(pt_ref, *refs):
    npg = PAST_LEN // PAGE_SIZE
    pages = refs[:PAST_NB * npg]
    pe_ref, w1_ref, w2_ref, kg_ref, o_ref = refs[PAST_NB * npg:]
    kvr = 2 * A_KV
    bpp = PAGE_SIZE // CMP_BLOCK
    nblk = PAST_NB * (PAST_LEN // CMP_BLOCK)
    rows = nblk * SUB
    acc = jnp.zeros((rows, CMP_QUAD * A_DH), F32)
    for q in range(CMP_BLOCK // CMP_QUAD):
        halves = []
        for half in range(2):
            off = (q * 2 + half) * SUB
            tiles = jnp.stack([pg[c * CMP_BLOCK * kvr + off:c * CMP_BLOCK * kvr + off + SUB, :]
                               for pg in pages for c in range(bpp)], axis=0)
            halves.append((tiles + pe_ref[off:off + SUB, :][None]).reshape(rows, A_DH).astype(BF))
        acc = acc + _dot(jnp.concatenate(halves, axis=1), w1_ref[q])
    j = jnp.bitwise_and(lax.broadcasted_iota(I32, (rows, A_DH), 0), SUB - 1)
    is_v = jnp.bitwise_and(j, kvr - 1) >= A_KV
    blk = jnp.right_shift(j, int(math.log2(kvr))) * 2 + is_v.astype(I32)
    sel = jnp.zeros((rows, A_DH), F32)
    for b in range(CMP_QUAD):
        sel = jnp.where(blk == b, acc[:, b * A_DH:(b + 1) * A_DH], sel)
    hid = jax.nn.gelu(sel + pltpu.roll(sel, rows - kvr, 0)).astype(BF)
    out2 = _dot(hid, w2_ref[...])
    out = jnp.where(is_v, out2[:, A_DH:], _rms(out2[:, :A_DH], kg_ref[...]))
    o_ref[...] = out.reshape(nblk, SUB, A_DH)


def _compress_past(layer, cache, page_table, pe, w1, w2, kg):
    nb, npg = page_table.shape
    ncb = PAST_LEN // CMP_BLOCK
    kvr = 2 * A_KV
    nq = CMP_BLOCK // CMP_QUAD

    def page_spec(i, p):
        return pl.BlockSpec((None, None, PAGE_SIZE * kvr, A_DH), lambda b, pt: (layer, pt[(b * PAST_NB + i) * npg + p], 0, 0))

    pe_flat = jnp.broadcast_to(jnp.transpose(pe, (1, 0, 2))[:, :, None, :], (CMP_BLOCK, 2, A_KV, A_DH)).reshape(CMP_BLOCK * kvr, A_DH)
    w1q = jnp.transpose(w1.reshape(2, nq, 2, CMP_RPT, A_DH, A_DH), (1, 2, 4, 3, 0, 5)).reshape(nq, 2 * A_DH, CMP_QUAD * A_DH)
    w2c = jnp.concatenate([w2[0], w2[1]], axis=1)
    specs = [page_spec(i, p) for i in range(PAST_NB) for p in range(npg)]
    out = _call(
        _cmp_past_body, (nb // PAST_NB,),
        specs + [_full(pe_flat.shape), _full(w1q.shape), _full(w2c.shape), _full((1, A_DH))],
        pl.BlockSpec((PAST_NB * ncb, SUB, A_DH), lambda b, pt: (b, 0, 0)),
        jax.ShapeDtypeStruct((nb * ncb, SUB, A_DH), F32), nsp=1, name="compress_past")(
            page_table.reshape(-1), *([cache] * (PAST_NB * npg)), pe_flat, w1q, w2c, kg)
    return jnp.transpose(out[:, :kvr].reshape(nb, ncb, 2, A_KV, A_DH), (0, 2, 3, 1, 4))


def _select(score, nblk):
    j = lax.broadcasted_iota(I32, score.shape, 0)
    rank = jnp.zeros(score.shape, F32)
    for k in range(nblk):
        rk = score[k:k + 1, :]
        rank = rank + ((rk > score) | ((rk == score) & (k < j))).astype(F32)
    return rank


def _cmpattn_body(q_ref, ck_ref, cv_ref, o_ref, sel_ref):
    qi = pl.program_id(1)
    tq = q_ref.shape[0]
    nc = ck_ref.shape[1]
    ns = nc // 2
    base = qi * tq
    sel_rows = []
    for g in range(A_KV):
        q4 = jnp.concatenate([q_ref[:, (g * A_HPG + h) * A_DH:(g * A_HPG + h + 1) * A_DH] for h in range(A_HPG)], axis=0)
        ckg = ck_ref[g].astype(BF)
        cvg = cv_ref[g].astype(BF)
        pos = base + jnp.bitwise_and(lax.broadcasted_iota(I32, (A_HPG * tq, nc), 0), tq - 1)
        blk_end = (lax.broadcasted_iota(I32, (A_HPG * tq, nc), 1) + 1) * CMP_BLOCK - 1
        p = _masked_softmax(_dot_nt(q4, ckg), blk_end <= pos, -1)
        o = _dot(p.astype(BF), cvg)
        for h in range(A_HPG):
            o_ref[:, (g * A_HPG + h) * A_DH:(g * A_HPG + h + 1) * A_DH] = o[h * tq:(h + 1) * tq]
        ckp = jnp.concatenate([ck_ref[g, pl.ds(0, ns, stride=2), :], ck_ref[g, pl.ds(1, ns, stride=2), :]], axis=0).astype(BF)
        row = lax.broadcasted_iota(I32, (nc, A_HPG * tq), 0)
        blk = jnp.where(row < ns, 2 * row, 2 * (row - ns) + 1)
        post = base + jnp.bitwise_and(lax.broadcasted_iota(I32, (nc, A_HPG * tq), 1), tq - 1)
        pt = _masked_softmax(_dot_nt(ckp, q4), (blk + 1) * CMP_BLOCK - 1 <= post, 0)
        imp = pt[:, 0:tq]
        for h in range(1, A_HPG):
            imp = imp + pt[:, h * tq:(h + 1) * tq]
        imp = imp[:ns] + imp[ns:]
        j = lax.broadcasted_iota(I32, (ns, tq), 0)
        pos2 = base + lax.broadcasted_iota(I32, (ns, tq), 1)
        cur = jnp.right_shift(pos2, int(math.log2(SEL_BLOCK)))
        forced = (j == 0) | (j == cur) | (j == cur - 1)
        score = jnp.where(j * SEL_BLOCK <= pos2, imp + jnp.where(forced, FORCE_BONUS, 0.0), -1e9)
        sel_rows.append((_select(score, ns) < min(N_SEL, ns)).astype(F32))
    pad = jnp.zeros((LANE - A_KV * ns, tq), F32)
    sel_ref[...] = jnp.concatenate(sel_rows + [pad], axis=0).T


def _cmp_attn(nb, t, qn, ck, cv):
    tq = 256
    nc = ck.shape[2]
    cspec = pl.BlockSpec((None, A_KV, nc, A_DH), lambda b, i: (b, 0, 0, 0))
    return _call(
        _cmpattn_body, (nb, t // tq),
        [pl.BlockSpec((tq, A_WIDTH), lambda b, i: (b * (t // tq) + i, 0)), cspec, cspec],
        [pl.BlockSpec((tq, A_WIDTH), lambda b, i: (b * (t // tq) + i, 0)),
         pl.BlockSpec((tq, LANE), lambda b, i: (b * (t // tq) + i, 0))],
        [jax.ShapeDtypeStruct((nb * t, A_WIDTH), F32), jax.ShapeDtypeStruct((nb * t, LANE), F32)],
        name="cmp_attn")(qn, ck, cv)


SLC_SPAN = 512


def _slcwin_body(q_ref, sk_ref, wk_ref, sel_ref, gt_ref, oc_ref, o_ref, os_ref, ow_ref):
    qi = pl.program_id(1)
    tq = q_ref.shape[0]
    t = sk_ref.shape[0]
    ns = t // SEL_BLOCK
    base = qi * tq
    wkeys = WINDOW + tq
    wstart = pl.multiple_of(jnp.maximum(base - WINDOW, 0), tq)
    selb = sel_ref[...].astype(BF)
    qpw = base + jnp.bitwise_and(lax.broadcasted_iota(I32, (A_HPG * tq, wkeys), 0), tq - 1)
    kpw = wstart + lax.broadcasted_iota(I32, (A_HPG * tq, wkeys), 1)
    wmask = (kpw <= qpw) & (kpw > qpw - WINDOW)

    def heads_of(g):
        return jnp.concatenate([q_ref[:, (g * A_HPG + h) * A_DH:(g * A_HPG + h + 1) * A_DH] for h in range(A_HPG)], axis=0)

    def put(ref, g, o):
        for h in range(A_HPG):
            ref[:, (g * A_HPG + h) * A_DH:(g * A_HPG + h + 1) * A_DH] = o[h * tq:(h + 1) * tq]

    for g in range(A_KV):
        ksl = slice(g * A_DH, (g + 1) * A_DH)
        vsl = slice(A_KVW + g * A_DH, A_KVW + (g + 1) * A_DH)
        ew, rw = _masked_exp(_dot_nt(heads_of(g), wk_ref[pl.ds(wstart, wkeys), ksl].astype(BF)), wmask, -1)
        put(ow_ref, g, _dot(ew.astype(BF), wk_ref[pl.ds(wstart, wkeys), vsl].astype(BF)) * rw)

    nspan = t // SLC_SPAN
    for c in range(nspan):
        @pl.when((base + tq - 1) // SLC_SPAN == c)
        def _(c=c):
            nk = (c + 1) * SLC_SPAN
            lrow = lax.broadcasted_iota(I32, (LANE, nk), 0)
            kblk = jnp.right_shift(lax.broadcasted_iota(I32, (LANE, nk), 1), int(math.log2(SEL_BLOCK)))
            qpos = base + jnp.bitwise_and(lax.broadcasted_iota(I32, (A_HPG * tq, nk), 0), tq - 1)
            causal = lax.broadcasted_iota(I32, (A_HPG * tq, nk), 1) <= qpos
            for g in range(A_KV):
                expand = (lrow == g * ns + kblk).astype(BF)
                picked = _dot(selb, expand)
                smask = (jnp.concatenate([picked] * A_HPG, axis=0) > 0.5) & causal
                e, r = _masked_exp(_dot_nt(heads_of(g), sk_ref[0:nk, g * A_DH:(g + 1) * A_DH].astype(BF)), smask, -1)
                put(os_ref, g, _dot(e.astype(BF), sk_ref[0:nk, A_KVW + g * A_DH:A_KVW + (g + 1) * A_DH].astype(BF)) * r)

    _combine_body(gt_ref, oc_ref, os_ref, ow_ref, o_ref)


def _slc_win_attn(nb, t, qr, slc_rows, win_rows, sel, gates, o_cmp):
    tq = 128
    nq = t // tq
    rowspec = pl.BlockSpec((tq, A_WIDTH), lambda b, i: (b * nq + i, 0))
    lanespec = pl.BlockSpec((tq, LANE), lambda b, i: (b * nq + i, 0))
    kvspec = pl.BlockSpec((t, 2 * A_KVW), lambda b, i: (b, 0))
    return _call(
        _slcwin_body, (nb, nq), [rowspec, kvspec, kvspec, lanespec, lanespec, rowspec], rowspec,
        jax.ShapeDtypeStruct((nb * t, A_WIDTH), BF), scratch=[pltpu.VMEM((tq, A_WIDTH), F32)] * 2,
        name="slc_win_attn")(qr, slc_rows, win_rows, sel, gates, o_cmp)


def _combine_body(g_ref, oc_ref, os_ref, ow_ref, o_ref):
    gt = g_ref[...]
    for hd in range(A_HEADS):
        sl = slice(hd * A_DH, (hd + 1) * A_DH)
        acc = (gt[:, hd:hd + 1] * oc_ref[:, sl] + gt[:, A_HEADS + hd:A_HEADS + hd + 1] * os_ref[:, sl]
               + gt[:, 2 * A_HEADS + hd:2 * A_HEADS + hd + 1] * ow_ref[:, sl])
        o_ref[:, sl] = acc.astype(BF)


def _nsa_combine(n, tm, gates, o_cmp, o_slc, o_win):
    spec = pl.BlockSpec((tm, A_WIDTH), lambda i: (i, 0))
    return _call(_combine_body, (n // tm,), [pl.BlockSpec((tm, LANE), lambda i: (i, 0)), spec, spec, spec], spec,
                 jax.ShapeDtypeStruct((n, A_WIDTH), BF), name="nsa_combine")(gates, o_cmp, o_slc, o_win)


CMP_SB = 4


def _cmpattn_s_body(*refs):
    for i in range(CMP_SB):
        _cmpattn_s_one(*(r.at[i] for r in refs))


def _cmpattn_s_one(q_ref, cp_ref, cn_ref, pt_ref, o_ref, idx_ref, blk_ref):
    pos = PAST_LEN
    q = q_ref[...]
    ncp = cp_ref.shape[2]
    ns = ncp // 2
    nrow = ns + 8
    hrow = lax.broadcasted_iota(I32, (A_HEADS, A_DH), 0)
    o_all = jnp.zeros((A_HEADS, A_DH), F32)
    idx_ref[...] = jnp.zeros(idx_ref.shape, I32)
    blk_ref[...] = jnp.zeros(blk_ref.shape, I32)
    new_ok =(ncp + 1) * CMP_BLOCK - 1 <= pos
    for g in range(A_KV):
        ck = cp_ref[0, g].astype(BF)
        cv = cp_ref[1, g].astype(BF)
        ckn = cn_ref[0, g].astype(BF)
        cvn = cn_ref[1, g]
        vp = (lax.broadcasted_iota(I32, (A_HEADS, ncp), 1) + 1) * CMP_BLOCK - 1 <= pos
        vn = (lax.broadcasted_iota(I32, (A_HEADS, 8), 1) == 0) & new_ok
        sp = jnp.where(vp, _dot_nt(q, ck), -jnp.inf)
        sn = jnp.where(vn, _dot_nt(q, ckn), -jnp.inf)
        m = jnp.maximum(jnp.max(sp, axis=-1, keepdims=True), jnp.max(sn, axis=-1, keepdims=True))
        m = _finite_or_zero(m)
        ep = jnp.where(vp, jnp.exp(sp - m), 0.0)
        en = jnp.where(vn, jnp.exp(sn - m), 0.0)
        den = jnp.maximum(jnp.sum(ep, axis=-1, keepdims=True) + jnp.sum(en, axis=-1, keepdims=True), 1e-30)
        o = _dot((ep / den).astype(BF), cv) + (en / den)[:, 0:1] * cvn[0:1, :]
        o_all = jnp.where((hrow >= g * A_HPG) & (hrow < (g + 1) * A_HPG), o, o_all)
        cke = cp_ref[0, g, pl.ds(0, ns, stride=2), :].astype(BF)
        cko = cp_ref[0, g, pl.ds(1, ns, stride=2), :].astype(BF)
        rowe = lax.broadcasted_iota(I32, (ns, A_HEADS), 0)
        ve = (2 * rowe + 1) * CMP_BLOCK - 1 <= pos
        vo = (2 * rowe + 2) * CMP_BLOCK - 1 <= pos
        vnt = (lax.broadcasted_iota(I32, (8, A_HEADS), 0) == 0) & new_ok
        ste = jnp.where(ve, _dot_nt(cke, q), -jnp.inf)
        sto = jnp.where(vo, _dot_nt(cko, q), -jnp.inf)
        stn = jnp.where(vnt, _dot_nt(ckn, q), -jnp.inf)
        mt = jnp.maximum(jnp.maximum(jnp.max(ste, axis=0, keepdims=True), jnp.max(sto, axis=0, keepdims=True)),
                         jnp.max(stn, axis=0, keepdims=True))
        mt = _finite_or_zero(mt)
        ee = jnp.where(ve, jnp.exp(ste - mt), 0.0)
        eo = jnp.where(vo, jnp.exp(sto - mt), 0.0)
        et = jnp.where(vnt, jnp.exp(stn - mt), 0.0)
        dent = jnp.maximum(jnp.sum(ee, axis=0, keepdims=True) + jnp.sum(eo, axis=0, keepdims=True)
                           + jnp.sum(et, axis=0, keepdims=True), 1e-30)
        hlane = lax.broadcasted_iota(I32, (1, A_HEADS), 1)
        ing = (hlane >= g * A_HPG) & (hlane < (g + 1) * A_HPG)

        def imp_of(e):
            return jnp.sum(jnp.where(ing, e / dent, 0.0), axis=1, keepdims=True)

        imp = jnp.concatenate([imp_of(ee) + imp_of(eo), imp_of(et)], axis=0)
        imp = jnp.broadcast_to(imp, (nrow, LANE))
        j = lax.broadcasted_iota(I32, (nrow, LANE), 0)
        cur = pos // SEL_BLOCK
        nsel = ns + 1
        forced = (j == 0) | (j == cur) | (j == cur - 1)
        score = jnp.where(j * SEL_BLOCK <= pos, imp + jnp.where(forced, FORCE_BONUS, 0.0), -1e9)
        score = jnp.where(j < nsel, score, -3e9)
        rank = _select(score, nsel)
        slot = lax.broadcasted_iota(I32, (nrow, LANE), 1)
        hit = (rank == slot.astype(F32)) & (slot < min(N_SEL, nsel)) & (j < nsel)
        ids = jnp.sum(jnp.where(hit, j, 0), axis=0, keepdims=True)
        idx_ref[g:g + 1, :] = ids
        bpp = PAGE_SIZE // SEL_BLOCK
        pidx = jnp.minimum(ids, PAST_LEN // SEL_BLOCK - 1)
        page = jnp.right_shift(pidx, int(math.log2(bpp)))
        phys = jnp.zeros_like(ids)
        for p in range(pt_ref.shape[1]):
            phys = jnp.where(page == p, pt_ref[:, p:p + 1], phys)
        blk_ref[g:g + 1, :] = phys * bpp + jnp.bitwise_and(pidx, bpp - 1)
    o_ref[...] = o_all


def _cmp_attn_sample(q3, ckv_past, ckv_new, page_table):
    nb = q3.shape[0]
    ncp = ckv_past.shape[3]
    npg = page_table.shape[1]
    ids = pl.BlockSpec((CMP_SB, 8, LANE), lambda b: (b, 0, 0))
    return _call(
        _cmpattn_s_body, (nb // CMP_SB,),
        [pl.BlockSpec((CMP_SB, A_HEADS, A_DH), lambda b: (b, 0, 0)),
         pl.BlockSpec((CMP_SB, 2, A_KV, ncp, A_DH), lambda b: (b, 0, 0, 0, 0)),
         pl.BlockSpec((CMP_SB, 2, A_KV, 8, A_DH), lambda b: (b, 0, 0, 0, 0)),
         pl.BlockSpec((CMP_SB, 1, npg), lambda b: (b, 0, 0))],
        [pl.BlockSpec((CMP_SB, A_HEADS, A_DH), lambda b: (b, 0, 0)), ids, ids],
        [jax.ShapeDtypeStruct((nb, A_HEADS, A_DH), F32), jax.ShapeDtypeStruct((nb, 8, LANE), I32),
         jax.ShapeDtypeStruct((nb, 8, LANE), I32)],
        name="cmp_attn_sample")(q3, ckv_past, ckv_new, page_table.reshape(nb, 1, npg))


KV_ROWS = 2 * A_KV


def _flat_attend(q, xb, kmask, sn, vn):
    s = jnp.where(kmask, _dot_nt(q, xb), -jnp.inf)
    m = _finite_or_zero(jnp.maximum(jnp.max(s, axis=-1, keepdims=True), sn))
    e = jnp.where(kmask, jnp.exp(s - m), 0.0)
    en = jnp.exp(sn - m)
    den = jnp.maximum(jnp.sum(e, axis=-1, keepdims=True) + en, 1e-30)
    return _dot(pltpu.roll(e / den, A_KV, 1).astype(BF), xb) + (en / den) * vn


def _slc_s_body(idx_ref, pt_ref, q_ref, *refs):
    blocks = refs[:A_KV * N_SEL]
    new_ref, o_ref = refs[A_KV * N_SEL:]
    b = pl.program_id(0)
    npast = PAST_LEN // SEL_BLOCK
    nfb = SEL_BLOCK * KV_ROWS
    nk = N_SEL * nfb
    q = q_ref[...]
    new = new_ref[...]
    col = lax.broadcasted_iota(I32, (A_HEADS, nk), 1)
    krow = jnp.bitwise_and(col, KV_ROWS - 1)
    slot = jnp.right_shift(col, int(math.log2(nfb)))
    hrow = lax.broadcasted_iota(I32, (A_HEADS, A_DH), 0)
    o_all = jnp.zeros((A_HEADS, A_DH), F32)
    for g in range(A_KV):
        xb = jnp.concatenate([blocks[g * N_SEL + n][...].astype(BF) for n in range(N_SEL)], axis=0)
        past = jnp.zeros((A_HEADS, nk), I32)
        nfresh = jnp.int32(0)
        for n in range(N_SEL):
            is_past = (idx_ref[(b * A_KV + g) * N_SEL + n] < npast).astype(I32)
            past = jnp.where(slot == n, is_past, past)
            nfresh = nfresh + (1 - is_past)
        fresh = jnp.full((A_HEADS, 1), nfresh, I32) > 0
        sn = jnp.where(fresh, jnp.sum(q.astype(F32) * new[g:g + 1, :], axis=-1, keepdims=True), -jnp.inf)
        o = _flat_attend(q, xb, (past > 0) & (krow == g), sn, new[A_KV + g:A_KV + g + 1, :])
        o_all = jnp.where((hrow >= g * A_HPG) & (hrow < (g + 1) * A_HPG), o, o_all)
    o_ref[...] = o_all


def _slc_attn_sample(layer, q3, cache, idx, blk, slc_new):
    nb = q3.shape[0]

    def blk_spec(g, n):
        return pl.BlockSpec((None, None, SEL_BLOCK * KV_ROWS, A_DH),
                            lambda b, idx_ref, blk_ref: (layer, blk_ref[(b * A_KV + g) * N_SEL + n], 0, 0))

    return _call(
        _slc_s_body, (nb,),
        [pl.BlockSpec((None, A_HEADS, A_DH), lambda b, i, p: (b, 0, 0))]
        + [blk_spec(g, n) for g in range(A_KV) for n in range(N_SEL)]
        + [pl.BlockSpec((None, KV_ROWS, A_DH), lambda b, i, p: (b, 0, 0))],
        pl.BlockSpec((None, A_HEADS, A_DH), lambda b, i, p: (b, 0, 0)),
        jax.ShapeDtypeStruct((nb, A_HEADS, A_DH), F32), nsp=2, name="slc_attn_sample")(
            idx, blk, q3, *([cache] * (A_KV * N_SEL)), slc_new)


WIN_SB = 4


def _win_s_body(q_ref, buf_ref, new_ref, *refs):
    o_ref, nb_ref = refs[-2:]
    pos = PAST_LEN
    nf = buf_ref.shape[1]
    wb = nf // KV_ROWS
    col = lax.broadcasted_iota(I32, (A_HEADS, nf), 1)
    hrow = lax.broadcasted_iota(I32, (A_HEADS, nf), 0)
    kpos = PAST_LEN - wb + jnp.right_shift(col, int(math.log2(KV_ROWS)))
    kmask = ((jnp.bitwise_and(col, KV_ROWS - 1) == jnp.right_shift(hrow, int(math.log2(A_HPG))))
             & (kpos <= pos) & (kpos > pos - WINDOW))
    h1 = lax.broadcasted_iota(I32, (A_HEADS, 1), 0)
    for i in range(WIN_SB):
        q = q_ref[i]
        new = new_ref[i]
        qf = q.astype(F32)
        sn = jnp.zeros((A_HEADS, 1), F32)
        vn = jnp.zeros((A_HEADS, A_DH), F32)
        for g in range(A_KV):
            ing = (h1 >= g * A_HPG) & (h1 < (g + 1) * A_HPG)
            sn = jnp.where(ing, jnp.sum(qf * new[g:g + 1, :], axis=-1, keepdims=True), sn)
            vn = jnp.where(ing, new[A_KV + g:A_KV + g + 1, :], vn)
        o_ref[i] = _flat_attend(q, buf_ref[i].astype(BF), kmask, sn, vn)
        nb_ref[i, pl.ds(0, nf - KV_ROWS), :] = buf_ref[i, pl.ds(KV_ROWS, nf - KV_ROWS), :]
        nb_ref[i, pl.ds(nf - KV_ROWS, KV_ROWS), :] = new


def _win_attn_sample(layer, q3, cache, win_new, acc):
    depth, nb, nf, _ = cache.shape
    ins = [pl.BlockSpec((WIN_SB, A_HEADS, A_DH), lambda b: (b, 0, 0)),
           pl.BlockSpec((None, WIN_SB, nf, A_DH), lambda b: (layer, b, 0, 0)),
           pl.BlockSpec((WIN_SB, KV_ROWS, A_DH), lambda b: (b, 0, 0))]
    args = [q3, cache, win_new]
    if acc is not None:
        ins.append(pl.BlockSpec(memory_space=pl.ANY))
        args.append(acc)
    return _call(
        _win_s_body, (nb // WIN_SB,), ins,
        [pl.BlockSpec((WIN_SB, A_HEADS, A_DH), lambda b: (b, 0, 0)),
         pl.BlockSpec((None, WIN_SB, nf, A_DH), lambda b: (layer, b, 0, 0))],
        [jax.ShapeDtypeStruct((nb, A_HEADS, A_DH), F32), jax.ShapeDtypeStruct(cache.shape, F32)],
        name="win_attn_sample", aliases=None if acc is None else {3: 1})(*args)


def _ssm_disc_body(are_ref, aim_ref, ldt_ref, bre_ref, bim_ref, lre_ref, lim_ref, bbre_ref, bbim_ref):
    a_re, a_im = are_ref[...], aim_ref[...]
    dt = jnp.exp(ldt_ref[...])
    mag = jnp.exp(a_re * dt)
    lam_re = mag * jnp.cos(a_im * dt)
    lam_im = mag * jnp.sin(a_im * dt)
    den = a_re * a_re + a_im * a_im
    co_re = ((lam_re - 1.0) * a_re + lam_im * a_im) / den
    co_im = (lam_im * a_re - (lam_re - 1.0) * a_im) / den
    lre_ref[...] = lam_re
    lim_ref[...] = lam_im
    bbre_ref[...] = co_re * bre_ref[...] - co_im * bim_ref[...]
    bbim_ref[...] = co_re * bim_ref[...] + co_im * bre_ref[...]


def _ssm_disc(a_re, a_im, log_dt, b_re, b_im):
    flat = lambda a: a.reshape(1, S_FLAT)
    bt = lambda b: jnp.transpose(b, (2, 0, 1)).reshape(S_CH, S_FLAT)
    ldt = jnp.broadcast_to(log_dt[:, None], (S_GROUPS, S_STATE))
    v = jax.ShapeDtypeStruct((1, S_FLAT), F32)
    m = jax.ShapeDtypeStruct((S_CH, S_FLAT), F32)
    return pl.pallas_call(_ssm_disc_body, out_shape=[v, v, m, m], name="ssm_disc")(
        flat(a_re), flat(a_im), flat(ldt), bt(b_re), bt(b_im))


def _ssm_tail(hr, hi, u, cre, cim, d, wg, bg):
    y = _dot(hr.astype(BF), cre) - _dot(hi.astype(BF), cim) + d * u
    y = jax.nn.gelu(y)
    return (y * jax.nn.sigmoid(_dot(y.astype(BF), wg) + bg)).astype(BF)


def _ssm_body(u_ref, lre_ref, lim_ref, bbre_ref, bbim_ref, cre_ref, cim_ref, d_ref, wg_ref, bg_ref,
              o_ref, st_ref, xre, xim, hre, him):
    @pl.when(pl.program_id(1) == 0)
    def _():
        st_ref[...] = jnp.zeros_like(st_ref)

    tt = u_ref.shape[0]
    u = u_ref[...]
    ub = u.astype(BF)
    xre[...] = _dot(ub, bbre_ref[...])
    xim[...] = _dot(ub, bbim_ref[...])
    lre, lim = lre_ref[...], lim_ref[...]

    def step(t, c):
        hr, hi = c
        nr = lre * hr - lim * hi + xre[pl.ds(t, 1), :]
        ni = lre * hi + lim * hr + xim[pl.ds(t, 1), :]
        hre[pl.ds(t, 1), :] = nr
        him[pl.ds(t, 1), :] = ni
        return nr, ni

    hr, hi = lax.fori_loop(0, tt, step, (st_ref[0:1, :], st_ref[1:2, :]), unroll=4)
    st_ref[0:1, :] = hr
    st_ref[1:2, :] = hi
    o_ref[...] = _ssm_tail(hre[...], him[...], u, cre_ref[...], cim_ref[...], d_ref[...], wg_ref[...], bg_ref[...])


def _ssm_prompt(nb, t, z, sp):
    tt = 256
    nt = t // tt
    cs = [_full(a.shape) for a in sp]
    return _call(
        _ssm_body, (nb, nt),
        [pl.BlockSpec((tt, S_WIDTH), lambda b, i: (b * nt + i, ZSSM // S_WIDTH))] + cs,
        [pl.BlockSpec((tt, S_WIDTH), lambda b, i: (b * nt + i, 0)), pl.BlockSpec((None, 2, S_FLAT), lambda b, i: (b, 0, 0))],
        [jax.ShapeDtypeStruct((nb * t, S_WIDTH), BF), jax.ShapeDtypeStruct((nb, 2, S_FLAT), F32)],
        scratch=[pltpu.VMEM((tt, S_FLAT), F32)] * 4, name="ssm_prompt")(z, *sp)


def _ssm_s_body(u_ref, h0r_ref, h0i_ref, lre_ref, lim_ref, bbre_ref, bbim_ref, cre_ref, cim_ref, d_ref, wg_ref, bg_ref,
                o_ref, hr_ref, hi_ref):
    u = u_ref[...]
    ub = u.astype(BF)
    lre, lim = lre_ref[...], lim_ref[...]
    h0r, h0i = h0r_ref[...], h0i_ref[...]
    hr = _dot(ub, bbre_ref[...]) + (lre * h0r - lim * h0i)
    hi = _dot(ub, bbim_ref[...]) + (lre * h0i + lim * h0r)
    hr_ref[...] = hr
    hi_ref[...] = hi
    o_ref[...] = _ssm_tail(hr, hi, u, cre_ref[...], cim_ref[...], d_ref[...], wg_ref[...], bg_ref[...])


def _ssm_sample(z, h0r, h0i, sp):
    n = z.shape[0]
    st = pl.BlockSpec((n, S_FLAT), lambda i: (0, 0))
    return _call(
        _ssm_s_body, (1,),
        [pl.BlockSpec((n, S_WIDTH), lambda i: (0, ZSSM // S_WIDTH)), st, st] + [_full(a.shape) for a in sp],
        [pl.BlockSpec((n, S_WIDTH), lambda i: (0, 0)), st, st],
        [jax.ShapeDtypeStruct((n, S_WIDTH), BF), jax.ShapeDtypeStruct((n, S_FLAT), F32), jax.ShapeDtypeStruct((n, S_FLAT), F32)],
        name="ssm_sample")(z, h0r, h0i, *sp)


def _rwkv_prep_body(tiles_per_seq, *refs):
    z_refs, refs = refs[:5], refs[5:]
    if tiles_per_seq is None:
        p_refs, refs = refs[:5], refs[5:]
    m_refs, refs = refs[:5], refs[5:]
    w0_ref, w2_ref, a0_ref, a2_ref, g2_ref, kkw_ref, kaw_ref, rk_ref, hs_ref = refs[:9]
    r_ref, w_ref, k_ref, kk_ref, ka_ref, v_ref, g_ref, bonus_ref = refs[9:17]
    carry_refs = refs[17:]

    def mix(i):
        z = z_refs[i][...]
        if tiles_per_seq is None:
            prev = p_refs[i][...]
        else:
            first = jnp.where(pl.program_id(0) % tiles_per_seq == 0, 0.0, carry_refs[i][...])
            prev = jnp.where(lax.broadcasted_iota(I32, z.shape, 0) == 0, first, pltpu.roll(z, 1, 0))
            carry_refs[i][...] = z[z.shape[0] - 1:, :]
        return z + (prev - z) * m_refs[i][...]

    r, k, v = mix(0), mix(1), mix(2)
    wa = mix(3)
    gl = mix(4)
    w = -jax.nn.softplus(-(w0_ref[...] + _dot(jnp.tanh(wa).astype(BF), w2_ref[...]))) - 0.5
    a = jax.nn.sigmoid(a0_ref[...] + _dot(wa.astype(BF), a2_ref[...]))
    hs = hs_ref[...]
    kk = k * kkw_ref[...]
    kk = kk / jnp.maximum(jnp.sqrt(_dot(kk * kk, hs, HI)), 1e-12)
    k = k * (1.0 + (a - 1.0) * kaw_ref[...])
    r_ref[...] = r
    w_ref[...] = jnp.exp(-jnp.exp(w))
    k_ref[...] = k
    kk_ref[...] = kk
    ka_ref[...] = kk * a
    v_ref[...] = v
    g_ref[...] = _dot(jax.nn.sigmoid(gl).astype(BF), g2_ref[...])
    bonus_ref[...] = _dot(r * k * rk_ref[...], hs, HI) * v


def _rwkv_prep(n, tm, z, prev, mu, rp, seq=None):
    def spec(rows, width, off):
        assert off % width == 0
        return pl.BlockSpec((rows, width), (lambda i: (i, off // width)) if rows == tm else (lambda i: (0, off // width)))

    w = R_WIDTH
    offs = [(w, 0), (w, w), (w, 2 * w), (LANE, 3 * w), (LANE, 3 * w + LANE)]
    out = pl.BlockSpec((tm, w), lambda i: (i, 0))
    given = prev is not None
    return _call(
        functools.partial(_rwkv_prep_body, None if given else seq // tm), (n // tm,),
        [spec(tm, wd, ZRW + o) for wd, o in offs] + ([spec(tm, wd, o) for wd, o in offs] if given else [])
        + [spec(1, wd, o) for wd, o in offs] + [_full(a.shape) for a in rp],
        [out] * 8, [jax.ShapeDtypeStruct((n, w), F32)] * 8,
        scratch=[] if given else [pltpu.VMEM((1, wd), F32) for wd, _ in offs], name="rwkv_prep")(
            *([z] * 5), *([prev] * 5 if given else []), *([mu] * 5), *rp)


def _rwkv_scan_body(r_ref, w_ref, k_ref, kk_ref, ka_ref, v_ref, g_ref, bonus_ref, lng_ref, lnb_ref, ha_ref,
                    o_ref, st_ref, rs, ws, ks, kks, kas, vs, os_, obuf, *bufs):
    @pl.when(pl.program_id(1) == 0)
    def _():
        st_ref[...] = jnp.zeros_like(st_ref)

    tt = r_ref.shape[0]
    n = R_DH
    lc = RWKV_CHUNK
    for h in range(R_HEADS):
        sl = slice(h * n, (h + 1) * n)
        for src, dst in ((r_ref, rs), (w_ref, ws), (k_ref, ks), (kk_ref, kks), (ka_ref, kas), (v_ref, vs)):
            dst[h] = src[:, sl]
    eye = (lax.broadcasted_iota(I32, (n, n), 0) == lax.broadcasted_iota(I32, (n, n), 1)).astype(F32)
    m0 = jnp.concatenate([eye, jnp.zeros((lc, n), F32)], axis=0)
    xrow = lax.broadcasted_iota(I32, (2 * lc, lc), 0)
    xcol = lax.broadcasted_iota(I32, (2 * lc, lc), 1)
    xsign = jnp.where(xrow < lc, -1.0, 1.0)
    xkeep = jnp.where(xrow < lc, xrow, xrow - lc) <= xcol
    heads = range(R_HEADS)

    def rows_part(ci, buf):
        ms_ref, cs_ref, gam_ref = buf
        t0 = pl.multiple_of(ci * lc, lc)
        gs = [jnp.ones((1, n), F32)] * R_HEADS
        for h in heads:
            ms_ref[h] = m0
        for s in range(lc):
            t = t0 + s
            for h in heads:
                w = ws[h, pl.ds(t, 1), :]
                c = jnp.sum(ms_ref[h] * kks[h, pl.ds(t, 1), :], axis=1, keepdims=True)
                cs_ref[h, :, s:s + 1] = c
                ms_ref[h] = ms_ref[h] * w - c * kas[h, pl.ds(t, 1), :]
                ms_ref[h, n + s:n + s + 1, :] = ks[h, pl.ds(t, 1), :]
                gs[h] = gs[h] * w
                gam_ref[h, s:s + 1, :] = gs[h]

    def state_part(ci, buf):
        ms_ref, cs_ref, gam_ref = buf
        sl16 = pl.ds(pl.multiple_of(ci * lc, lc), lc)
        part = []
        for h in heads:
            g = gam_ref[h]
            ginv = 1.0 / g
            rt = g * rs[h, sl16, :]
            v16 = vs[h, sl16, :]
            st0 = st_ref[h]
            stack = jnp.concatenate([st0, v16], axis=0)
            x = _dot_nt(jnp.concatenate([kas[h, sl16, :] * ginv, ks[h, sl16, :] * ginv], axis=0), rt, HI)
            x = jnp.where(xkeep, x * xsign, 0.0)
            cst = _dot_tn(cs_ref[h], stack, HI)
            part.append((x, cst, v16, _dot(rt, st0, HI)))
            st_ref[h] = _dot_tn(ms_ref[h], stack, HI)
        for h in heads:
            x, cst, v16, o0 = part[h]
            os_[h, sl16, :] = o0 + _dot_tn(x, jnp.concatenate([cst, v16], axis=0), HI)

    nch = tt // lc
    buf_a, buf_b = bufs[:3], bufs[3:]
    rows_part(0, buf_a)

    def pair(pi, carry):
        c0 = 2 * pi
        rows_part(c0 + 1, buf_b)
        state_part(c0, buf_a)
        rows_part(jnp.minimum(c0 + 2, nch - 1), buf_a)
        state_part(c0 + 1, buf_b)
        return carry

    lax.fori_loop(0, nch // 2, pair, 0)
    for h in range(R_HEADS):
        obuf[:, h * n:(h + 1) * n] = os_[h]
    _rwkv_post_body(obuf, g_ref, bonus_ref, lng_ref, lnb_ref, ha_ref, o_ref)


def _rwkv_scan(nb, t, r, w, k, kk, ka, v, g, bonus, ln_g, ln_b, havg):
    tt = 256
    nt = t // tt
    row = pl.BlockSpec((tt, R_WIDTH), lambda b, i: (b * nt + i, 0))
    vec = _full((1, R_WIDTH))
    return _call(
        _rwkv_scan_body, (nb, nt), [row] * 8 + [vec, vec, _full(havg.shape)],
        [row, pl.BlockSpec((None, R_HEADS, R_DH, R_DH), lambda b, i: (b, 0, 0, 0))],
        [jax.ShapeDtypeStruct((nb * t, R_WIDTH), BF), jax.ShapeDtypeStruct((nb, R_HEADS, R_DH, R_DH), F32)],
        scratch=[pltpu.VMEM((R_HEADS, tt, R_DH), F32)] * 7 + [pltpu.VMEM((tt, R_WIDTH), F32)]
        + [pltpu.VMEM((R_HEADS, R_DH + RWKV_CHUNK, R_DH), F32), pltpu.VMEM((R_HEADS, R_DH + RWKV_CHUNK, RWKV_CHUNK), F32),
           pltpu.VMEM((R_HEADS, RWKV_CHUNK, R_DH), F32)] * 2,
        name="rwkv_scan")(r, w, k, kk, ka, v, g, bonus, ln_g, ln_b, havg)


RWKV_SB = 8
RWKV_HG = 4


def _rwkv_step_body(r_ref, w_ref, k_ref, kk_ref, ka_ref, vt_ref, s_ref, ot_ref, so_ref):
    n = R_DH
    lane = lax.broadcasted_iota(I32, (n, RWKV_SB), 1)
    samples = range(RWKV_SB)
    for h0 in range(0, R_HEADS, RWKV_HG):
        heads = range(h0, h0 + RWKV_HG)
        rem = {(h, s): jnp.sum(s_ref[s, h] * kk_ref[h, s:s + 1, :], axis=1, keepdims=True) for h in heads for s in samples}
        outs = {}
        for h in heads:
            for s in samples:
                st = (s_ref[s, h] * w_ref[h, s:s + 1, :] - rem[h, s] * ka_ref[h, s:s + 1, :]
                      + vt_ref[h * n:(h + 1) * n, s:s + 1] * k_ref[h, s:s + 1, :])
                so_ref[s, h] = st
                outs[h, s] = jnp.sum(st * r_ref[h, s:s + 1, :], axis=1, keepdims=True)
        for h in heads:
            ot = jnp.zeros((n, RWKV_SB), F32)
            for s in samples:
                ot = jnp.where(lane == s, outs[h, s], ot)
            ot_ref[h * n:(h + 1) * n, :] = ot


def _rwkv_step(r, w, k, kk, ka, v, state, layer):
    nb = r.shape[0]
    nblk = nb // RWKV_SB
    vt = jnp.transpose(v.reshape(nblk, RWKV_SB, R_WIDTH), (0, 2, 1))
    r, w, k, kk, ka = (jnp.transpose(a.reshape(nblk, RWKV_SB, R_HEADS, R_DH), (0, 2, 1, 3)) for a in (r, w, k, kk, ka))
    row = pl.BlockSpec((None, R_HEADS, RWKV_SB, R_DH), lambda i: (i, 0, 0, 0))
    col = pl.BlockSpec((None, R_WIDTH, RWKV_SB), lambda i: (i, 0, 0))
    sts = pl.BlockSpec((RWKV_SB, R_HEADS, R_DH, R_DH), lambda i: (i, 0, 0, 0))
    sti = pl.BlockSpec((None, RWKV_SB, R_HEADS, R_DH, R_DH), lambda i: (layer, i, 0, 0, 0))
    ot, so = _call(
        _rwkv_step_body, (nblk,), [row] * 5 + [col, sti], [col, sts],
        [jax.ShapeDtypeStruct((nblk, R_WIDTH, RWKV_SB), F32), jax.ShapeDtypeStruct(state.shape[1:], F32)],
        name="rwkv_step")(r, w, k, kk, ka, vt, state)
    return jnp.transpose(ot, (0, 2, 1)).reshape(nb, R_WIDTH), so


def _rwkv_post_body(o_ref, g_ref, bonus_ref, lng_ref, lnb_ref, ha_ref, out_ref):
    o = o_ref[...]
    ha = ha_ref[...]
    cen = o - _dot(o, ha, HI)
    var = _dot(cen * cen, ha, HI)
    y = cen * lax.rsqrt(var + GN_EPS) * lng_ref[...] + lnb_ref[...]
    out_ref[...] = ((y + bonus_ref[...]) * g_ref[...]).astype(BF)


def _rwkv_post(n, tm, o, g, bonus, ln_g, ln_b, havg):
    row = pl.BlockSpec((tm, R_WIDTH), lambda i: (i, 0))
    vec = _full((1, R_WIDTH))
    return _call(_rwkv_post_body, (n // tm,), [row, row, row, vec, vec, _full(havg.shape)], row,
                 jax.ShapeDtypeStruct((n, R_WIDTH), BF), name="rwkv_post")(o, g, bonus, ln_g, ln_b, havg)


def _block_diag(blocks):
    g, a, b = blocks.shape
    return jnp.einsum('gab,gh->gahb', blocks, jnp.eye(g, dtype=blocks.dtype)).reshape(g * a, g * b)


class _LayerParams(NamedTuple):
    norm_g: jax.Array
    ffn: tuple
    w_in: tuple
    w_out: jax.Array
    q_gain: jax.Array
    k_gain: jax.Array
    cmp: tuple
    ssm: tuple
    rwkv_mu: jax.Array
    rwkv: tuple
    rwkv_ln: tuple


def _layer_params(l, p):
    w_in_r = (p['w_in_bf16'], l)
    ffn = p['ffn_bf16'] + (l,)
    cmp = (p['nsa_cmp_pe'][l], p['nsa_cmp_w1'][l].astype(BF), p['nsa_cmp_w2'][l].astype(BF), p['nsa_k_gain'][l, 0:1])
    lre, lim, bbre, bbim = _ssm_disc(p['ssm_a_re'][l], p['ssm_a_im'][l], p['ssm_log_dt'][l], p['ssm_b_re'][l], p['ssm_b_im'][l])
    to_gcp = lambda m: jnp.transpose(m.reshape(S_CH, S_GROUPS, S_STATE), (1, 0, 2))
    ssm = (lre, lim, _block_diag(to_gcp(bbre)).astype(BF), _block_diag(to_gcp(bbim)).astype(BF),
           _block_diag(jnp.transpose(p['ssm_c_re'][l], (0, 2, 1))).astype(BF),
           _block_diag(jnp.transpose(p['ssm_c_im'][l], (0, 2, 1))).astype(BF),
           p['ssm_d'][l].reshape(1, S_WIDTH), p['ssm_w_glu'][l].astype(BF), p['ssm_b_glu'][l].reshape(1, S_WIDTH))
    zl = jnp.zeros((R_DECAY_LORA, R_WIDTH), F32)
    vec = lambda a: a.reshape(1, R_WIDTH)
    hsum = _block_diag(jnp.ones((R_HEADS, R_DH, R_DH), F32))
    rwkv = (vec(p['rwkv_w0'][l]), jnp.concatenate([p['rwkv_w2'][l], zl], axis=0).astype(BF),
            vec(p['rwkv_a0'][l]), jnp.concatenate([zl, p['rwkv_a2'][l]], axis=0).astype(BF), p['rwkv_g2'][l].astype(BF),
            vec(p['rwkv_k_k'][l]), vec(p['rwkv_k_a'][l]), vec(p['rwkv_r_k'][l]), hsum)
    return _LayerParams(p['norm_g'][l], ffn, w_in_r, p['w_out'][l].astype(BF), p['nsa_q_gain'][l], p['nsa_k_gain'][l], cmp,
                        ssm, p['rwkv_mu'][l].reshape(1, R_IN), rwkv,
                        (vec(p['rwkv_ln_g'][l]), vec(p['rwkv_ln_b'][l]), hsum / R_DH))


def _layer_prompt(rows, nb, t, x, lp, tabs):
    n = rows.n
    h = _ffn(rows, x, lp.norm_g[0:1], 0, *lp.ffn, 0)
    z = _projin(rows, h, lp.norm_g[1:2], 3, *lp.w_in)
    qn, qr, slc_rows, win_rows, gates = _nsa_prep(n, 256, z, tabs, t // 256, lp.q_gain, lp.k_gain)
    ncb = t // CMP_BLOCK
    cmp_rows = z[:, ZCMP:ZSLC]
    xcmp = jnp.transpose(cmp_rows.reshape(nb, ncb, CMP_BLOCK, 2, A_KV, A_DH), (3, 0, 4, 1, 2, 5))
    ckv = _compress(xcmp.reshape(2, nb * A_KV * ncb, CMP_BLOCK * A_DH), *lp.cmp).reshape(2, nb, A_KV, ncb, A_DH)
    o_cmp, sel = _cmp_attn(nb, t, qn, ckv[0], ckv[1])
    o_nsa = _slc_win_attn(nb, t, qr, slc_rows, win_rows, sel, gates, o_cmp)
    o_ssm, ssm_st = _ssm_prompt(nb, t, z, lp.ssm)
    r, w, k, kk, ka, v, g, bonus = _rwkv_prep(n, 256, z, None, lp.rwkv_mu, lp.rwkv, seq=t)
    o_rwkv, wkv_t = _rwkv_scan(nb, t, r, w, k, kk, ka, v, g, bonus, *lp.rwkv_ln)
    h = _projout(rows, h, o_nsa, o_ssm, o_rwkv, lp.w_out)
    y = _ffn(rows, h, lp.norm_g[2:3], 6, *lp.ffn, 1)
    keep = min(WINDOW, t)
    shape6 = lambda a: a.reshape(nb, -1, 2, A_KV, A_DH)
    state = (shape6(cmp_rows), shape6(slc_rows), shape6(win_rows)[:, t - keep:],
             jnp.stack([ssm_st[:, 0], ssm_st[:, 1]], axis=-1).reshape(nb, S_GROUPS, S_STATE, 2),
             z.reshape(nb, t, ZW)[:, -1, ZRW:ZGATE], jnp.swapaxes(wkv_t, -1, -2))
    return y, state


def _layer_sample(rows, layer, x, lp, tabs, cache_cmp, cache_slc, cache_win, win_acc, page_table, ssm0, shift0, wkv_all):
    n = rows.n
    h = _ffn(rows, x, lp.norm_g[0:1], 0, *lp.ffn, 0)
    z = _projin(rows, h, lp.norm_g[1:2], 3, *lp.w_in)
    qn, qr, slc_new, win_new, gates = _nsa_prep(n, n, z, tabs, 1, lp.q_gain, lp.k_gain)
    cmp_new = z[:, ZCMP:ZSLC]
    ckv_past = _compress_past(layer, cache_cmp, page_table, *lp.cmp)
    xnew = jnp.transpose(cmp_new.reshape(n, 2, A_KV, A_DH), (1, 0, 2, 3)).reshape(2, n * A_KV, A_DH)
    xnew = jnp.pad(xnew, ((0, 0), (0, 0), (0, (CMP_BLOCK - 1) * A_DH)))
    ckv_new = _compress(xnew, *lp.cmp).reshape(2, n, A_KV, 1, A_DH)
    ckv_new = jnp.pad(jnp.transpose(ckv_new, (1, 0, 2, 3, 4)), ((0, 0), (0, 0), (0, 0), (0, 7), (0, 0)))
    q3n = qn.reshape(n, A_HEADS, A_DH)
    q3r = qr.reshape(n, A_HEADS, A_DH)
    o_cmp, idx, blk = _cmp_attn_sample(q3n, ckv_past, ckv_new, page_table)
    ids_flat = lambda a: a[:, :A_KV, :N_SEL].reshape(-1)
    o_slc = _slc_attn_sample(layer, q3r, cache_slc, ids_flat(idx), ids_flat(blk), slc_new.reshape(n, KV_ROWS, A_DH))
    o_win, win_acc = _win_attn_sample(layer, q3r, cache_win, win_new.reshape(n, KV_ROWS, A_DH), win_acc)
    flat = lambda a: a.reshape(n, A_WIDTH)
    o_nsa = _nsa_combine(n, n, gates, flat(o_cmp), flat(o_slc), flat(o_win))
    o_ssm, hr, hi = _ssm_sample(z, ssm0[..., 0].reshape(n, S_FLAT), ssm0[..., 1].reshape(n, S_FLAT), lp.ssm)
    r, w, k, kk, ka, v, g, bonus = _rwkv_prep(n, n, z, shift0, lp.rwkv_mu, lp.rwkv)
    o_step, wkv = _rwkv_step(r, w, k, kk, ka, v, wkv_all, layer)
    o_rwkv = _rwkv_post(n, n, o_step, g, bonus, *lp.rwkv_ln)
    h = _projout(rows, h, o_nsa, o_ssm, o_rwkv, lp.w_out)
    y = _ffn(rows, h, lp.norm_g[2:3], 6, *lp.ffn, 1)
    shape6 = lambda a: a.reshape(n, 1, 2, A_KV, A_DH)
    state = (shape6(cmp_new), shape6(slc_new), None,
             jnp.stack([hr, hi], axis=-1).reshape(n, S_GROUPS, S_STATE, 2), z[:, ZRW:ZGATE], wkv)
    return y, state, win_acc


def kernel(x_prompt, x_sample, cache_nsa_cmp, cache_nsa_slc, cache_nsa_win, state_ssm, state_rwkv_shift, state_rwkv_wkv, page_table, c_prompt, c_sample, norm_g, w_ada, b_ada, ffn_w1, ffn_w3, ffn_w2, w_in, w_out, nsa_q_gain, nsa_k_gain, nsa_cmp_pe, nsa_cmp_w1, nsa_cmp_w2, ssm_a_re, ssm_a_im, ssm_log_dt, ssm_b_re, ssm_b_im, ssm_c_re, ssm_c_im, ssm_d, ssm_w_glu, ssm_b_glu, rwkv_mu, rwkv_w0, rwkv_w2, rwkv_a0, rwkv_a2, rwkv_g2, rwkv_k_k, rwkv_k_a, rwkv_r_k, rwkv_ln_g, rwkv_ln_b):
    p = dict(norm_g=norm_g, ffn_w1=ffn_w1, ffn_w3=ffn_w3, ffn_w2=ffn_w2, w_in=w_in, w_out=w_out, nsa_q_gain=nsa_q_gain,
             nsa_k_gain=nsa_k_gain, nsa_cmp_pe=nsa_cmp_pe, nsa_cmp_w1=nsa_cmp_w1, nsa_cmp_w2=nsa_cmp_w2, ssm_a_re=ssm_a_re,
             ssm_a_im=ssm_a_im, ssm_log_dt=ssm_log_dt, ssm_b_re=ssm_b_re, ssm_b_im=ssm_b_im, ssm_c_re=ssm_c_re,
             ssm_c_im=ssm_c_im, ssm_d=ssm_d, ssm_w_glu=ssm_w_glu, ssm_b_glu=ssm_b_glu, rwkv_mu=rwkv_mu, rwkv_w0=rwkv_w0,
             rwkv_w2=rwkv_w2, rwkv_a0=rwkv_a0, rwkv_a2=rwkv_a2, rwkv_g2=rwkv_g2, rwkv_k_k=rwkv_k_k, rwkv_k_a=rwkv_k_a,
             rwkv_r_k=rwkv_r_k, rwkv_ln_g=rwkv_ln_g, rwkv_ln_b=rwkv_ln_b)
    p['ffn_bf16'] = (ffn_w1.astype(BF), ffn_w3.astype(BF), ffn_w2.astype(BF))
    p['w_in_bf16'] = jnp.concatenate(
        [w_in[:, :, :OFF_GATE], w_in[:, :, OFF_SSM:], w_in[:, :, OFF_GATE:OFF_SSM],
         jnp.zeros(w_in.shape[:2] + (ZW - ZGATE - 3 * A_HEADS,), F32)], axis=2).astype(BF)
    depth = w_in.shape[0]
    nbp, t, d = x_prompt.shape
    nbs, ts, _ = x_sample.shape
    assert ts == 1 and nbp <= 8 and d == D_MODEL
    cp8 = jnp.pad(c_prompt, ((0, 8 - nbp), (0, 0)))
    mod_p, mod_s = _ada(cp8, c_sample, w_ada, b_ada)
    tabs_p = _rope_tables(jnp.arange(t))
    tabs_s = _rope_tables(jnp.full((nbs,), PAST_LEN))
    hp = x_prompt.reshape(nbp * t, d)
    hs = x_sample.reshape(nbs, d)
    st_p, st_s = [], []
    cmp_flat = cache_nsa_cmp.reshape(cache_nsa_cmp.shape[:2] + (PAGE_SIZE * KV_ROWS, A_DH))
    slc_flat = cache_nsa_slc.reshape(cache_nsa_slc.shape[0], -1, SEL_BLOCK * KV_ROWS, A_DH)
    win_flat = cache_nsa_win.reshape(cache_nsa_win.shape[:2] + (cache_nsa_win.shape[2] * KV_ROWS, A_DH))
    win_acc = None
    for l in range(depth):
        lp = _layer_params(l, p)
        rows_p = _Rows(nbp * t, t, mod_p[l].reshape(8, 1, 9 * d), False, 512)
        rows_s = _Rows(nbs, 1, mod_s[l], True, nbs)
        hp, sp = _layer_prompt(rows_p, nbp, t, hp, lp, tabs_p)
        hs, ss, win_acc = _layer_sample(rows_s, l, hs, lp, tabs_s, cmp_flat, slc_flat, win_flat, win_acc, page_table,
                                        state_ssm[l], state_rwkv_shift[l], state_rwkv_wkv)
        st_p.append(sp)
        st_s.append(ss)
    outs = [hp.reshape(nbp, t, d), hs.reshape(nbs, 1, d)]
    for i in range(6):
        outs.append(jnp.stack([s[i] for s in st_p]))
        outs.append(win_acc.reshape(cache_nsa_win.shape) if i == 2 else jnp.stack([s[i] for s in st_s]))
    return tuple(outs)
```

```python
import functools
import math
from typing import NamedTuple

import jax
import jax.numpy as jnp
from jax import lax
from jax.experimental import pallas as pl
from jax.experimental.pallas import tpu as pltpu

F32 = jnp.float32
BF = jnp.bfloat16
I32 = jnp.int32
HI = lax.Precision.HIGHEST

D_MODEL = 2048
PAST_LEN = 2048
PAGE_SIZE = 128
A_HEADS, A_KV, A_HPG, A_DH = 8, 2, 4, 128
A_WIDTH = A_HEADS * A_DH
A_KVW = A_KV * A_DH
CMP_BLOCK, SEL_BLOCK, N_SEL, WINDOW = 32, 64, 16, 512
ROT_DIM = A_DH // 4
ROPE_THETA = 500000.0
ATT_SCALE = A_DH ** -0.5
FORCE_BONUS = 1.0e4
S_GROUPS, S_CH, S_STATE = 32, 16, 64
S_WIDTH = S_GROUPS * S_CH
S_FLAT = S_GROUPS * S_STATE
R_HEADS, R_DH = 8, 64
R_WIDTH = R_HEADS * R_DH
R_DECAY_LORA, R_A_LORA, R_GATE_LORA = 64, 64, 128
R_IN = 3 * R_WIDTH + R_DECAY_LORA + R_A_LORA + R_GATE_LORA
GN_EPS = 64e-5
OFF_KV = A_WIDTH
OFF_GATE = OFF_KV + 6 * A_KVW
OFF_SSM = OFF_GATE + 3 * A_HEADS
OFF_RWKV = OFF_SSM + S_WIDTH

ZQ, ZCMP, ZSLC, ZWIN, ZSSM, ZRW = 0, 1024, 1536, 2048, 2560, 3072
ZGATE = ZRW + R_IN
ZW = ZGATE + 128
LANE = 128
RWKV_CHUNK = 16


def _dot(a, b, precision=None):
    return jnp.dot(a, b, preferred_element_type=F32, precision=precision)


def _dot_nt(a, b, precision=None):
    return lax.dot_general(a, b, (((1,), (1,)), ((), ())), preferred_element_type=F32, precision=precision)


def _dot_tn(a, b, precision=None):
    return lax.dot_general(a, b, (((0,), (0,)), ((), ())), preferred_element_type=F32, precision=precision)


def _call(body, grid, in_specs, out_specs, out_shape, scratch=(), nsp=0, name=None, aliases=None):
    gs = pltpu.PrefetchScalarGridSpec(num_scalar_prefetch=nsp, grid=grid, in_specs=in_specs, out_specs=out_specs,
                                      scratch_shapes=list(scratch))
    return pl.pallas_call(body, grid_spec=gs, out_shape=out_shape, name=name, input_output_aliases=aliases or {},
                          compiler_params=pltpu.CompilerParams(dimension_semantics=("arbitrary",) * len(grid)))


def _rms(x, g):
    return x * lax.rsqrt(jnp.mean(x * x, axis=-1, keepdims=True) + 1e-6) * g


def _finite_or_zero(m):
    return jnp.where(jnp.abs(m) < jnp.inf, m, 0.0)


def _masked_exp(s, mask, axis):
    s = jnp.where(mask, s, -jnp.inf)
    m = _finite_or_zero(jnp.max(s, axis=axis, keepdims=True))
    e = jnp.exp(s - m)
    return e, 1.0 / jnp.maximum(jnp.sum(e, axis=axis, keepdims=True), 1e-30)


def _masked_softmax(s, mask, axis):
    e, r = _masked_exp(s, mask, axis)
    return e * r


class _Rows(NamedTuple):
    n: int
    seq: int
    mod: jax.Array
    per_row: bool
    tm: int


def _mod_spec(rows, tm, col, width=D_MODEL, jdep=False):
    per = D_MODEL // width
    if rows.per_row:
        return pl.BlockSpec((tm, width), lambda i, j: (i, col * per + (j if jdep else 0)))
    tpb = rows.seq // tm
    return pl.BlockSpec((None, 1, width), lambda i, j: (i // tpb, 0, col * per + (j if jdep else 0)))


def _zspec(tm, width, off):
    assert off % width == 0
    return pl.BlockSpec((tm, width), lambda i, j: (i, off // width))


def _full(shape):
    nd = len(shape)
    return pl.BlockSpec(shape, lambda *a: (0,) * nd)


def _ada_body(cp_ref, cs_ref, w_ref, b_ref, op_ref, os_ref):
    w = w_ref[...].astype(BF)
    b = b_ref[...]
    for c_ref, o_ref in ((cp_ref, op_ref), (cs_ref, os_ref)):
        c = c_ref[...]
        o_ref[...] = _dot((c * jax.nn.sigmoid(c)).astype(BF), w) + b


def _ada(cp8, cs, w_ada, b_ada):
    nl, d, n = w_ada.shape
    tn = 1024
    ns = cs.shape[0]
    return _call(
        _ada_body, (nl, n // tn),
        [_full((8, d)), _full((ns, d)),
         pl.BlockSpec((None, d, tn), lambda l, j: (l, 0, j)),
         pl.BlockSpec((None, 1, tn), lambda l, j: (l, 0, j))],
        [pl.BlockSpec((None, 8, tn), lambda l, j: (l, 0, j)),
         pl.BlockSpec((None, ns, tn), lambda l, j: (l, 0, j))],
        [jax.ShapeDtypeStruct((nl, 8, n), F32), jax.ShapeDtypeStruct((nl, ns, n), F32)],
        name="ada")(cp8, cs, w_ada, b_ada.reshape(nl, 1, n))


def _norm_mod(x, g, scale, shift):
    return _rms(x, g) * (1.0 + scale) + shift


def _ffn_body(x_ref, g_ref, sh_ref, sc_ref, gt_ref, w1_ref, w3_ref, w2_ref, o_ref, xn_ref):
    f = pl.program_id(1)

    @pl.when(f == 0)
    def _():
        xn_ref[...] = _norm_mod(x_ref[...], g_ref[...], sc_ref[...], sh_ref[...]).astype(BF)
        o_ref[...] = jnp.zeros_like(o_ref)

    xn = xn_ref[...]
    h1 = _dot(xn, w1_ref[...])
    h3 = _dot(xn, w3_ref[...])
    o_ref[...] += _dot((h1 * jax.nn.sigmoid(h1) * h3).astype(BF), w2_ref[...])

    @pl.when(f == pl.num_programs(1) - 1)
    def _():
        o_ref[...] = x_ref[...] + 0.5 * gt_ref[...] * o_ref[...]


def _ffn(rows, x, g, col0, w1, w3, w2, layer, which):
    d, ff = w1.shape[2:]
    tm = rows.tm
    tf = 512 if tm <= 512 else 256
    return _call(
        _ffn_body, (rows.n // tm, ff // tf),
        [pl.BlockSpec((tm, d), lambda i, f: (i, 0)), _full((1, d)),
         _mod_spec(rows, tm, col0), _mod_spec(rows, tm, col0 + 1), _mod_spec(rows, tm, col0 + 2),
         pl.BlockSpec((None, None, d, tf), lambda i, f: (layer, which, 0, f)),
         pl.BlockSpec((None, None, d, tf), lambda i, f: (layer, which, 0, f)),
         pl.BlockSpec((None, None, tf, d), lambda i, f: (layer, which, f, 0))],
        pl.BlockSpec((tm, d), lambda i, f: (i, 0)),
        jax.ShapeDtypeStruct((rows.n, d), F32),
        scratch=[pltpu.VMEM((tm, d), BF)], name="ffn")(x, g, rows.mod, rows.mod, rows.mod, w1, w3, w2)


def _projin_body(x_ref, g_ref, sh_ref, sc_ref, w_ref, o_ref, xn_ref):
    @pl.when(pl.program_id(1) == 0)
    def _():
        xn_ref[...] = _norm_mod(x_ref[...], g_ref[...], sc_ref[...], sh_ref[...]).astype(BF)

    o_ref[...] = _dot(xn_ref[...], w_ref[...])


def _projin(rows, x, g, col0, w, layer):
    d, n = w.shape[1:]
    tm, tn = rows.tm, 1664
    return _call(
        _projin_body, (rows.n // tm, n // tn),
        [pl.BlockSpec((tm, d), lambda i, j: (i, 0)), _full((1, d)),
         _mod_spec(rows, tm, col0), _mod_spec(rows, tm, col0 + 1),
         pl.BlockSpec((None, d, tn), lambda i, j: (layer, 0, j))],
        pl.BlockSpec((tm, tn), lambda i, j: (i, j)),
        jax.ShapeDtypeStruct((rows.n, n), F32),
        scratch=[pltpu.VMEM((tm, d), BF)], name="proj_in")(x, g, rows.mod, rows.mod, w)


def _projout_body(h_ref, gt_ref, a_ref, b_ref, c_ref, wa_ref, wb_ref, wc_ref, o_ref):
    mix = _dot(a_ref[...], wa_ref[...]) + _dot(b_ref[...], wb_ref[...]) + _dot(c_ref[...], wc_ref[...])
    o_ref[...] = h_ref[...] + gt_ref[...] * mix


def _projout(rows, h, o_nsa, o_ssm, o_rwkv, w):
    d = h.shape[1]
    tm, tn = rows.tm, d
    nblk = A_WIDTH // S_WIDTH
    return _call(
        _projout_body, (rows.n // tm, d // tn),
        [pl.BlockSpec((tm, tn), lambda i, j: (i, j)), _mod_spec(rows, tm, 5, tn, True),
         pl.BlockSpec((tm, A_WIDTH), lambda i, j: (i, 0)), pl.BlockSpec((tm, S_WIDTH), lambda i, j: (i, 0)),
         pl.BlockSpec((tm, R_WIDTH), lambda i, j: (i, 0)),
         pl.BlockSpec((A_WIDTH, tn), lambda i, j: (0, j)), pl.BlockSpec((S_WIDTH, tn), lambda i, j: (nblk, j)),
         pl.BlockSpec((R_WIDTH, tn), lambda i, j: (nblk + 1, j))],
        pl.BlockSpec((tm, tn), lambda i, j: (i, j)),
        jax.ShapeDtypeStruct((rows.n, d), F32), name="proj_out")(h, rows.mod, o_nsa, o_ssm, o_rwkv, w, w, w)


def _rope_tables(pos):
    half = ROT_DIM // 2
    inv = ROPE_THETA ** (-2.0 * jnp.arange(half, dtype=F32) / ROT_DIM)
    ang = pos.astype(F32)[:, None] * inv[None, :]
    cos, sin = jnp.cos(ang), jnp.sin(ang)
    n = pos.shape[0]
    ct = jnp.concatenate([cos, cos, jnp.ones((n, A_DH - ROT_DIM), F32)], axis=1)
    sa = jnp.concatenate([-sin, jnp.zeros((n, A_DH - half), F32)], axis=1)
    sb = jnp.concatenate([jnp.zeros((n, half), F32), sin, jnp.zeros((n, A_DH - ROT_DIM), F32)], axis=1)
    return ct, sa, sb


def _prep_body(q_ref, ks_ref, kw_ref, gt_ref, ct_ref, sa_ref, sb_ref, qg_ref, kg1_ref, kg2_ref,
               qn_ref, qr_ref, slc_ref, win_ref, go_ref):
    ct, sa, sb = ct_ref[...], sa_ref[...], sb_ref[...]
    half = ROT_DIM // 2

    def rope(x):
        return x * ct + pltpu.roll(x, A_DH - half, 1) * sa + pltpu.roll(x, half, 1) * sb

    for h in range(A_HEADS):
        sl = slice(h * A_DH, (h + 1) * A_DH)
        x = _rms(q_ref[:, sl], qg_ref[...])
        qn_ref[:, sl] = (x * ATT_SCALE).astype(BF)
        qr_ref[:, sl] = (rope(x) * ATT_SCALE).astype(BF)
    for src, dst, kg in ((ks_ref, slc_ref, kg1_ref), (kw_ref, win_ref, kg2_ref)):
        for g in range(A_KV):
            sl = slice(g * A_DH, (g + 1) * A_DH)
            dst[:, sl] = rope(_rms(src[:, sl], kg[...]))
        dst[:, A_KVW:] = src[:, A_KVW:]
    go_ref[...] = jax.nn.sigmoid(gt_ref[...])


def _nsa_prep(n, tm, z, tabs, tab_blocks, q_gain, k_gain):
    tspec = pl.BlockSpec((tm, LANE), lambda i, j: (i % tab_blocks, 0))
    vspec = _full((1, A_DH))
    return _call(
        _prep_body, (n // tm, 1),
        [_zspec(tm, A_WIDTH, ZQ), _zspec(tm, 2 * A_KVW, ZSLC), _zspec(tm, 2 * A_KVW, ZWIN), _zspec(tm, LANE, ZGATE),
         tspec, tspec, tspec, vspec, vspec, vspec],
        [pl.BlockSpec((tm, A_WIDTH), lambda i, j: (i, 0)), pl.BlockSpec((tm, A_WIDTH), lambda i, j: (i, 0)),
         pl.BlockSpec((tm, 2 * A_KVW), lambda i, j: (i, 0)), pl.BlockSpec((tm, 2 * A_KVW), lambda i, j: (i, 0)),
         pl.BlockSpec((tm, LANE), lambda i, j: (i, 0))],
        [jax.ShapeDtypeStruct((n, A_WIDTH), BF), jax.ShapeDtypeStruct((n, A_WIDTH), BF),
         jax.ShapeDtypeStruct((n, 2 * A_KVW), F32), jax.ShapeDtypeStruct((n, 2 * A_KVW), F32),
         jax.ShapeDtypeStruct((n, LANE), F32)],
        name="nsa_prep")(z, z, z, z, *tabs, q_gain.reshape(1, A_DH), k_gain[1:2], k_gain[2:3])


def _cmp_tail(acc, w2, kg, is_k):
    out = _dot(jax.nn.gelu(acc).astype(BF), w2)
    return jnp.where(is_k, _rms(out, kg), out)


def _compress_body(x_ref, pe_ref, w1_ref, w2_ref, kg_ref, o_ref):
    x = (x_ref[...] + pe_ref[...]).astype(BF)
    o_ref[...] = _cmp_tail(_dot(x, w1_ref[...]), w2_ref[...], kg_ref[...], pl.program_id(0) == 0)


def _compress(x, pe, w1, w2, kg):
    _, m, f = x.shape
    tm = min(m, 256)
    return _call(
        _compress_body, (2, m // tm),
        [pl.BlockSpec((None, tm, f), lambda s, i: (s, i, 0)), pl.BlockSpec((None, 1, f), lambda s, i: (s, 0, 0)),
         pl.BlockSpec((None, f, A_DH), lambda s, i: (s, 0, 0)), pl.BlockSpec((None, A_DH, A_DH), lambda s, i: (s, 0, 0)),
         _full((1, A_DH))],
        pl.BlockSpec((None, tm, A_DH), lambda s, i: (s, i, 0)),
        jax.ShapeDtypeStruct((2, m, A_DH), F32), name="compress")(x, pe.reshape(2, 1, f), w1, w2, kg)


PAST_NB = 2


SUB = 8
CMP_RPT = SUB // (2 * A_KV)
CMP_QUAD = 2 * CMP_RPT


def _cmp_past_body(pt_ref, *refs):
    npg = PAST_LEN // PAGE_SIZE
    pages = refs[:PAST_NB * npg]
    pe_ref, w1_ref, w2_ref, kg_ref, o_ref = refs[PAST_NB * npg:]
    kvr = 2 * A_KV
    bpp = PAGE_SIZE // CMP_BLOCK
    nblk = PAST_NB * (PAST_LEN // CMP_BLOCK)
    rows = nblk * SUB
    acc = jnp.zeros((rows, CMP_QUAD * A_DH), F32)
    for q in range(CMP_BLOCK // CMP_QUAD):
        halves = []
        for half in range(2):
            off = (q * 2 + half) * SUB
            tiles = jnp.stack([pg[c * CMP_BLOCK * kvr + off:c * CMP_BLOCK * kvr + off + SUB, :]
                               for pg in pages for c in range(bpp)], axis=0)
            halves.append((tiles + pe_ref[off:off + SUB, :][None]).reshape(rows, A_DH).astype(BF))
        acc = acc + _dot(jnp.concatenate(halves, axis=1), w1_ref[q])
    j = jnp.bitwise_and(lax.broadcasted_iota(I32, (rows, A_DH), 0), SUB - 1)
    is_v = jnp.bitwise_and(j, kvr - 1) >= A_KV
    blk = jnp.right_shift(j, int(math.log2(kvr))) * 2 + is_v.astype(I32)
    sel = jnp.zeros((rows, A_DH), F32)
    for b in range(CMP_QUAD):
        sel = jnp.where(blk == b, acc[:, b * A_DH:(b + 1) * A_DH], sel)
    hid = jax.nn.gelu(sel + pltpu.roll(sel, rows - kvr, 0)).astype(BF)
    out2 = _dot(hid, w2_ref[...])
    out = jnp.where(is_v, out2[:, A_DH:], _rms(out2[:, :A_DH], kg_ref[...]))
    o_ref[...] = out.reshape(nblk, SUB, A_DH)


def _compress_past(layer, cache, page_table, pe, w1, w2, kg):
    nb, npg = page_table.shape
    ncb = PAST_LEN // CMP_BLOCK
    kvr = 2 * A_KV
    nq = CMP_BLOCK // CMP_QUAD

    def page_spec(i, p):
        return pl.BlockSpec((None, None, PAGE_SIZE * kvr, A_DH), lambda b, pt: (layer, pt[(b * PAST_NB + i) * npg + p], 0, 0))

    pe_flat = jnp.broadcast_to(jnp.transpose(pe, (1, 0, 2))[:, :, None, :], (CMP_BLOCK, 2, A_KV, A_DH)).reshape(CMP_BLOCK * kvr, A_DH)
    w1q = jnp.transpose(w1.reshape(2, nq, 2, CMP_RPT, A_DH, A_DH), (1, 2, 4, 3, 0, 5)).reshape(nq, 2 * A_DH, CMP_QUAD * A_DH)
    w2c = jnp.concatenate([w2[0], w2[1]], axis=1)
    specs = [page_spec(i, p) for i in range(PAST_NB) for p in range(npg)]
    out = _call(
        _cmp_past_body, (nb // PAST_NB,),
        specs + [_full(pe_flat.shape), _full(w1q.shape), _full(w2c.shape), _full((1, A_DH))],
        pl.BlockSpec((PAST_NB * ncb, SUB, A_DH), lambda b, pt: (b, 0, 0)),
        jax.ShapeDtypeStruct((nb * ncb, SUB, A_DH), F32), nsp=1, name="compress_past")(
            page_table.reshape(-1), *([cache] * (PAST_NB * npg)), pe_flat, w1q, w2c, kg)
    return jnp.transpose(out[:, :kvr].reshape(nb, ncb, 2, A_KV, A_DH), (0, 2, 3, 1, 4))


def _select(score, nblk):
    j = lax.broadcasted_iota(I32, score.shape, 0)
    rank = jnp.zeros(score.shape, F32)
    for k in range(nblk):
        rk = score[k:k + 1, :]
        rank = rank + ((rk > score) | ((rk == score) & (k < j))).astype(F32)
    return rank


def _cmpattn_body(q_ref, ck_ref, cv_ref, o_ref, sel_ref):
    qi = pl.program_id(1)
    tq = q_ref.shape[0]
    nc = ck_ref.shape[1]
    ns = nc // 2
    base = qi * tq
    sel_rows = []
    for g in range(A_KV):
        q4 = jnp.concatenate([q_ref[:, (g * A_HPG + h) * A_DH:(g * A_HPG + h + 1) * A_DH] for h in range(A_HPG)], axis=0)
        ckg = ck_ref[g].astype(BF)
        cvg = cv_ref[g].astype(BF)
        pos = base + jnp.bitwise_and(lax.broadcasted_iota(I32, (A_HPG * tq, nc), 0), tq - 1)
        blk_end = (lax.broadcasted_iota(I32, (A_HPG * tq, nc), 1) + 1) * CMP_BLOCK - 1
        p = _masked_softmax(_dot_nt(q4, ckg), blk_end <= pos, -1)
        o = _dot(p.astype(BF), cvg)
        for h in range(A_HPG):
            o_ref[:, (g * A_HPG + h) * A_DH:(g * A_HPG + h + 1) * A_DH] = o[h * tq:(h + 1) * tq]
        ckp = jnp.concatenate([ck_ref[g, pl.ds(0, ns, stride=2), :], ck_ref[g, pl.ds(1, ns, stride=2), :]], axis=0).astype(BF)
        row = lax.broadcasted_iota(I32, (nc, A_HPG * tq), 0)
        blk = jnp.where(row < ns, 2 * row, 2 * (row - ns) + 1)
        post = base + jnp.bitwise_and(lax.broadcasted_iota(I32, (nc, A_HPG * tq), 1), tq - 1)
        pt = _masked_softmax(_dot_nt(ckp, q4), (blk + 1) * CMP_BLOCK - 1 <= post, 0)
        imp = pt[:, 0:tq]
        for h in range(1, A_HPG):
            imp = imp + pt[:, h * tq:(h + 1) * tq]
        imp = imp[:ns] + imp[ns:]
        j = lax.broadcasted_iota(I32, (ns, tq), 0)
        pos2 = base + lax.broadcasted_iota(I32, (ns, tq), 1)
        cur = jnp.right_shift(pos2, int(math.log2(SEL_BLOCK)))
        forced = (j == 0) | (j == cur) | (j == cur - 1)
        score = jnp.where(j * SEL_BLOCK <= pos2, imp + jnp.where(forced, FORCE_BONUS, 0.0), -1e9)
        sel_rows.append((_select(score, ns) < min(N_SEL, ns)).astype(F32))
    pad = jnp.zeros((LANE - A_KV * ns, tq), F32)
    sel_ref[...] = jnp.concatenate(sel_rows + [pad], axis=0).T


def _cmp_attn(nb, t, qn, ck, cv):
    tq = 256
    nc = ck.shape[2]
    cspec = pl.BlockSpec((None, A_KV, nc, A_DH), lambda b, i: (b, 0, 0, 0))
    return _call(
        _cmpattn_body, (nb, t // tq),
        [pl.BlockSpec((tq, A_WIDTH), lambda b, i: (b * (t // tq) + i, 0)), cspec, cspec],
        [pl.BlockSpec((tq, A_WIDTH), lambda b, i: (b * (t // tq) + i, 0)),
         pl.BlockSpec((tq, LANE), lambda b, i: (b * (t // tq) + i, 0))],
        [jax.ShapeDtypeStruct((nb * t, A_WIDTH), F32), jax.ShapeDtypeStruct((nb * t, LANE), F32)],
        name="cmp_attn")(qn, ck, cv)


SLC_SPAN = 512


def _slcwin_body(q_ref, sk_ref, wk_ref, sel_ref, gt_ref, oc_ref, o_ref, os_ref, ow_ref):
    qi = pl.program_id(1)
    tq = q_ref.shape[0]
    t = sk_ref.shape[0]
    ns = t // SEL_BLOCK
    base = qi * tq
    wkeys = WINDOW + tq
    wstart = pl.multiple_of(jnp.maximum(base - WINDOW, 0), tq)
    selb = sel_ref[...].astype(BF)
    qpw = base + jnp.bitwise_and(lax.broadcasted_iota(I32, (A_HPG * tq, wkeys), 0), tq - 1)
    kpw = wstart + lax.broadcasted_iota(I32, (A_HPG * tq, wkeys), 1)
    wmask = (kpw <= qpw) & (kpw > qpw - WINDOW)

    def heads_of(g):
        return jnp.concatenate([q_ref[:, (g * A_HPG + h) * A_DH:(g * A_HPG + h + 1) * A_DH] for h in range(A_HPG)], axis=0)

    def put(ref, g, o):
        for h in range(A_HPG):
            ref[:, (g * A_HPG + h) * A_DH:(g * A_HPG + h + 1) * A_DH] = o[h * tq:(h + 1) * tq]

    for g in range(A_KV):
        ksl = slice(g * A_DH, (g + 1) * A_DH)
        vsl = slice(A_KVW + g * A_DH, A_KVW + (g + 1) * A_DH)
        ew, rw = _masked_exp(_dot_nt(heads_of(g), wk_ref[pl.ds(wstart, wkeys), ksl].astype(BF)), wmask, -1)
        put(ow_ref, g, _dot(ew.astype(BF), wk_ref[pl.ds(wstart, wkeys), vsl].astype(BF)) * rw)

    nspan = t // SLC_SPAN
    for c in range(nspan):
        @pl.when((base + tq - 1) // SLC_SPAN == c)
        def _(c=c):
            nk = (c + 1) * SLC_SPAN
            lrow = lax.broadcasted_iota(I32, (LANE, nk), 0)
            kblk = jnp.right_shift(lax.broadcasted_iota(I32, (LANE, nk), 1), int(math.log2(SEL_BLOCK)))
            qpos = base + jnp.bitwise_and(lax.broadcasted_iota(I32, (A_HPG * tq, nk), 0), tq - 1)
            causal = lax.broadcasted_iota(I32, (A_HPG * tq, nk), 1) <= qpos
            for g in range(A_KV):
                expand = (lrow == g * ns + kblk).astype(BF)
                picked = _dot(selb, expand)
                smask = (jnp.concatenate([picked] * A_HPG, axis=0) > 0.5) & causal
                e, r = _masked_exp(_dot_nt(heads_of(g), sk_ref[0:nk, g * A_DH:(g + 1) * A_DH].astype(BF)), smask, -1)
                put(os_ref, g, _dot(e.astype(BF), sk_ref[0:nk, A_KVW + g * A_DH:A_KVW + (g + 1) * A_DH].astype(BF)) * r)

    _combine_body(gt_ref, oc_ref, os_ref, ow_ref, o_ref)


def _slc_win_attn(nb, t, qr, slc_rows, win_rows, sel, gates, o_cmp):
    tq = 128
    nq = t // tq
    rowspec = pl.BlockSpec((tq, A_WIDTH), lambda b, i: (b * nq + i, 0))
    lanespec = pl.BlockSpec((tq, LANE), lambda b, i: (b * nq + i, 0))
    kvspec = pl.BlockSpec((t, 2 * A_KVW), lambda b, i: (b, 0))
    return _call(
        _slcwin_body, (nb, nq), [rowspec, kvspec, kvspec, lanespec, lanespec, rowspec], rowspec,
        jax.ShapeDtypeStruct((nb * t, A_WIDTH), BF), scratch=[pltpu.VMEM((tq, A_WIDTH), F32)] * 2,
        name="slc_win_attn")(qr, slc_rows, win_rows, sel, gates, o_cmp)


def _combine_body(g_ref, oc_ref, os_ref, ow_ref, o_ref):
    gt = g_ref[...]
    for hd in range(A_HEADS):
        sl = slice(hd * A_DH, (hd + 1) * A_DH)
        acc = (gt[:, hd:hd + 1] * oc_ref[:, sl] + gt[:, A_HEADS + hd:A_HEADS + hd + 1] * os_ref[:, sl]
               + gt[:, 2 * A_HEADS + hd:2 * A_HEADS + hd + 1] * ow_ref[:, sl])
        o_ref[:, sl] = acc.astype(BF)


def _nsa_combine(n, tm, gates, o_cmp, o_slc, o_win):
    spec = pl.BlockSpec((tm, A_WIDTH), lambda i: (i, 0))
    return _call(_combine_body, (n // tm,), [pl.BlockSpec((tm, LANE), lambda i: (i, 0)), spec, spec, spec], spec,
                 jax.ShapeDtypeStruct((n, A_WIDTH), BF), name="nsa_combine")(gates, o_cmp, o_slc, o_win)


CMP_SB = 4


def _cmpattn_s_body(*refs):
    for i in range(CMP_SB):
        _cmpattn_s_one(*(r.at[i] for r in refs))


def _cmpattn_s_one(q_ref, cp_ref, cn_ref, pt_ref, o_ref, idx_ref, blk_ref):
    pos = PAST_LEN
    q = q_ref[...]
    ncp = cp_ref.shape[2]
    ns = ncp // 2
    nrow = ns + 8
    hrow = lax.broadcasted_iota(I32, (A_HEADS, A_DH), 0)
    o_all = jnp.zeros((A_HEADS, A_DH), F32)
    idx_ref[...] = jnp.zeros(idx_ref.shape, I32)
    blk_ref[...] = jnp.zeros(blk_ref.shape, I32)
    new_ok =(ncp + 1) * CMP_BLOCK - 1 <= pos
    for g in range(A_KV):
        ck = cp_ref[0, g].astype(BF)
        cv = cp_ref[1, g].astype(BF)
        ckn = cn_ref[0, g].astype(BF)
        cvn = cn_ref[1, g]
        vp = (lax.broadcasted_iota(I32, (A_HEADS, ncp), 1) + 1) * CMP_BLOCK - 1 <= pos
        vn = (lax.broadcasted_iota(I32, (A_HEADS, 8), 1) == 0) & new_ok
        sp = jnp.where(vp, _dot_nt(q, ck), -jnp.inf)
        sn = jnp.where(vn, _dot_nt(q, ckn), -jnp.inf)
        m = jnp.maximum(jnp.max(sp, axis=-1, keepdims=True), jnp.max(sn, axis=-1, keepdims=True))
        m = _finite_or_zero(m)
        ep = jnp.where(vp, jnp.exp(sp - m), 0.0)
        en = jnp.where(vn, jnp.exp(sn - m), 0.0)
        den = jnp.maximum(jnp.sum(ep, axis=-1, keepdims=True) + jnp.sum(en, axis=-1, keepdims=True), 1e-30)
        o = _dot((ep / den).astype(BF), cv) + (en / den)[:, 0:1] * cvn[0:1, :]
        o_all = jnp.where((hrow >= g * A_HPG) & (hrow < (g + 1) * A_HPG), o, o_all)
        cke = cp_ref[0, g, pl.ds(0, ns, stride=2), :].astype(BF)
        cko = cp_ref[0, g, pl.ds(1, ns, stride=2), :].astype(BF)
        rowe = lax.broadcasted_iota(I32, (ns, A_HEADS), 0)
        ve = (2 * rowe + 1) * CMP_BLOCK - 1 <= pos
        vo = (2 * rowe + 2) * CMP_BLOCK - 1 <= pos
        vnt = (lax.broadcasted_iota(I32, (8, A_HEADS), 0) == 0) & new_ok
        ste = jnp.where(ve, _dot_nt(cke, q), -jnp.inf)
        sto = jnp.where(vo, _dot_nt(cko, q), -jnp.inf)
        stn = jnp.where(vnt, _dot_nt(ckn, q), -jnp.inf)
        mt = jnp.maximum(jnp.maximum(jnp.max(ste, axis=0, keepdims=True), jnp.max(sto, axis=0, keepdims=True)),
                         jnp.max(stn, axis=0, keepdims=True))
        mt = _finite_or_zero(mt)
        ee = jnp.where(ve, jnp.exp(ste - mt), 0.0)
        eo = jnp.where(vo, jnp.exp(sto - mt), 0.0)
        et = jnp.where(vnt, jnp.exp(stn - mt), 0.0)
        dent = jnp.maximum(jnp.sum(ee, axis=0, keepdims=True) + jnp.sum(eo, axis=0, keepdims=True)
                           + jnp.sum(et, axis=0, keepdims=True), 1e-30)
        hlane = lax.broadcasted_iota(I32, (1, A_HEADS), 1)
        ing = (hlane >= g * A_HPG) & (hlane < (g + 1) * A_HPG)

        def imp_of(e):
            return jnp.sum(jnp.where(ing, e / dent, 0.0), axis=1, keepdims=True)

        imp = jnp.concatenate([imp_of(ee) + imp_of(eo), imp_of(et)], axis=0)
        imp = jnp.broadcast_to(imp, (nrow, LANE))
        j = lax.broadcasted_iota(I32, (nrow, LANE), 0)
        cur = pos // SEL_BLOCK
        nsel = ns + 1
        forced = (j == 0) | (j == cur) | (j == cur - 1)
        score = jnp.where(j * SEL_BLOCK <= pos, imp + jnp.where(forced, FORCE_BONUS, 0.0), -1e9)
        score = jnp.where(j < nsel, score, -3e9)
        rank = _select(score, nsel)
        slot = lax.broadcasted_iota(I32, (nrow, LANE), 1)
        hit = (rank == slot.astype(F32)) & (slot < min(N_SEL, nsel)) & (j < nsel)
        ids = jnp.sum(jnp.where(hit, j, 0), axis=0, keepdims=True)
        idx_ref[g:g + 1, :] = ids
        bpp = PAGE_SIZE // SEL_BLOCK
        pidx = jnp.minimum(ids, PAST_LEN // SEL_BLOCK - 1)
        page = jnp.right_shift(pidx, int(math.log2(bpp)))
        phys = jnp.zeros_like(ids)
        for p in range(pt_ref.shape[1]):
            phys = jnp.where(page == p, pt_ref[:, p:p + 1], phys)
        blk_ref[g:g + 1, :] = phys * bpp + jnp.bitwise_and(pidx, bpp - 1)
    o_ref[...] = o_all


def _cmp_attn_sample(q3, ckv_past, ckv_new, page_table):
    nb = q3.shape[0]
    ncp = ckv_past.shape[3]
    npg = page_table.shape[1]
    ids = pl.BlockSpec((CMP_SB, 8, LANE), lambda b: (b, 0, 0))
    return _call(
        _cmpattn_s_body, (nb // CMP_SB,),
        [pl.BlockSpec((CMP_SB, A_HEADS, A_DH), lambda b: (b, 0, 0)),
         pl.BlockSpec((CMP_SB, 2, A_KV, ncp, A_DH), lambda b: (b, 0, 0, 0, 0)),
         pl.BlockSpec((CMP_SB, 2, A_KV, 8, A_DH), lambda b: (b, 0, 0, 0, 0)),
         pl.BlockSpec((CMP_SB, 1, npg), lambda b: (b, 0, 0))],
        [pl.BlockSpec((CMP_SB, A_HEADS, A_DH), lambda b: (b, 0, 0)), ids, ids],
        [jax.ShapeDtypeStruct((nb, A_HEADS, A_DH), F32), jax.ShapeDtypeStruct((nb, 8, LANE), I32),
         jax.ShapeDtypeStruct((nb, 8, LANE), I32)],
        name="cmp_attn_sample")(q3, ckv_past, ckv_new, page_table.reshape(nb, 1, npg))


KV_ROWS = 2 * A_KV


def _flat_attend(q, xb, kmask, sn, vn):
    s = jnp.where(kmask, _dot_nt(q, xb), -jnp.inf)
    m = _finite_or_zero(jnp.maximum(jnp.max(s, axis=-1, keepdims=True), sn))
    e = jnp.where(kmask, jnp.exp(s - m), 0.0)
    en = jnp.exp(sn - m)
    den = jnp.maximum(jnp.sum(e, axis=-1, keepdims=True) + en, 1e-30)
    return _dot(pltpu.roll(e / den, A_KV, 1).astype(BF), xb) + (en / den) * vn


def _slc_s_body(idx_ref, pt_ref, q_ref, *refs):
    blocks = refs[:A_KV * N_SEL]
    new_ref, o_ref = refs[A_KV * N_SEL:]
    b = pl.program_id(0)
    npast = PAST_LEN // SEL_BLOCK
    nfb = SEL_BLOCK * KV_ROWS
    nk = N_SEL * nfb
    q = q_ref[...]
    new = new_ref[...]
    col = lax.broadcasted_iota(I32, (A_HEADS, nk), 1)
    krow = jnp.bitwise_and(col, KV_ROWS - 1)
    slot = jnp.right_shift(col, int(math.log2(nfb)))
    hrow = lax.broadcasted_iota(I32, (A_HEADS, A_DH), 0)
    o_all = jnp.zeros((A_HEADS, A_DH), F32)
    for g in range(A_KV):
        xb = jnp.concatenate([blocks[g * N_SEL + n][...].astype(BF) for n in range(N_SEL)], axis=0)
        past = jnp.zeros((A_HEADS, nk), I32)
        nfresh = jnp.int32(0)
        for n in range(N_SEL):
            is_past = (idx_ref[(b * A_KV + g) * N_SEL + n] < npast).astype(I32)
            past = jnp.where(slot == n, is_past, past)
            nfresh = nfresh + (1 - is_past)
        fresh = jnp.full((A_HEADS, 1), nfresh, I32) > 0
        sn = jnp.where(fresh, jnp.sum(q.astype(F32) * new[g:g + 1, :], axis=-1, keepdims=True), -jnp.inf)
        o = _flat_attend(q, xb, (past > 0) & (krow == g), sn, new[A_KV + g:A_KV + g + 1, :])
        o_all = jnp.where((hrow >= g * A_HPG) & (hrow < (g + 1) * A_HPG), o, o_all)
    o_ref[...] = o_all


def _slc_attn_sample(layer, q3, cache, idx, blk, slc_new):
    nb = q3.shape[0]

    def blk_spec(g, n):
        return pl.BlockSpec((None, None, SEL_BLOCK * KV_ROWS, A_DH),
                            lambda b, idx_ref, blk_ref: (layer, blk_ref[(b * A_KV + g) * N_SEL + n], 0, 0))

    return _call(
        _slc_s_body, (nb,),
        [pl.BlockSpec((None, A_HEADS, A_DH), lambda b, i, p: (b, 0, 0))]
        + [blk_spec(g, n) for g in range(A_KV) for n in range(N_SEL)]
        + [pl.BlockSpec((None, KV_ROWS, A_DH), lambda b, i, p: (b, 0, 0))],
        pl.BlockSpec((None, A_HEADS, A_DH), lambda b, i, p: (b, 0, 0)),
        jax.ShapeDtypeStruct((nb, A_HEADS, A_DH), F32), nsp=2, name="slc_attn_sample")(
            idx, blk, q3, *([cache] * (A_KV * N_SEL)), slc_new)


WIN_SB = 4


def _win_s_body(q_ref, buf_ref, new_ref, *refs):
    o_ref, nb_ref = refs[-2:]
    pos = PAST_LEN
    nf = buf_ref.shape[1]
    wb = nf // KV_ROWS
    col = lax.broadcasted_iota(I32, (A_HEADS, nf), 1)
    hrow = lax.broadcasted_iota(I32, (A_HEADS, nf), 0)
    kpos = PAST_LEN - wb + jnp.right_shift(col, int(math.log2(KV_ROWS)))
    kmask = ((jnp.bitwise_and(col, KV_ROWS - 1) == jnp.right_shift(hrow, int(math.log2(A_HPG))))
             & (kpos <= pos) & (kpos > pos - WINDOW))
    h1 = lax.broadcasted_iota(I32, (A_HEADS, 1), 0)
    for i in range(WIN_SB):
        q = q_ref[i]
        new = new_ref[i]
        qf = q.astype(F32)
        sn = jnp.zeros((A_HEADS, 1), F32)
        vn = jnp.zeros((A_HEADS, A_DH), F32)
        for g in range(A_KV):
            ing = (h1 >= g * A_HPG) & (h1 < (g + 1) * A_HPG)
            sn = jnp.where(ing, jnp.sum(qf * new[g:g + 1, :], axis=-1, keepdims=True), sn)
            vn = jnp.where(ing, new[A_KV + g:A_KV + g + 1, :], vn)
        o_ref[i] = _flat_attend(q, buf_ref[i].astype(BF), kmask, sn, vn)
        nb_ref[i, pl.ds(0, nf - KV_ROWS), :] = buf_ref[i, pl.ds(KV_ROWS, nf - KV_ROWS), :]
        nb_ref[i, pl.ds(nf - KV_ROWS, KV_ROWS), :] = new


def _win_attn_sample(layer, q3, cache, win_new, acc):
    depth, nb, nf, _ = cache.shape
    ins = [pl.BlockSpec((WIN_SB, A_HEADS, A_DH), lambda b: (b, 0, 0)),
           pl.BlockSpec((None, WIN_SB, nf, A_DH), lambda b: (layer, b, 0, 0)),
           pl.BlockSpec((WIN_SB, KV_ROWS, A_DH), lambda b: (b, 0, 0))]
    args = [q3, cache, win_new]
    if acc is not None:
        ins.append(pl.BlockSpec(memory_space=pl.ANY))
        args.append(acc)
    return _call(
        _win_s_body, (nb // WIN_SB,), ins,
        [pl.BlockSpec((WIN_SB, A_HEADS, A_DH), lambda b: (b, 0, 0)),
         pl.BlockSpec((None, WIN_SB, nf, A_DH), lambda b: (layer, b, 0, 0))],
        [jax.ShapeDtypeStruct((nb, A_HEADS, A_DH), F32), jax.ShapeDtypeStruct(cache.shape, F32)],
        name="win_attn_sample", aliases=None if acc is None else {3: 1})(*args)


def _ssm_disc_body(are_ref, aim_ref, ldt_ref, bre_ref, bim_ref, lre_ref, lim_ref, bbre_ref, bbim_ref):
    a_re, a_im = are_ref[...], aim_ref[...]
    dt = jnp.exp(ldt_ref[...])
    mag = jnp.exp(a_re * dt)
    lam_re = mag * jnp.cos(a_im * dt)
    lam_im = mag * jnp.sin(a_im * dt)
    den = a_re * a_re + a_im * a_im
    co_re = ((lam_re - 1.0) * a_re + lam_im * a_im) / den
    co_im = (lam_im * a_re - (lam_re - 1.0) * a_im) / den
    lre_ref[...] = lam_re
    lim_ref[...] = lam_im
    bbre_ref[...] = co_re * bre_ref[...] - co_im * bim_ref[...]
    bbim_ref[...] = co_re * bim_ref[...] + co_im * bre_ref[...]


def _ssm_disc(a_re, a_im, log_dt, b_re, b_im):
    flat = lambda a: a.reshape(1, S_FLAT)
    bt = lambda b: jnp.transpose(b, (2, 0, 1)).reshape(S_CH, S_FLAT)
    ldt = jnp.broadcast_to(log_dt[:, None], (S_GROUPS, S_STATE))
    v = jax.ShapeDtypeStruct((1, S_FLAT), F32)
    m = jax.ShapeDtypeStruct((S_CH, S_FLAT), F32)
    return pl.pallas_call(_ssm_disc_body, out_shape=[v, v, m, m], name="ssm_disc")(
        flat(a_re), flat(a_im), flat(ldt), bt(b_re), bt(b_im))


def _ssm_tail(hr, hi, u, cre, cim, d, wg, bg):
    y = _dot(hr.astype(BF), cre) - _dot(hi.astype(BF), cim) + d * u
    y = jax.nn.gelu(y)
    return (y * jax.nn.sigmoid(_dot(y.astype(BF), wg) + bg)).astype(BF)


def _ssm_body(u_ref, lre_ref, lim_ref, bbre_ref, bbim_ref, cre_ref, cim_ref, d_ref, wg_ref, bg_ref,
              o_ref, st_ref, xre, xim, hre, him):
    @pl.when(pl.program_id(1) == 0)
    def _():
        st_ref[...] = jnp.zeros_like(st_ref)

    tt = u_ref.shape[0]
    u = u_ref[...]
    ub = u.astype(BF)
    xre[...] = _dot(ub, bbre_ref[...])
    xim[...] = _dot(ub, bbim_ref[...])
    lre, lim = lre_ref[...], lim_ref[...]

    def step(t, c):
        hr, hi = c
        nr = lre * hr - lim * hi + xre[pl.ds(t, 1), :]
        ni = lre * hi + lim * hr + xim[pl.ds(t, 1), :]
        hre[pl.ds(t, 1), :] = nr
        him[pl.ds(t, 1), :] = ni
        return nr, ni

    hr, hi = lax.fori_loop(0, tt, step, (st_ref[0:1, :], st_ref[1:2, :]), unroll=4)
    st_ref[0:1, :] = hr
    st_ref[1:2, :] = hi
    o_ref[...] = _ssm_tail(hre[...], him[...], u, cre_ref[...], cim_ref[...], d_ref[...], wg_ref[...], bg_ref[...])


def _ssm_prompt(nb, t, z, sp):
    tt = 256
    nt = t // tt
    cs = [_full(a.shape) for a in sp]
    return _call(
        _ssm_body, (nb, nt),
        [pl.BlockSpec((tt, S_WIDTH), lambda b, i: (b * nt + i, ZSSM // S_WIDTH))] + cs,
        [pl.BlockSpec((tt, S_WIDTH), lambda b, i: (b * nt + i, 0)), pl.BlockSpec((None, 2, S_FLAT), lambda b, i: (b, 0, 0))],
        [jax.ShapeDtypeStruct((nb * t, S_WIDTH), BF), jax.ShapeDtypeStruct((nb, 2, S_FLAT), F32)],
        scratch=[pltpu.VMEM((tt, S_FLAT), F32)] * 4, name="ssm_prompt")(z, *sp)


def _ssm_s_body(u_ref, h0r_ref, h0i_ref, lre_ref, lim_ref, bbre_ref, bbim_ref, cre_ref, cim_ref, d_ref, wg_ref, bg_ref,
                o_ref, hr_ref, hi_ref):
    u = u_ref[...]
    ub = u.astype(BF)
    lre, lim = lre_ref[...], lim_ref[...]
    h0r, h0i = h0r_ref[...], h0i_ref[...]
    hr = _dot(ub, bbre_ref[...]) + (lre * h0r - lim * h0i)
    hi = _dot(ub, bbim_ref[...]) + (lre * h0i + lim * h0r)
    hr_ref[...] = hr
    hi_ref[...] = hi
    o_ref[...] = _ssm_tail(hr, hi, u, cre_ref[...], cim_ref[...], d_ref[...], wg_ref[...], bg_ref[...])


def _ssm_sample(z, h0r, h0i, sp):
    n = z.shape[0]
    st = pl.BlockSpec((n, S_FLAT), lambda i: (0, 0))
    return _call(
        _ssm_s_body, (1,),
        [pl.BlockSpec((n, S_WIDTH), lambda i: (0, ZSSM // S_WIDTH)), st, st] + [_full(a.shape) for a in sp],
        [pl.BlockSpec((n, S_WIDTH), lambda i: (0, 0)), st, st],
        [jax.ShapeDtypeStruct((n, S_WIDTH), BF), jax.ShapeDtypeStruct((n, S_FLAT), F32), jax.ShapeDtypeStruct((n, S_FLAT), F32)],
        name="ssm_sample")(z, h0r, h0i, *sp)


def _rwkv_prep_body(tiles_per_seq, *refs):
    z_refs, refs = refs[:5], refs[5:]
    if tiles_per_seq is None:
        p_refs, refs = refs[:5], refs[5:]
    m_refs, refs = refs[:5], refs[5:]
    w0_ref, w2_ref, a0_ref, a2_ref, g2_ref, kkw_ref, kaw_ref, rk_ref, hs_ref = refs[:9]
    r_ref, w_ref, k_ref, kk_ref, ka_ref, v_ref, g_ref, bonus_ref = refs[9:17]
    carry_refs = refs[17:]

    def mix(i):
        z = z_refs[i][...]
        if tiles_per_seq is None:
            prev = p_refs[i][...]
        else:
            first = jnp.where(pl.program_id(0) % tiles_per_seq == 0, 0.0, carry_refs[i][...])
            prev = jnp.where(lax.broadcasted_iota(I32, z.shape, 0) == 0, first, pltpu.roll(z, 1, 0))
            carry_refs[i][...] = z[z.shape[0] - 1:, :]
        return z + (prev - z) * m_refs[i][...]

    r, k, v = mix(0), mix(1), mix(2)
    wa = mix(3)
    gl = mix(4)
    w = -jax.nn.softplus(-(w0_ref[...] + _dot(jnp.tanh(wa).astype(BF), w2_ref[...]))) - 0.5
    a = jax.nn.sigmoid(a0_ref[...] + _dot(wa.astype(BF), a2_ref[...]))
    hs = hs_ref[...]
    kk = k * kkw_ref[...]
    kk = kk / jnp.maximum(jnp.sqrt(_dot(kk * kk, hs, HI)), 1e-12)
    k = k * (1.0 + (a - 1.0) * kaw_ref[...])
    r_ref[...] = r
    w_ref[...] = jnp.exp(-jnp.exp(w))
    k_ref[...] = k
    kk_ref[...] = kk
    ka_ref[...] = kk * a
    v_ref[...] = v
    g_ref[...] = _dot(jax.nn.sigmoid(gl).astype(BF), g2_ref[...])
    bonus_ref[...] = _dot(r * k * rk_ref[...], hs, HI) * v


def _rwkv_prep(n, tm, z, prev, mu, rp, seq=None):
    def spec(rows, width, off):
        assert off % width == 0
        return pl.BlockSpec((rows, width), (lambda i: (i, off // width)) if rows == tm else (lambda i: (0, off // width)))

    w = R_WIDTH
    offs = [(w, 0), (w, w), (w, 2 * w), (LANE, 3 * w), (LANE, 3 * w + LANE)]
    out = pl.BlockSpec((tm, w), lambda i: (i, 0))
    given = prev is not None
    return _call(
        functools.partial(_rwkv_prep_body, None if given else seq // tm), (n // tm,),
        [spec(tm, wd, ZRW + o) for wd, o in offs] + ([spec(tm, wd, o) for wd, o in offs] if given else [])
        + [spec(1, wd, o) for wd, o in offs] + [_full(a.shape) for a in rp],
        [out] * 8, [jax.ShapeDtypeStruct((n, w), F32)] * 8,
        scratch=[] if given else [pltpu.VMEM((1, wd), F32) for wd, _ in offs], name="rwkv_prep")(
            *([z] * 5), *([prev] * 5 if given else []), *([mu] * 5), *rp)


def _rwkv_scan_body(r_ref, w_ref, k_ref, kk_ref, ka_ref, v_ref, g_ref, bonus_ref, lng_ref, lnb_ref, ha_ref,
                    o_ref, st_ref, rs, ws, ks, kks, kas, vs, os_, obuf, *bufs):
    @pl.when(pl.program_id(1) == 0)
    def _():
        st_ref[...] = jnp.zeros_like(st_ref)

    tt = r_ref.shape[0]
    n = R_DH
    lc = RWKV_CHUNK
    for h in range(R_HEADS):
        sl = slice(h * n, (h + 1) * n)
        for src, dst in ((r_ref, rs), (w_ref, ws), (k_ref, ks), (kk_ref, kks), (ka_ref, kas), (v_ref, vs)):
            dst[h] = src[:, sl]
    eye = (lax.broadcasted_iota(I32, (n, n), 0) == lax.broadcasted_iota(I32, (n, n), 1)).astype(F32)
    m0 = jnp.concatenate([eye, jnp.zeros((lc, n), F32)], axis=0)
    xrow = lax.broadcasted_iota(I32, (2 * lc, lc), 0)
    xcol = lax.broadcasted_iota(I32, (2 * lc, lc), 1)
    xsign = jnp.where(xrow < lc, -1.0, 1.0)
    xkeep = jnp.where(xrow < lc, xrow, xrow - lc) <= xcol
    heads = range(R_HEADS)

    def rows_part(ci, buf):
        ms_ref, cs_ref, gam_ref = buf
        t0 = pl.multiple_of(ci * lc, lc)
        gs = [jnp.ones((1, n), F32)] * R_HEADS
        for h in heads:
            ms_ref[h] = m0
        for s in range(lc):
            t = t0 + s
            for h in heads:
                w = ws[h, pl.ds(t, 1), :]
                c = jnp.sum(ms_ref[h] * kks[h, pl.ds(t, 1), :], axis=1, keepdims=True)
                cs_ref[h, :, s:s + 1] = c
                ms_ref[h] = ms_ref[h] * w - c * kas[h, pl.ds(t, 1), :]
                ms_ref[h, n + s:n + s + 1, :] = ks[h, pl.ds(t, 1), :]
                gs[h] = gs[h] * w
                gam_ref[h, s:s + 1, :] = gs[h]

    def state_part(ci, buf):
        ms_ref, cs_ref, gam_ref = buf
        sl16 = pl.ds(pl.multiple_of(ci * lc, lc), lc)
        part = []
        for h in heads:
            g = gam_ref[h]
            ginv = 1.0 / g
            rt = g * rs[h, sl16, :]
            v16 = vs[h, sl16, :]
            st0 = st_ref[h]
            stack = jnp.concatenate([st0, v16], axis=0)
            x = _dot_nt(jnp.concatenate([kas[h, sl16, :] * ginv, ks[h, sl16, :] * ginv], axis=0), rt, HI)
            x = jnp.where(xkeep, x * xsign, 0.0)
            cst = _dot_tn(cs_ref[h], stack, HI)
            part.append((x, cst, v16, _dot(rt, st0, HI)))
            st_ref[h] = _dot_tn(ms_ref[h], stack, HI)
        for h in heads:
            x, cst, v16, o0 = part[h]
            os_[h, sl16, :] = o0 + _dot_tn(x, jnp.concatenate([cst, v16], axis=0), HI)

    nch = tt // lc
    buf_a, buf_b = bufs[:3], bufs[3:]
    rows_part(0, buf_a)

    def pair(pi, carry):
        c0 = 2 * pi
        rows_part(c0 + 1, buf_b)
        state_part(c0, buf_a)
        rows_part(jnp.minimum(c0 + 2, nch - 1), buf_a)
        state_part(c0 + 1, buf_b)
        return carry

    lax.fori_loop(0, nch // 2, pair, 0)
    for h in range(R_HEADS):
        obuf[:, h * n:(h + 1) * n] = os_[h]
    _rwkv_post_body(obuf, g_ref, bonus_ref, lng_ref, lnb_ref, ha_ref, o_ref)


def _rwkv_scan(nb, t, r, w, k, kk, ka, v, g, bonus, ln_g, ln_b, havg):
    tt = 512
    nt = t // tt
    row = pl.BlockSpec((tt, R_WIDTH), lambda b, i: (b * nt + i, 0))
    vec = _full((1, R_WIDTH))
    return _call(
        _rwkv_scan_body, (nb, nt), [row] * 8 + [vec, vec, _full(havg.shape)],
        [row, pl.BlockSpec((None, R_HEADS, R_DH, R_DH), lambda b, i: (b, 0, 0, 0))],
        [jax.ShapeDtypeStruct((nb * t, R_WIDTH), BF), jax.ShapeDtypeStruct((nb, R_HEADS, R_DH, R_DH), F32)],
        scratch=[pltpu.VMEM((R_HEADS, tt, R_DH), F32)] * 7 + [pltpu.VMEM((tt, R_WIDTH), F32)]
        + [pltpu.VMEM((R_HEADS, R_DH + RWKV_CHUNK, R_DH), F32), pltpu.VMEM((R_HEADS, R_DH + RWKV_CHUNK, RWKV_CHUNK), F32),
           pltpu.VMEM((R_HEADS, RWKV_CHUNK, R_DH), F32)] * 2,
        name="rwkv_scan")(r, w, k, kk, ka, v, g, bonus, ln_g, ln_b, havg)


RWKV_SB = 8
RWKV_HG = 4


def _rwkv_step_body(r_ref, w_ref, k_ref, kk_ref, ka_ref, vt_ref, s_ref, ot_ref, so_ref):
    n = R_DH
    lane = lax.broadcasted_iota(I32, (n, RWKV_SB), 1)
    samples = range(RWKV_SB)
    for h0 in range(0, R_HEADS, RWKV_HG):
        heads = range(h0, h0 + RWKV_HG)
        rem = {(h, s): jnp.sum(s_ref[s, h] * kk_ref[h, s:s + 1, :], axis=1, keepdims=True) for h in heads for s in samples}
        outs = {}
        for h in heads:
            for s in samples:
                st = (s_ref[s, h] * w_ref[h, s:s + 1, :] - rem[h, s] * ka_ref[h, s:s + 1, :]
                      + vt_ref[h * n:(h + 1) * n, s:s + 1] * k_ref[h, s:s + 1, :])
                so_ref[s, h] = st
                outs[h, s] = jnp.sum(st * r_ref[h, s:s + 1, :], axis=1, keepdims=True)
        for h in heads:
            ot = jnp.zeros((n, RWKV_SB), F32)
            for s in samples:
                ot = jnp.where(lane == s, outs[h, s], ot)
            ot_ref[h * n:(h + 1) * n, :] = ot


def _rwkv_step(r, w, k, kk, ka, v, state, layer):
    nb = r.shape[0]
    nblk = nb // RWKV_SB
    vt = jnp.transpose(v.reshape(nblk, RWKV_SB, R_WIDTH), (0, 2, 1))
    r, w, k, kk, ka = (jnp.transpose(a.reshape(nblk, RWKV_SB, R_HEADS, R_DH), (0, 2, 1, 3)) for a in (r, w, k, kk, ka))
    row = pl.BlockSpec((None, R_HEADS, RWKV_SB, R_DH), lambda i: (i, 0, 0, 0))
    col = pl.BlockSpec((None, R_WIDTH, RWKV_SB), lambda i: (i, 0, 0))
    sts = pl.BlockSpec((RWKV_SB, R_HEADS, R_DH, R_DH), lambda i: (i, 0, 0, 0))
    sti = pl.BlockSpec((None, RWKV_SB, R_HEADS, R_DH, R_DH), lambda i: (layer, i, 0, 0, 0))
    ot, so = _call(
        _rwkv_step_body, (nblk,), [row] * 5 + [col, sti], [col, sts],
        [jax.ShapeDtypeStruct((nblk, R_WIDTH, RWKV_SB), F32), jax.ShapeDtypeStruct(state.shape[1:], F32)],
        name="rwkv_step")(r, w, k, kk, ka, vt, state)
    return jnp.transpose(ot, (0, 2, 1)).reshape(nb, R_WIDTH), so


def _rwkv_post_body(o_ref, g_ref, bonus_ref, lng_ref, lnb_ref, ha_ref, out_ref):
    o = o_ref[...]
    ha = ha_ref[...]
    cen = o - _dot(o, ha, HI)
    var = _dot(cen * cen, ha, HI)
    y = cen * lax.rsqrt(var + GN_EPS) * lng_ref[...] + lnb_ref[...]
    out_ref[...] = ((y + bonus_ref[...]) * g_ref[...]).astype(BF)


def _rwkv_post(n, tm, o, g, bonus, ln_g, ln_b, havg):
    row = pl.BlockSpec((tm, R_WIDTH), lambda i: (i, 0))
    vec = _full((1, R_WIDTH))
    return _call(_rwkv_post_body, (n // tm,), [row, row, row, vec, vec, _full(havg.shape)], row,
                 jax.ShapeDtypeStruct((n, R_WIDTH), BF), name="rwkv_post")(o, g, bonus, ln_g, ln_b, havg)


def _block_diag(blocks):
    g, a, b = blocks.shape
    return jnp.einsum('gab,gh->gahb', blocks, jnp.eye(g, dtype=blocks.dtype)).reshape(g * a, g * b)


class _LayerParams(NamedTuple):
    norm_g: jax.Array
    ffn: tuple
    w_in: tuple
    w_out: jax.Array
    q_gain: jax.Array
    k_gain: jax.Array
    cmp: tuple
    ssm: tuple
    rwkv_mu: jax.Array
    rwkv: tuple
    rwkv_ln: tuple


def _layer_params(l, p):
    w_in_r = (p['w_in_bf16'], l)
    ffn = p['ffn_bf16'] + (l,)
    cmp = (p['nsa_cmp_pe'][l], p['nsa_cmp_w1'][l].astype(BF), p['nsa_cmp_w2'][l].astype(BF), p['nsa_k_gain'][l, 0:1])
    lre, lim, bbre, bbim = _ssm_disc(p['ssm_a_re'][l], p['ssm_a_im'][l], p['ssm_log_dt'][l], p['ssm_b_re'][l], p['ssm_b_im'][l])
    to_gcp = lambda m: jnp.transpose(m.reshape(S_CH, S_GROUPS, S_STATE), (1, 0, 2))
    ssm = (lre, lim, _block_diag(to_gcp(bbre)).astype(BF), _block_diag(to_gcp(bbim)).astype(BF),
           _block_diag(jnp.transpose(p['ssm_c_re'][l], (0, 2, 1))).astype(BF),
           _block_diag(jnp.transpose(p['ssm_c_im'][l], (0, 2, 1))).astype(BF),
           p['ssm_d'][l].reshape(1, S_WIDTH), p['ssm_w_glu'][l].astype(BF), p['ssm_b_glu'][l].reshape(1, S_WIDTH))
    zl = jnp.zeros((R_DECAY_LORA, R_WIDTH), F32)
    vec = lambda a: a.reshape(1, R_WIDTH)
    hsum = _block_diag(jnp.ones((R_HEADS, R_DH, R_DH), F32))
    rwkv = (vec(p['rwkv_w0'][l]), jnp.concatenate([p['rwkv_w2'][l], zl], axis=0).astype(BF),
            vec(p['rwkv_a0'][l]), jnp.concatenate([zl, p['rwkv_a2'][l]], axis=0).astype(BF), p['rwkv_g2'][l].astype(BF),
            vec(p['rwkv_k_k'][l]), vec(p['rwkv_k_a'][l]), vec(p['rwkv_r_k'][l]), hsum)
    return _LayerParams(p['norm_g'][l], ffn, w_in_r, p['w_out'][l].astype(BF), p['nsa_q_gain'][l], p['nsa_k_gain'][l], cmp,
                        ssm, p['rwkv_mu'][l].reshape(1, R_IN), rwkv,
                        (vec(p['rwkv_ln_g'][l]), vec(p['rwkv_ln_b'][l]), hsum / R_DH))


def _layer_prompt(rows, nb, t, x, lp, tabs):
    n = rows.n
    h = _ffn(rows, x, lp.norm_g[0:1], 0, *lp.ffn, 0)
    z = _projin(rows, h, lp.norm_g[1:2], 3, *lp.w_in)
    qn, qr, slc_rows, win_rows, gates = _nsa_prep(n, 256, z, tabs, t // 256, lp.q_gain, lp.k_gain)
    ncb = t // CMP_BLOCK
    cmp_rows = z[:, ZCMP:ZSLC]
    xcmp = jnp.transpose(cmp_rows.reshape(nb, ncb, CMP_BLOCK, 2, A_KV, A_DH), (3, 0, 4, 1, 2, 5))
    ckv = _compress(xcmp.reshape(2, nb * A_KV * ncb, CMP_BLOCK * A_DH), *lp.cmp).reshape(2, nb, A_KV, ncb, A_DH)
    o_cmp, sel = _cmp_attn(nb, t, qn, ckv[0], ckv[1])
    o_nsa = _slc_win_attn(nb, t, qr, slc_rows, win_rows, sel, gates, o_cmp)
    o_ssm, ssm_st = _ssm_prompt(nb, t, z, lp.ssm)
    r, w, k, kk, ka, v, g, bonus = _rwkv_prep(n, 256, z, None, lp.rwkv_mu, lp.rwkv, seq=t)
    o_rwkv, wkv_t = _rwkv_scan(nb, t, r, w, k, kk, ka, v, g, bonus, *lp.rwkv_ln)
    h = _projout(rows, h, o_nsa, o_ssm, o_rwkv, lp.w_out)
    y = _ffn(rows, h, lp.norm_g[2:3], 6, *lp.ffn, 1)
    keep = min(WINDOW, t)
    shape6 = lambda a: a.reshape(nb, -1, 2, A_KV, A_DH)
    state = (shape6(cmp_rows), shape6(slc_rows), shape6(win_rows)[:, t - keep:],
             jnp.stack([ssm_st[:, 0], ssm_st[:, 1]], axis=-1).reshape(nb, S_GROUPS, S_STATE, 2),
             z.reshape(nb, t, ZW)[:, -1, ZRW:ZGATE], jnp.swapaxes(wkv_t, -1, -2))
    return y, state


def _layer_sample(rows, layer, x, lp, tabs, cache_cmp, cache_slc, cache_win, win_acc, page_table, ssm0, shift0, wkv_all):
    n = rows.n
    h = _ffn(rows, x, lp.norm_g[0:1], 0, *lp.ffn, 0)
    z = _projin(rows, h, lp.norm_g[1:2], 3, *lp.w_in)
    qn, qr, slc_new, win_new, gates = _nsa_prep(n, n, z, tabs, 1, lp.q_gain, lp.k_gain)
    cmp_new = z[:, ZCMP:ZSLC]
    ckv_past = _compress_past(layer, cache_cmp, page_table, *lp.cmp)
    xnew = jnp.transpose(cmp_new.reshape(n, 2, A_KV, A_DH), (1, 0, 2, 3)).reshape(2, n * A_KV, A_DH)
    xnew = jnp.pad(xnew, ((0, 0), (0, 0), (0, (CMP_BLOCK - 1) * A_DH)))
    ckv_new = _compress(xnew, *lp.cmp).reshape(2, n, A_KV, 1, A_DH)
    ckv_new = jnp.pad(jnp.transpose(ckv_new, (1, 0, 2, 3, 4)), ((0, 0), (0, 0), (0, 0), (0, 7), (0, 0)))
    q3n = qn.reshape(n, A_HEADS, A_DH)
    q3r = qr.reshape(n, A_HEADS, A_DH)
    o_cmp, idx, blk = _cmp_attn_sample(q3n, ckv_past, ckv_new, page_table)
    ids_flat = lambda a: a[:, :A_KV, :N_SEL].reshape(-1)
    o_slc = _slc_attn_sample(layer, q3r, cache_slc, ids_flat(idx), ids_flat(blk), slc_new.reshape(n, KV_ROWS, A_DH))
    o_win, win_acc = _win_attn_sample(layer, q3r, cache_win, win_new.reshape(n, KV_ROWS, A_DH), win_acc)
    flat = lambda a: a.reshape(n, A_WIDTH)
    o_nsa = _nsa_combine(n, n, gates, flat(o_cmp), flat(o_slc), flat(o_win))
    o_ssm, hr, hi = _ssm_sample(z, ssm0[..., 0].reshape(n, S_FLAT), ssm0[..., 1].reshape(n, S_FLAT), lp.ssm)
    r, w, k, kk, ka, v, g, bonus = _rwkv_prep(n, n, z, shift0, lp.rwkv_mu, lp.rwkv)
    o_step, wkv = _rwkv_step(r, w, k, kk, ka, v, wkv_all, layer)
    o_rwkv = _rwkv_post(n, n, o_step, g, bonus, *lp.rwkv_ln)
    h = _projout(rows, h, o_nsa, o_ssm, o_rwkv, lp.w_out)
    y = _ffn(rows, h, lp.norm_g[2:3], 6, *lp.ffn, 1)
    shape6 = lambda a: a.reshape(n, 1, 2, A_KV, A_DH)
    state = (shape6(cmp_new), shape6(slc_new), None,
             jnp.stack([hr, hi], axis=-1).reshape(n, S_GROUPS, S_STATE, 2), z[:, ZRW:ZGATE], wkv)
    return y, state, win_acc


def kernel(x_prompt, x_sample, cache_nsa_cmp, cache_nsa_slc, cache_nsa_win, state_ssm, state_rwkv_shift, state_rwkv_wkv, page_table, c_prompt, c_sample, norm_g, w_ada, b_ada, ffn_w1, ffn_w3, ffn_w2, w_in, w_out, nsa_q_gain, nsa_k_gain, nsa_cmp_pe, nsa_cmp_w1, nsa_cmp_w2, ssm_a_re, ssm_a_im, ssm_log_dt, ssm_b_re, ssm_b_im, ssm_c_re, ssm_c_im, ssm_d, ssm_w_glu, ssm_b_glu, rwkv_mu, rwkv_w0, rwkv_w2, rwkv_a0, rwkv_a2, rwkv_g2, rwkv_k_k, rwkv_k_a, rwkv_r_k, rwkv_ln_g, rwkv_ln_b):
    p = dict(norm_g=norm_g, ffn_w1=ffn_w1, ffn_w3=ffn_w3, ffn_w2=ffn_w2, w_in=w_in, w_out=w_out, nsa_q_gain=nsa_q_gain,
             nsa_k_gain=nsa_k_gain, nsa_cmp_pe=nsa_cmp_pe, nsa_cmp_w1=nsa_cmp_w1, nsa_cmp_w2=nsa_cmp_w2, ssm_a_re=ssm_a_re,
             ssm_a_im=ssm_a_im, ssm_log_dt=ssm_log_dt, ssm_b_re=ssm_b_re, ssm_b_im=ssm_b_im, ssm_c_re=ssm_c_re,
             ssm_c_im=ssm_c_im, ssm_d=ssm_d, ssm_w_glu=ssm_w_glu, ssm_b_glu=ssm_b_glu, rwkv_mu=rwkv_mu, rwkv_w0=rwkv_w0,
             rwkv_w2=rwkv_w2, rwkv_a0=rwkv_a0, rwkv_a2=rwkv_a2, rwkv_g2=rwkv_g2, rwkv_k_k=rwkv_k_k, rwkv_k_a=rwkv_k_a,
             rwkv_r_k=rwkv_r_k, rwkv_ln_g=rwkv_ln_g, rwkv_ln_b=rwkv_ln_b)
    p['ffn_bf16'] = (ffn_w1.astype(BF), ffn_w3.astype(BF), ffn_w2.astype(BF))
    p['w_in_bf16'] = jnp.concatenate(
        [w_in[:, :, :OFF_GATE], w_in[:, :, OFF_SSM:], w_in[:, :, OFF_GATE:OFF_SSM],
         jnp.zeros(w_in.shape[:2] + (ZW - ZGATE - 3 * A_HEADS,), F32)], axis=2).astype(BF)
    depth = w_in.shape[0]
    nbp, t, d = x_prompt.shape
    nbs, ts, _ = x_sample.shape
    assert ts == 1 and nbp <= 8 and d == D_MODEL
    cp8 = jnp.pad(c_prompt, ((0, 8 - nbp), (0, 0)))
    mod_p, mod_s = _ada(cp8, c_sample, w_ada, b_ada)
    tabs_p = _rope_tables(jnp.arange(t))
    tabs_s = _rope_tables(jnp.full((nbs,), PAST_LEN))
    hp = x_prompt.reshape(nbp * t, d)
    hs = x_sample.reshape(nbs, d)
    st_p, st_s = [], []
    cmp_flat = cache_nsa_cmp.reshape(cache_nsa_cmp.shape[:2] + (PAGE_SIZE * KV_ROWS, A_DH))
    slc_flat = cache_nsa_slc.reshape(cache_nsa_slc.shape[0], -1, SEL_BLOCK * KV_ROWS, A_DH)
    win_flat = cache_nsa_win.reshape(cache_nsa_win.shape[:2] + (cache_nsa_win.shape[2] * KV_ROWS, A_DH))
    win_acc = None
    for l in range(depth):
        lp = _layer_params(l, p)
        rows_p = _Rows(nbp * t, t, mod_p[l].reshape(8, 1, 9 * d), False, 512)
        rows_s = _Rows(nbs, 1, mod_s[l], True, nbs)
        hp, sp = _layer_prompt(rows_p, nbp, t, hp, lp, tabs_p)
        hs, ss, win_acc = _layer_sample(rows_s, l, hs, lp, tabs_s, cmp_flat, slc_flat, win_flat, win_acc, page_table,
                                        state_ssm[l], state_rwkv_shift[l], state_rwkv_wkv)
        st_p.append(sp)
        st_s.append(ss)
    outs = [hp.reshape(nbp, t, d), hs.reshape(nbs, 1, d)]
    for i in range(6):
        outs.append(jnp.stack([s[i] for s in st_p]))
        outs.append(win_acc.reshape(cache_nsa_win.shape) if i == 2 else jnp.stack([s[i] for s in st_s]))
    return tuple(outs)
```
